```python
import jax, jax.numpy as jnp
from jax import lax
import numpy as np

D_MODEL = 2048
BATCH = 1
SEQ = 8192
DEPTH = 2
DEC_BATCH = 32
DEC_SEQ = 4
PAST_LEN = 8192
PAGE_SIZE = 128

N_META = 16
HEAD_DIM = 128
ROPE_THETA = 10000.0
EPS = 1e-6
Q_BLOCK = 128
A_HEADS = D_MODEL // (2 * HEAD_DIM)
A_KV_HEADS = 2
IDX_HEADS = 16
IDX_DIM = 64
TOPK_MAX = 256
B_HEADS = 4
B_DV = D_MODEL // (2 * B_HEADS)
B_DK = B_DV // 2
B_GATE_RANK = 16
B_GATE_TAU = 16.0
B_CHUNK = 64
C_HEADS = D_MODEL // HEAD_DIM
C_KV_HEADS = 4
D_FF = 4 * D_MODEL
POOL_SPARE_DIV = 4

A_Q = A_HEADS * HEAD_DIM
A_KV = A_KV_HEADS * HEAD_DIM
IDX_Q = IDX_HEADS * IDX_DIM
B_QK = B_HEADS * B_DK
B_V = B_HEADS * B_DV
SPLIT_EVEN = (A_Q, A_KV, A_KV, IDX_Q, IDX_DIM, IDX_HEADS, B_QK, B_QK, B_V, B_GATE_RANK, B_V)
C_Q = C_HEADS * HEAD_DIM
C_KV = C_KV_HEADS * HEAD_DIM
SPLIT_ODD = (C_Q, C_KV, C_KV)
IDX_SCALE = (IDX_HEADS * IDX_DIM) ** -0.5

kernel_name = 'meta_dsa_gla_stickbreak_decoder_step'


def rms_norm(x, g):
    xf = x.astype(jnp.float32)
    y = xf * lax.rsqrt(jnp.mean(xf * xf, axis=-1, keepdims=True) + EPS)
    return (y * g.astype(jnp.float32)).astype(x.dtype)


def rope(x, pos):
    d = x.shape[-1]
    half = d // 2
    inv = ROPE_THETA ** (-jnp.arange(half, dtype=jnp.float32) * (2.0 / d))
    ang = pos.astype(jnp.float32)[:, None] * inv[None, :]
    cos = jnp.cos(ang)[:, None, :]
    sin = jnp.sin(ang)[:, None, :]
    xf = x.astype(jnp.float32)
    x1, x2 = xf[..., :half], xf[..., half:]
    return jnp.concatenate([x1 * cos - x2 * sin, x2 * cos + x1 * sin], axis=-1).astype(x.dtype)


def split_cols(h, sizes):
    parts, off = [], 0
    for s in sizes:
        parts.append(h[..., off:off + s])
        off += s
    return parts


def to_blocks(a, n_blk):
    pad = n_blk * Q_BLOCK - a.shape[1]
    a = jnp.pad(a, [(0, 0), (0, pad)] + [(0, 0)] * (a.ndim - 2))
    return jnp.moveaxis(a.reshape((a.shape[0], n_blk, Q_BLOCK) + a.shape[2:]), 1, 0)


def from_blocks(a, length):
    a = jnp.moveaxis(a, 0, 1)
    return a.reshape((a.shape[0], -1) + a.shape[3:])[:, :length]


def gather_rows(a, sel):
    return jax.vmap(lambda arr, s: arr[s])(a, sel)


def gather_pages(pool, page_table):
    g = pool[page_table]
    return g.reshape((g.shape[0], g.shape[1] * g.shape[2]) + pool.shape[2:])


def paged_rows(pool, page_table, sel, new_rows):
    page = pool.shape[1]
    past_len = page_table.shape[1] * page
    flat = pool.reshape((-1,) + pool.shape[2:])
    s_past = jnp.minimum(sel, past_len - 1)
    phys_page = jnp.take_along_axis(page_table, (s_past // page).reshape(sel.shape[0], -1), axis=1).reshape(sel.shape)
    from_past = flat[phys_page * page + s_past % page]
    from_new = gather_rows(new_rows, jnp.clip(sel - past_len, 0, new_rows.shape[1] - 1)).astype(from_past.dtype)
    is_past = (sel < past_len).reshape(sel.shape + (1,) * (from_past.ndim - sel.ndim))
    return jnp.where(is_past, from_past, from_new)


def dsa_attend(q, idx_q, idx_w, q_pos, idx_k_all, gather_kv, topk):
    B, Q = q.shape[:2]
    L = idx_k_all.shape[1]
    dots = jnp.einsum('bqhd,bsd->bqhs', idx_q.astype(jnp.float32), idx_k_all.astype(jnp.float32))
    score = jnp.einsum('bqhs,bqh->bqs', jax.nn.relu(dots), idx_w.astype(jnp.float32)) * IDX_SCALE
    admissible = jnp.arange(L)[None, :] <= q_pos[:, None]
    score = jnp.where(admissible[None], score, -jnp.inf)
    _, sel = lax.top_k(score, topk)
    valid = sel <= q_pos[None, :, None]
    k_sel, v_sel = gather_kv(sel)
    qg = q.reshape(B, Q, A_KV_HEADS, A_HEADS // A_KV_HEADS, HEAD_DIM).astype(jnp.float32)
    logits = jnp.einsum('bqgrd,bqkgd->bqgrk', qg, k_sel.astype(jnp.float32)) * (HEAD_DIM ** -0.5)
    logits = jnp.where(valid[:, :, None, None, :], logits, -1e30)
    p = jax.nn.softmax(logits, axis=-1)
    o = jnp.einsum('bqgrk,bqkgd->bqgrd', p, v_sel.astype(jnp.float32))
    return o.reshape(B, Q, A_Q).astype(q.dtype)


def gla_chunk(s, inp):
    q, k, v, g = inp
    C = q.shape[1]
    G = jnp.cumsum(g, axis=1)
    causal = (jnp.arange(C)[:, None] >= jnp.arange(C)[None, :])[None, :, :, None, None]
    diff = G[:, :, None] - G[:, None, :]
    decay = jnp.where(causal, jnp.exp(jnp.where(causal, diff, 0.0)), 0.0)
    att = jnp.einsum('bijhd,bjhd->bhij', q[:, :, None] * decay, k)
    o = jnp.einsum('bhij,bjhv->bihv', att, v) + jnp.einsum('bihd,bhdv->bihv', q * jnp.exp(G), s)
    g_last = G[:, -1]
    s_new = jnp.exp(g_last)[..., None] * s + jnp.einsum('bjhd,bjhv->bhdv', k * jnp.exp(g_last[:, None] - G), v)
    return s_new, o


def gla_prompt(q, k, v, g):
    B, T = q.shape[:2]
    pad = (-T) % B_CHUNK

    def chunks(a):
        a = jnp.pad(a.astype(jnp.float32), [(0, 0), (pad, 0), (0, 0), (0, 0)])
        return jnp.moveaxis(a.reshape((B, -1, B_CHUNK) + a.shape[2:]), 1, 0)

    s0 = jnp.zeros((B, B_HEADS, B_DK, B_DV), jnp.float32)
    s_final, o = lax.scan(gla_chunk, s0, (chunks(q), chunks(k), chunks(v), chunks(g)))
    o = jnp.moveaxis(o, 0, 1).reshape(B, -1, B_HEADS, B_DV)[:, pad:]
    return s_final, o


def gla_out(o, br, b_head_norm):
    B, T = o.shape[:2]
    gate = jax.nn.silu(br.astype(jnp.float32)).reshape(B, T, B_HEADS, B_DV)
    return (rms_norm(o, b_head_norm) * gate).reshape(B, T, B_V).astype(br.dtype)


def stick_breaking(q, q_pos, k_all, v_all):
    B, Q = q.shape[:2]
    L = k_all.shape[1]
    qg = q.reshape(B, Q, C_KV_HEADS, C_HEADS // C_KV_HEADS, HEAD_DIM).astype(jnp.float32)
    z = jnp.einsum('bqgrd,blgd->bgrql', qg, k_all.astype(jnp.float32)) * (HEAD_DIM ** -0.5)
    strict = jnp.arange(L)[None, :] < q_pos[:, None]
    log_keep = jnp.where(strict, jax.nn.log_sigmoid(-z), 0.0)
    suffix = lax.cumsum(log_keep, axis=log_keep.ndim - 1, reverse=True)
    later = jnp.concatenate([suffix[..., 1:], jnp.zeros_like(suffix[..., :1])], axis=-1)
    weight = jnp.where(strict, jnp.exp(jax.nn.log_sigmoid(z) + later), 0.0)
    o = jnp.einsum('bgrql,blgd->bqgrd', weight, v_all.astype(jnp.float32))
    return o.reshape(B, Q, C_Q).astype(q.dtype)


def project_even(xn, pos, w_in0, w_b_gate2, b_b_gate):
    B, T, _ = xn.shape
    aq, ak, av, iq, ik, iw, bq, bk, bv, bglr, br = split_cols(xn @ w_in0, SPLIT_EVEN)
    aq = rope(aq.reshape(B, T, A_HEADS, HEAD_DIM), pos)
    ak = rope(ak.reshape(B, T, A_KV_HEADS, HEAD_DIM), pos)
    av = av.reshape(B, T, A_KV_HEADS, HEAD_DIM)
    iq = rope(iq.reshape(B, T, IDX_HEADS, IDX_DIM), pos)
    ik = rope(ik[:, :, None, :], pos)[:, :, 0]
    bq = bq.reshape(B, T, B_HEADS, B_DK) * (B_DK ** -0.5)
    bk = bk.reshape(B, T, B_HEADS, B_DK)
    bv = bv.reshape(B, T, B_HEADS, B_DV)
    bg = (jax.nn.log_sigmoid((bglr @ w_b_gate2 + b_b_gate).astype(jnp.float32)) / B_GATE_TAU).reshape(B, T, B_HEADS, B_DK)
    return aq, ak, av, iq, ik, iw, bq, bk, bv, bg, br


def project_odd(xn, w_in1):
    B, T, _ = xn.shape
    q, k, v = split_cols(xn @ w_in1, SPLIT_ODD)
    return (q.reshape(B, T, C_HEADS, HEAD_DIM), k.reshape(B, T, C_KV_HEADS, HEAD_DIM), v.reshape(B, T, C_KV_HEADS, HEAD_DIM))


def sq_relu_mlp(x, w1, w2):
    h = jax.nn.relu(x @ w1)
    return (h * h) @ w2


def even_prompt(xn, topk, w_in0, w_b_gate2, b_b_gate, b_head_norm, w_out0):
    B, T, _ = xn.shape
    pos = jnp.arange(T)
    aq, ak, av, iq, ik, iw, bq, bk, bv, bg, br = project_even(xn, pos, w_in0, w_b_gate2, b_b_gate)
    n_blk = -(-T // Q_BLOCK)
    blk_pos = jnp.arange(n_blk * Q_BLOCK).reshape(n_blk, Q_BLOCK)
    gather = lambda sel: (gather_rows(ak, sel), gather_rows(av, sel))

    def block(args):
        q_b, iq_b, iw_b, p_b = args
        return dsa_attend(q_b, iq_b, iw_b, p_b, ik, gather, topk)

    oa = from_blocks(lax.map(block, (to_blocks(aq, n_blk), to_blocks(iq, n_blk), to_blocks(iw, n_blk), blk_pos)), T)
    s_final, ob = gla_prompt(bq, bk, bv, bg)
    y = jnp.concatenate([oa, gla_out(ob, br, b_head_norm)], axis=-1) @ w_out0
    return y, ak, av, ik, s_final.astype(xn.dtype)


def even_sample(xn, cache_a_k, cache_a_v, cache_a_idx_k, state_b, page_table, topk, w_in0, w_b_gate2, b_b_gate, b_head_norm, w_out0):
    past_len = page_table.shape[1] * cache_a_k.shape[1]
    pos = past_len + jnp.arange(xn.shape[1])
    aq, ak, av, iq, ik, iw, bq, bk, bv, bg, br = project_even(xn, pos, w_in0, w_b_gate2, b_b_gate)
    idx_all = jnp.concatenate([gather_pages(cache_a_idx_k, page_table).astype(ik.dtype), ik], axis=1)
    gather = lambda sel: (paged_rows(cache_a_k, page_table, sel, ak), paged_rows(cache_a_v, page_table, sel, av))
    oa = dsa_attend(aq, iq, iw, pos, idx_all, gather, topk)
    f32 = jnp.float32
    s_new, ob = gla_chunk(state_b.astype(f32), (bq.astype(f32), bk.astype(f32), bv.astype(f32), bg.astype(f32)))
    y = jnp.concatenate([oa, gla_out(ob, br, b_head_norm)], axis=-1) @ w_out0
    return y, ak, av, ik, s_new.astype(state_b.dtype)


def odd_prompt(xn, w_in1, w_out1):
    B, T, _ = xn.shape
    q, k, v = project_odd(xn, w_in1)
    n_blk = -(-T // Q_BLOCK)
    blk_pos = jnp.arange(n_blk * Q_BLOCK).reshape(n_blk, Q_BLOCK)

    def block(args):
        q_b, p_b = args
        return stick_breaking(q_b, p_b, k, v)

    o = from_blocks(lax.map(block, (to_blocks(q, n_blk), blk_pos)), T)
    return o @ w_out1, k, v


def odd_sample(xn, cache_c_k, cache_c_v, page_table, w_in1, w_out1):
    past_len = page_table.shape[1] * cache_c_k.shape[1]
    pos = past_len + jnp.arange(xn.shape[1])
    q, k, v = project_odd(xn, w_in1)
    k_all = jnp.concatenate([gather_pages(cache_c_k, page_table).astype(k.dtype), k], axis=1)
    v_all = jnp.concatenate([gather_pages(cache_c_v, page_table).astype(v.dtype), v], axis=1)
    o = stick_breaking(q, pos, k_all, v_all)
    return o @ w_out1, k, v


def setup_inputs(seed: int = 0) -> dict:
    key = jax.random.key(seed)
    ks = jax.random.split(key, 24)

    def nrm(k, shape, scale=1.0):
        return jax.random.normal(k, shape, jnp.float32) * scale

    n_pages = PAST_LEN // PAGE_SIZE
    n_used = DEC_BATCH * n_pages
    n_pool = n_used + max(1, n_used // POOL_SPARE_DIV)
    page_table = jax.random.permutation(ks[0], n_pool)[:n_used].reshape(DEC_BATCH, n_pages).astype(jnp.int32)
    d_in_even = sum(SPLIT_EVEN)
    d_in_odd = sum(SPLIT_ODD)
    return {
        'x_prompt': nrm(ks[1], (BATCH, SEQ, D_MODEL)),
        'x_sample': nrm(ks[2], (DEC_BATCH, DEC_SEQ, D_MODEL)),
        'cache_a_k': nrm(ks[3], (n_pool, PAGE_SIZE, A_KV_HEADS, HEAD_DIM)),
        'cache_a_v': nrm(ks[4], (n_pool, PAGE_SIZE, A_KV_HEADS, HEAD_DIM)),
        'cache_a_idx_k': nrm(ks[5], (n_pool, PAGE_SIZE, IDX_DIM)),
        'state_b': nrm(ks[6], (DEC_BATCH, B_HEADS, B_DK, B_DV)),
        'cache_c_k': nrm(ks[7], (n_pool, PAGE_SIZE, C_KV_HEADS, HEAD_DIM)),
        'cache_c_v': nrm(ks[8], (n_pool, PAGE_SIZE, C_KV_HEADS, HEAD_DIM)),
        'page_table': page_table,
        'meta_tokens': nrm(ks[9], (N_META, D_MODEL)),
        'norm_mix': 1.0 + nrm(ks[10], (DEPTH, D_MODEL), 0.01),
        'norm_ffn': 1.0 + nrm(ks[11], (DEPTH, D_MODEL), 0.01),
        'norm_final': 1.0 + nrm(ks[12], (D_MODEL,), 0.01),
        'w_in0': nrm(ks[13], (D_MODEL, d_in_even), D_MODEL ** -0.5),
        'w_b_gate2': nrm(ks[14], (B_GATE_RANK, B_QK), B_GATE_RANK ** -0.5),
        'b_b_gate': nrm(ks[15], (B_QK,), 0.1),
        'b_head_norm': 1.0 + nrm(ks[16], (B_DV,), 0.01),
        'w_out0': nrm(ks[17], (A_Q + B_V, D_MODEL), (A_Q + B_V) ** -0.5),
        'w_in1': nrm(ks[18], (D_MODEL, d_in_odd), D_MODEL ** -0.5),
        'w_out1': nrm(ks[19], (C_Q, D_MODEL), C_Q ** -0.5),
        'w_ff1': nrm(ks[20], (DEPTH, D_MODEL, D_FF), D_MODEL ** -0.5),
        'w_ff2': nrm(ks[21], (DEPTH, D_FF, D_MODEL), D_FF ** -0.5),
    }


def reference(x_prompt, x_sample, cache_a_k, cache_a_v, cache_a_idx_k, state_b, cache_c_k, cache_c_v, page_table,
              meta_tokens, norm_mix, norm_ffn, norm_final, w_in0, w_b_gate2, b_b_gate, b_head_norm, w_out0,
              w_in1, w_out1, w_ff1, w_ff2):
    B, seq_len, _ = x_prompt.shape
    dec_seq = x_sample.shape[1]
    past_len = page_table.shape[1] * cache_a_k.shape[1]
    topk_prompt = min(TOPK_MAX, seq_len // 4)
    topk_sample = min(TOPK_MAX, (past_len + dec_seq) // 4)
    meta = jnp.broadcast_to(meta_tokens.astype(x_prompt.dtype)[None], (B, N_META, D_MODEL))
    h_p = jnp.concatenate([meta, x_prompt], axis=1)
    h_s = x_sample
    for layer in range(DEPTH):
        xp = rms_norm(h_p, norm_mix[layer])
        xs = rms_norm(h_s, norm_mix[layer])
        if layer % 2 == 0:
            yp, a_k_p, a_v_p, a_idx_p, s_b_p = even_prompt(xp, topk_prompt, w_in0, w_b_gate2, b_b_gate, b_head_norm, w_out0)
            ys, a_k_s, a_v_s, a_idx_s, s_b_s = even_sample(xs, cache_a_k, cache_a_v, cache_a_idx_k, state_b, page_table,
                                                           topk_sample, w_in0, w_b_gate2, b_b_gate, b_head_norm, w_out0)
        else:
            yp, c_k_p, c_v_p = odd_prompt(xp, w_in1, w_out1)
            ys, c_k_s, c_v_s = odd_sample(xs, cache_c_k, cache_c_v, page_table, w_in1, w_out1)
        h_p = h_p + yp
        h_s = h_s + ys
        h_p = h_p + sq_relu_mlp(rms_norm(h_p, norm_ffn[layer]), w_ff1[layer], w_ff2[layer])
        h_s = h_s + sq_relu_mlp(rms_norm(h_s, norm_ffn[layer]), w_ff1[layer], w_ff2[layer])
    y_prompt = rms_norm(h_p, norm_final)[:, N_META:]
    y_sample = rms_norm(h_s, norm_final)
    return (y_prompt, y_sample, a_k_p, a_v_p, a_idx_p, s_b_p, c_k_p, c_v_p, a_k_s, a_v_s, a_idx_s, s_b_s, c_k_s, c_v_s)
```

```python
import functools
import math

import jax
import jax.numpy as jnp
from jax import lax
from jax.experimental import pallas as pl
from jax.experimental.pallas import tpu as pltpu

F32 = jnp.float32
BF16 = jnp.bfloat16
I32 = jnp.int32

N_META = 16
HEAD_DIM = 128
ROPE_THETA = 10000.0
EPS = 1e-6
A_KV_HEADS = 2
IDX_HEADS = 16
IDX_DIM = 64
TOPK_MAX = 256
B_HEADS = 4
B_GATE_RANK = 16
B_GATE_TAU = 16.0
C_KV_HEADS = 4

LANES = 128
SUBLANES = 8
VMEM_LIMIT_BYTES = 56 * 1024 * 1024

ROW_BLOCK = 128
INT_MIN = -(2 ** 31)


def _round_up(x, m):
    return (x + m - 1) // m * m


def _row_tile(m, cap):
    best = ROW_BLOCK
    for k in range(1, m // ROW_BLOCK + 1):
        t = k * ROW_BLOCK
        if t <= cap and m % t == 0:
            best = t
    return best


def _col_tile(n, cap):
    best = LANES
    for k in range(1, n // LANES + 1):
        t = k * LANES
        if t <= cap and n % t == 0:
            best = t
    return best


def _params(*sem):
    return pltpu.CompilerParams(dimension_semantics=sem, vmem_limit_bytes=VMEM_LIMIT_BYTES)


def _rms(x, g):
    ms = jnp.mean(x * x, axis=-1, keepdims=True)
    return x * lax.rsqrt(ms + EPS) * g


def _rmsnorm_kernel(x_ref, g_ref, o_ref):
    o_ref[...] = _rms(x_ref[...], g_ref[...]).astype(o_ref.dtype)


def _rmsnorm(x, g, out_dtype):
    m, d = x.shape
    tm = _row_tile(m, 512)
    return pl.pallas_call(
        _rmsnorm_kernel,
        out_shape=jax.ShapeDtypeStruct((m, d), out_dtype),
        grid=(m // tm,),
        in_specs=[pl.BlockSpec((tm, d), lambda i: (i, 0)), pl.BlockSpec((1, d), lambda i: (0, 0))],
        out_specs=pl.BlockSpec((tm, d), lambda i: (i, 0)),
        compiler_params=_params("parallel"),
        name="rmsnorm",
    )(x, g.reshape(1, d))


def _matmul_kernel(x_ref, w_ref, o_ref):
    o_ref[...] = jnp.dot(x_ref[...], w_ref[...], preferred_element_type=F32).astype(o_ref.dtype)


def _matmul(x, w, out_dtype):
    m, k = x.shape
    n = w.shape[1]
    tm = _row_tile(m, 768)
    tn = _col_tile(n, 1152)
    return pl.pallas_call(
        _matmul_kernel,
        out_shape=jax.ShapeDtypeStruct((m, n), out_dtype),
        grid=(m // tm, n // tn),
        in_specs=[pl.BlockSpec((tm, k), lambda i, j: (i, 0)), pl.BlockSpec((k, tn), lambda i, j: (0, j))],
        out_specs=pl.BlockSpec((tm, tn), lambda i, j: (i, j)),
        compiler_params=_params("parallel", "parallel"),
        name="matmul",
    )(x, w)


def _proj_res_norm_kernel(x_ref, w_ref, h_ref, g_ref, hout_ref, xn_ref):
    hn = h_ref[...] + jnp.dot(x_ref[...], w_ref[...], preferred_element_type=F32)
    hout_ref[...] = hn
    xn_ref[...] = _rms(hn, g_ref[...]).astype(xn_ref.dtype)


def _proj_res_norm(x, w, h, g):
    m, k = x.shape
    d = w.shape[1]
    tm = _row_tile(m, 384)
    return pl.pallas_call(
        _proj_res_norm_kernel,
        out_shape=(jax.ShapeDtypeStruct((m, d), F32), jax.ShapeDtypeStruct((m, d), BF16)),
        grid=(m // tm,),
        in_specs=[pl.BlockSpec((tm, k), lambda i: (i, 0)), pl.BlockSpec((k, d), lambda i: (0, 0)),
                  pl.BlockSpec((tm, d), lambda i: (i, 0)), pl.BlockSpec((1, d), lambda i: (0, 0))],
        out_specs=(pl.BlockSpec((tm, d), lambda i: (i, 0)), pl.BlockSpec((tm, d), lambda i: (i, 0))),
        compiler_params=_params("parallel"),
        name="proj_res_norm",
    )(x, w, h, g.reshape(1, d))


def _mlp_kernel(xn_ref, h_ref, w1_ref, w2_ref, g_ref, hout_ref, xnout_ref):
    f = pl.program_id(1)

    @pl.when(f == 0)
    def _():
        hout_ref[...] = h_ref[...]

    a = jnp.maximum(jnp.dot(xn_ref[...], w1_ref[...], preferred_element_type=F32), 0.0)
    hout_ref[...] += jnp.dot((a * a).astype(BF16), w2_ref[...], preferred_element_type=F32)

    @pl.when(f == pl.num_programs(1) - 1)
    def _():
        xnout_ref[...] = _rms(hout_ref[...], g_ref[...]).astype(xnout_ref.dtype)


def _mlp(xn, h, w1, w2, g, norm_dtype):
    m, d = xn.shape
    dff = w1.shape[1]
    tm = _row_tile(m, 384)
    tf = _col_tile(dff, 1024)
    return pl.pallas_call(
        _mlp_kernel,
        out_shape=(jax.ShapeDtypeStruct((m, d), F32), jax.ShapeDtypeStruct((m, d), norm_dtype)),
        grid=(m // tm, dff // tf),
        in_specs=[pl.BlockSpec((tm, d), lambda i, f: (i, 0)), pl.BlockSpec((tm, d), lambda i, f: (i, 0)),
                  pl.BlockSpec((d, tf), lambda i, f: (0, f)), pl.BlockSpec((tf, d), lambda i, f: (f, 0)),
                  pl.BlockSpec((1, d), lambda i, f: (0, 0))],
        out_specs=(pl.BlockSpec((tm, d), lambda i, f: (i, 0)), pl.BlockSpec((tm, d), lambda i, f: (i, 0))),
        compiler_params=_params("parallel", "arbitrary"),
        name="mlp",
    )(xn, h, w1, w2, g.reshape(1, d))


def _even_layout(d_model):
    a_q = d_model // 2
    a_kv = A_KV_HEADS * HEAD_DIM
    idx_q = IDX_HEADS * IDX_DIM
    b_v = d_model // 2
    b_qk = b_v // 2
    ref_sizes = dict(aq=a_q, ak=a_kv, av=a_kv, iq=idx_q, ik=IDX_DIM, iw=IDX_HEADS, bq=b_qk, bk=b_qk, bv=b_v,
                     bglr=B_GATE_RANK, br=b_v)
    ref_off, off = {}, 0
    for name, size in ref_sizes.items():
        ref_off[name] = off
        off += size
    lay, off = {}, 0
    for name in ("aq", "iq", "bq", "bk", "bv", "br", "ak", "av", "ik", "iw", "bglr"):
        lay[name] = (off, ref_sizes[name])
        off += ref_sizes[name]
    lay["misc"] = (lay["ik"][0], LANES)
    total = lay["misc"][0] + LANES
    assert off <= total
    return lay, ref_off, total


def _permute_w_in0(w_in0, lay, ref_off, total):
    w = jnp.zeros((w_in0.shape[0], total), w_in0.dtype)
    for name, (off, size) in lay.items():
        if name != "misc":
            w = w.at[:, off:off + size].set(w_in0[:, ref_off[name]:ref_off[name] + size])
    return w


def _rope_tables(pos):
    def tab(d):
        half = d // 2
        inv = ROPE_THETA ** (-jnp.arange(half, dtype=F32) * (2.0 / d))
        ang = pos.astype(F32)[:, None] * inv[None, :]
        c, s = jnp.cos(ang), jnp.sin(ang)
        reps = LANES // d
        return jnp.tile(jnp.concatenate([c, c], 1), (1, reps)), jnp.tile(jnp.concatenate([-s, s], 1), (1, reps))
    c128, s128 = tab(HEAD_DIM)
    c64, s64 = tab(IDX_DIM)
    return jnp.concatenate([c128, s128, c64, s64], axis=1)


def _rope_kernel(q_ref, kv_ref, tab_ref, aq_ref, iq_ref, ak_ref, ik_ref, kbf_ref, vbf_ref, ikbf_ref,
                 *, n_aq, n_iq, n_ak):
    cos128 = tab_ref[:, 0:128]
    sin128 = tab_ref[:, 128:256]
    cos64 = tab_ref[:, 256:384]
    sin64 = tab_ref[:, 384:512]
    lane = lax.broadcasted_iota(I32, cos64.shape, 1)
    low_half = (lane % IDX_DIM) < (IDX_DIM // 2)

    def rope128(x):
        return x * cos128 + pltpu.roll(x, HEAD_DIM // 2, 1) * sin128

    def rope64(x):
        rot = jnp.where(low_half, pltpu.roll(x, LANES - IDX_DIM // 2, 1), pltpu.roll(x, IDX_DIM // 2, 1))
        return x * cos64 + rot * sin64

    for c in range(n_aq):
        sl = slice(c * LANES, (c + 1) * LANES)
        aq_ref[:, sl] = rope128(q_ref[:, sl]).astype(aq_ref.dtype)
    for c in range(n_iq):
        sl = slice(c * LANES, (c + 1) * LANES)
        iq_ref[:, sl] = rope64(q_ref[:, n_aq * LANES + c * LANES:n_aq * LANES + (c + 1) * LANES]).astype(iq_ref.dtype)
    for c in range(n_ak):
        sl = slice(c * LANES, (c + 1) * LANES)
        kr = rope128(kv_ref[:, sl])
        ak_ref[:, sl] = kr
        kbf_ref[:, sl] = kr.astype(BF16)
    vbf_ref[...] = kv_ref[:, n_ak * LANES:2 * n_ak * LANES].astype(BF16)
    ik = rope64(kv_ref[:, 2 * n_ak * LANES:(2 * n_ak + 1) * LANES])
    ik_ref[...] = ik
    ikbf_ref[...] = ik[:, :IDX_DIM].astype(BF16)


def _rope(p0, tab, lay):
    m = p0.shape[0]
    tr = _row_tile(m, 384)
    a_q, a_kv, idx_q = lay["aq"][1], lay["ak"][1], lay["iq"][1]
    qw = a_q + idx_q
    kvw = 2 * a_kv + LANES
    assert lay["aq"][0] == 0 and lay["iq"][0] == a_q and lay["ak"][0] % kvw == 0
    kern = functools.partial(_rope_kernel, n_aq=a_q // LANES, n_iq=idx_q // LANES, n_ak=a_kv // LANES)
    row = lambda i: (i, 0)
    return pl.pallas_call(
        kern,
        out_shape=(jax.ShapeDtypeStruct((m, a_q), BF16), jax.ShapeDtypeStruct((m, idx_q), BF16),
                   jax.ShapeDtypeStruct((m, a_kv), F32), jax.ShapeDtypeStruct((m, LANES), F32),
                   jax.ShapeDtypeStruct((m, a_kv), BF16), jax.ShapeDtypeStruct((m, a_kv), BF16),
                   jax.ShapeDtypeStruct((m, IDX_DIM), BF16)),
        grid=(m // tr,),
        in_specs=[pl.BlockSpec((tr, qw), row),
                  pl.BlockSpec((tr, kvw), lambda i: (i, lay["ak"][0] // kvw)),
                  pl.BlockSpec((tr, 4 * LANES), row)],
        out_specs=(pl.BlockSpec((tr, a_q), row), pl.BlockSpec((tr, idx_q), row), pl.BlockSpec((tr, a_kv), row),
                   pl.BlockSpec((tr, LANES), row), pl.BlockSpec((tr, a_kv), row), pl.BlockSpec((tr, a_kv), row),
                   pl.BlockSpec((tr, IDX_DIM), row)),
        compiler_params=_params("parallel"),
        name="rope",
    )(p0, p0, tab)


GLA_SUB = 16


def _split_bf16(x):
    hi = x.astype(BF16)
    return hi, (x - hi.astype(F32)).astype(BF16)


def _log_sigmoid(x):
    return jnp.minimum(x, 0.0) - jnp.log(1.0 + jnp.exp(-jnp.abs(x)))


def _dot(a, b):
    return jnp.dot(a, b, preferred_element_type=F32)


def _dot_nt(a, b):
    return lax.dot_general(a, b, (((1,), (1,)), ((), ())), preferred_element_type=F32)


def _dot_tn(a, b):
    return lax.dot_general(a, b, (((0,), (0,)), ((), ())), preferred_element_type=F32)


def _gla_gate(misc, w2_hi, w2_lo, bias, valid):
    m_hi, m_lo = _split_bf16(misc)
    x = _dot(m_hi, w2_hi) + _dot(m_lo, w2_hi) + _dot(m_hi, w2_lo) + bias
    return jnp.where(valid, _log_sigmoid(x) * (1.0 / B_GATE_TAU), 0.0)


def _gla_chunk(q, k, v, g, state_ref, dk, dv):
    c = q.shape[0]
    n_sub = c // GLA_SUB
    ri = lax.broadcasted_iota(I32, (c, c), 0)
    ci = lax.broadcasted_iota(I32, (c, c), 1)
    causal = ri >= ci
    tri = jnp.where(causal, 1.0, 0.0).astype(BF16)
    g_hi, g_lo = _split_bf16(g)
    gcum = _dot(tri, g_hi) + _dot(tri, g_lo)
    scale = dk ** -0.5
    pad = _round_up(c, LANES)
    outs = []
    for h in range(B_HEADS):
        gh = gcum[:, h * dk:(h + 1) * dk]
        qh = q[:, h * dk:(h + 1) * dk] * scale
        kh = k[:, h * dk:(h + 1) * dk]
        vh = v[:, h * dv:(h + 1) * dv].astype(BF16)
        q_parts, k_parts = [], []
        for i in range(n_sub):
            lo, hi = i * GLA_SUB, (i + 1) * GLA_SUB
            base = gh[lo - 1:lo, :] if i else jnp.zeros((1, dk), F32)
            q_parts.append(qh[lo:hi] * jnp.exp(gh[lo:hi] - base))
            k_parts.append(kh[:hi] * jnp.exp(base - gh[:hi]))
            if pad > hi:
                k_parts.append(jnp.zeros((pad - hi, dk), F32))
        qt = jnp.concatenate(q_parts, axis=0).astype(BF16)
        kt = jnp.concatenate(k_parts, axis=0).astype(BF16)
        r = _dot_nt(qt, kt)
        att = jnp.concatenate([r[i * GLA_SUB:(i + 1) * GLA_SUB, i * pad:i * pad + c] for i in range(n_sub)], axis=0)
        att = jnp.where(causal, att, 0.0).astype(BF16)
        s = state_ref[h]
        o = _dot(att, vh) + _dot((qh * jnp.exp(gh)).astype(BF16), s.astype(BF16))
        last = gh[c - 1:c, :]
        k_tail = (kh * jnp.exp(last - gh)).astype(BF16)
        decay = jnp.transpose(jnp.broadcast_to(jnp.exp(last), (dk, dk)))
        decay = jnp.concatenate([decay] * (dv // dk), axis=1)
        state_ref[h] = decay * s + _dot_tn(k_tail, vh)
        outs.append(o)
    return outs


def _gla_out(outs, r, head_norm, dv):
    res = []
    for h, o in enumerate(outs):
        gate = r[:, h * dv:(h + 1) * dv]
        gate = gate * (1.0 / (1.0 + jnp.exp(-gate)))
        res.append(_rms(o, head_norm) * gate)
    return jnp.concatenate(res, axis=1)


def _gla_prompt_kernel(q_ref, k_ref, v_ref, r_ref, misc_ref, w2h_ref, w2l_ref, b_ref, hn_ref, o_ref, state_ref,
                       *, chunk, n_valid, dk, dv):
    blk = pl.program_id(0)

    @pl.when(blk == 0)
    def _():
        state_ref[...] = jnp.zeros_like(state_ref)

    rows = q_ref.shape[0]
    for c0 in range(0, rows, chunk):
        sl = slice(c0, c0 + chunk)
        row = blk * rows + c0 + lax.broadcasted_iota(I32, (chunk, 1), 0)
        g = _gla_gate(misc_ref[sl, :], w2h_ref[...], w2l_ref[...], b_ref[...], row < n_valid)
        outs = _gla_chunk(q_ref[sl, :], k_ref[sl, :], v_ref[sl, :], g, state_ref, dk, dv)
        o_ref[sl, :] = _gla_out(outs, r_ref[sl, :], hn_ref[...], dv).astype(o_ref.dtype)


def _gla_sample_kernel(q_ref, k_ref, v_ref, r_ref, misc_ref, w2h_ref, w2l_ref, b_ref, hn_ref, s_ref, o_ref,
                       state_ref, *, n_valid, dk, dv):
    state_ref[0] = s_ref[0]
    rows = q_ref.shape[1]
    row = lax.broadcasted_iota(I32, (rows, 1), 0)
    g = _gla_gate(misc_ref[0], w2h_ref[...], w2l_ref[...], b_ref[...], row < n_valid)
    outs = _gla_chunk(q_ref[0], k_ref[0], v_ref[0], g, state_ref.at[0], dk, dv)
    o_ref[0] = _gla_out(outs, r_ref[0], hn_ref[...], dv).astype(o_ref.dtype)


def _gate_weights(w_b_gate2, b_b_gate, lay):
    off = lay["bglr"][0] - lay["misc"][0]
    w2 = jnp.zeros((LANES, w_b_gate2.shape[1]), F32).at[off:off + B_GATE_RANK].set(w_b_gate2)
    hi = w2.astype(BF16)
    lo = (w2 - hi.astype(F32)).astype(BF16)
    return hi, lo, b_b_gate.reshape(1, -1)


def _gla_prompt(p0, lay, n_rows, n_valid, gate_w, head_norm, chunk=64):
    bqk, bv = lay["bq"][1], lay["bv"][1]
    dk, dv = bqk // B_HEADS, bv // B_HEADS
    w2h, w2l, bias = gate_w
    col = lambda name: (lambda i, o=lay[name][0] // lay[name][1]: (i, o))
    const = lambda i: (0, 0)
    kern = functools.partial(_gla_prompt_kernel, chunk=chunk, n_valid=n_valid, dk=dk, dv=dv)
    return pl.pallas_call(
        kern,
        out_shape=(jax.ShapeDtypeStruct((n_rows, bv), BF16), jax.ShapeDtypeStruct((B_HEADS, dk, dv), F32)),
        grid=(n_rows // ROW_BLOCK,),
        in_specs=[pl.BlockSpec((ROW_BLOCK, bqk), col("bq")), pl.BlockSpec((ROW_BLOCK, bqk), col("bk")),
                  pl.BlockSpec((ROW_BLOCK, bv), col("bv")), pl.BlockSpec((ROW_BLOCK, bv), col("br")),
                  pl.BlockSpec((ROW_BLOCK, LANES), col("misc")),
                  pl.BlockSpec(w2h.shape, const), pl.BlockSpec(w2l.shape, const), pl.BlockSpec(bias.shape, const),
                  pl.BlockSpec((1, dv), const)],
        out_specs=(pl.BlockSpec((ROW_BLOCK, bv), lambda i: (i, 0)),
                   pl.BlockSpec((B_HEADS, dk, dv), lambda i: (0, 0, 0))),
        compiler_params=_params("arbitrary"),
        name="gla_prompt",
    )(p0, p0, p0, p0, p0, w2h, w2l, bias, head_norm.reshape(1, dv))


def _gla_sample(q, k, v, r, misc, state, n_valid, gate_w, head_norm):
    db, rows, bqk = q.shape
    bv = v.shape[2]
    dk, dv = bqk // B_HEADS, bv // B_HEADS
    w2h, w2l, bias = gate_w
    seq = lambda i: (i, 0, 0)
    const = lambda i: (0, 0)
    kern = functools.partial(_gla_sample_kernel, n_valid=n_valid, dk=dk, dv=dv)
    return pl.pallas_call(
        kern,
        out_shape=(jax.ShapeDtypeStruct((db, rows, bv), BF16), jax.ShapeDtypeStruct(state.shape, F32)),
        grid=(db,),
        in_specs=[pl.BlockSpec((1, rows, bqk), seq), pl.BlockSpec((1, rows, bqk), seq),
                  pl.BlockSpec((1, rows, bv), seq), pl.BlockSpec((1, rows, bv), seq),
                  pl.BlockSpec((1, rows, LANES), seq),
                  pl.BlockSpec(w2h.shape, const), pl.BlockSpec(w2l.shape, const), pl.BlockSpec(bias.shape, const),
                  pl.BlockSpec((1, dv), const),
                  pl.BlockSpec((1, B_HEADS, dk, dv), lambda i: (i, 0, 0, 0))],
        out_specs=(pl.BlockSpec((1, rows, bv), seq), pl.BlockSpec((1, B_HEADS, dk, dv), lambda i: (i, 0, 0, 0))),
        compiler_params=_params("parallel"),
        name="gla_sample",
    )(q, k, v, r, misc, w2h, w2l, bias, head_norm.reshape(1, dv), state)


SB_DONE = -120.0


def _sb_suffix_matrix():
    j = lax.broadcasted_iota(I32, (LANES, 2 * LANES), 0)
    s = lax.broadcasted_iota(I32, (LANES, 2 * LANES), 1)
    return jnp.where((j > s) | (s >= LANES), 1.0, 0.0).astype(BF16)


def _sb_block(q, k, v, strict, carry, acc, suffix):
    z = _dot_nt(q, k) * (HEAD_DIM ** -0.5)
    log_keep = _log_sigmoid(-z)
    lk = log_keep if strict is None else jnp.where(strict, log_keep, 0.0)
    hi, lo = _split_bf16(lk)
    sums = _dot(hi, suffix) + _dot(lo, suffix)
    later = sums[:, :LANES] + carry
    w = jnp.exp(z + log_keep + later)
    if strict is not None:
        w = jnp.where(strict, w, 0.0)
    return carry + sums[:, LANES:], acc + _dot(w.astype(BF16), v)


def _sb_prompt_kernel(q_ref, k_ref, v_ref, o_ref, carry_ref, acc_ref, *, n_kv, rep):
    b = pl.program_id(0)
    suffix = _sb_suffix_matrix()
    carry_ref[...] = jnp.zeros_like(carry_ref)
    acc_ref[...] = jnp.zeros_like(acc_ref)
    rows = rep * ROW_BLOCK
    q_row = lax.broadcasted_iota(I32, (rows, LANES), 0) % ROW_BLOCK
    col = lax.broadcasted_iota(I32, (rows, LANES), 1)

    def cond(st):
        return (st[0] >= 0) & (st[1] == 0)

    def body(st):
        kb = st[0]
        k0 = pl.multiple_of(kb * ROW_BLOCK, ROW_BLOCK)
        strict = (kb * ROW_BLOCK + col) < (b * ROW_BLOCK + q_row)
        top = jnp.full((), -jnp.inf, F32)
        for g in range(n_kv):
            qg = jnp.concatenate([q_ref[:, (g * rep + r) * HEAD_DIM:(g * rep + r + 1) * HEAD_DIM]
                                  for r in range(rep)], axis=0)
            kg = k_ref[pl.ds(k0, ROW_BLOCK), g * HEAD_DIM:(g + 1) * HEAD_DIM]
            vg = v_ref[pl.ds(k0, ROW_BLOCK), g * HEAD_DIM:(g + 1) * HEAD_DIM]
            c, a = _sb_block(qg, kg, vg, strict, carry_ref[g], acc_ref[g], suffix)
            carry_ref[g] = c
            acc_ref[g] = a
            top = jnp.maximum(top, jnp.max(c))
        return kb - 1, (top < SB_DONE).astype(I32)

    lax.while_loop(cond, body, (b, jnp.zeros((), I32)))
    for g in range(n_kv):
        for r in range(rep):
            h = g * rep + r
            o_ref[:, h * HEAD_DIM:(h + 1) * HEAD_DIM] = acc_ref[g, r * ROW_BLOCK:(r + 1) * ROW_BLOCK, :].astype(o_ref.dtype)


def _sb_prompt(q, k, v, n_rows):
    cq = q.shape[1]
    ckv = k.shape[1]
    n_kv = ckv // HEAD_DIM
    rep = cq // ckv
    kern = functools.partial(_sb_prompt_kernel, n_kv=n_kv, rep=rep)
    return pl.pallas_call(
        kern,
        out_shape=jax.ShapeDtypeStruct((n_rows, cq), BF16),
        grid=(n_rows // ROW_BLOCK,),
        in_specs=[pl.BlockSpec((ROW_BLOCK, cq), lambda i: (i, 0)),
                  pl.BlockSpec((n_rows, ckv), lambda i: (0, 0)),
                  pl.BlockSpec((n_rows, ckv), lambda i: (0, 0))],
        out_specs=pl.BlockSpec((ROW_BLOCK, cq), lambda i: (i, 0)),
        scratch_shapes=[pltpu.VMEM((n_kv, rep * ROW_BLOCK, LANES), F32),
                        pltpu.VMEM((n_kv, rep * ROW_BLOCK, HEAD_DIM), F32)],
        compiler_params=_params("parallel"),
        name="stick_prompt",
    )(q, k, v)


def _sb_sample_kernel(pt_ref, q_ref, *refs, n_kv, n_new, dec_seq, n_pages_step, has_init):
    del pt_ref
    refs = list(refs)
    new_refs = [refs.pop(0), refs.pop(0)] if n_new else None
    k_pages = [refs.pop(0) for _ in range(n_pages_step)]
    v_pages = [refs.pop(0) for _ in range(n_pages_step)]
    init_refs = [refs.pop(0), refs.pop(0)] if has_init else None
    o_ref, carry_ref = refs
    suffix = _sb_suffix_matrix()
    rows = q_ref.shape[2]

    @pl.when(pl.program_id(1) == 0)
    def _():
        if has_init:
            o_ref[...] = init_refs[0][...]
            carry_ref[...] = init_refs[1][...]
        else:
            o_ref[...] = jnp.zeros_like(o_ref)
            carry_ref[...] = jnp.zeros_like(carry_ref)

    def visit(k_of, v_of, strict):
        @pl.when(jnp.max(carry_ref[...]) >= SB_DONE)
        def _():
            for g in range(n_kv):
                c, a = _sb_block(q_ref[0, g], k_of(g), v_of(g), strict, carry_ref[0, g], o_ref[0, g], suffix)
                carry_ref[0, g] = c
                o_ref[0, g] = a

    cols = lambda ref, g: ref[0, :, g * HEAD_DIM:(g + 1) * HEAD_DIM].astype(BF16)
    if n_new:
        q_idx = lax.broadcasted_iota(I32, (rows, LANES), 0) % dec_seq
        col = lax.broadcasted_iota(I32, (rows, LANES), 1)
        visit(lambda g: cols(new_refs[0], g), lambda g: cols(new_refs[1], g), col < q_idx)
    for kp, vp in zip(k_pages, v_pages):
        visit(lambda g, kp=kp: cols(kp, g), lambda g, vp=vp: cols(vp, g), None)


def _sb_sample_call(page_table, q, new_kv, cache_k, cache_v, init, dec_seq, first_page, n_steps, pages_per_step):
    db, n_kv, rows, _ = q.shape
    ckv = cache_k.shape[2]
    page = cache_k.shape[1]
    assert page == LANES
    seq4 = lambda b, c, pt: (b, 0, 0, 0)
    in_specs = [pl.BlockSpec((1, n_kv, rows, HEAD_DIM), seq4)]
    args = [q]
    if new_kv is not None:
        in_specs += [pl.BlockSpec((1, page, ckv), lambda b, c, pt: (b, 0, 0))] * 2
        args += list(new_kv)
    for cache in (cache_k, cache_v):
        for i in range(pages_per_step):
            in_specs.append(pl.BlockSpec(
                (1, page, ckv), lambda b, c, pt, i=i: (pt[b, first_page - c * pages_per_step - i], 0, 0)))
            args.append(cache)
    if init is not None:
        in_specs += [pl.BlockSpec((1, n_kv, rows, HEAD_DIM), seq4)] * 2
        args += list(init)
    kern = functools.partial(_sb_sample_kernel, n_kv=n_kv, n_new=new_kv is not None, dec_seq=dec_seq,
                             n_pages_step=pages_per_step, has_init=init is not None)
    out_sds = jax.ShapeDtypeStruct((db, n_kv, rows, HEAD_DIM), F32)
    return pl.pallas_call(
        kern,
        out_shape=(out_sds, out_sds),
        grid_spec=pltpu.PrefetchScalarGridSpec(
            num_scalar_prefetch=1,
            grid=(db, n_steps),
            in_specs=in_specs,
            out_specs=(pl.BlockSpec((1, n_kv, rows, HEAD_DIM), seq4), pl.BlockSpec((1, n_kv, rows, HEAD_DIM), seq4)),
        ),
        compiler_params=_params("parallel", "arbitrary"),
        name="stick_sample",
    )(page_table, *args)


def _sortable(score):
    bits = lax.bitcast_convert_type(score + 0.0, I32)
    key = bits ^ (lax.shift_right_arithmetic(bits, 31) & 0x7FFFFFFF)
    return jnp.maximum(key, INT_MIN + 1)


def _count(key_ref, n_units, unit, pred):
    def body(u, acc):
        r0 = pl.multiple_of(u * unit, unit)
        hit = pred(key_ref[pl.ds(r0, unit), :], r0)
        ones = jnp.where(hit, jnp.ones(hit.shape, I32), jnp.zeros(hit.shape, I32))
        return acc + jnp.sum(ones.reshape(unit // SUBLANES, SUBLANES, LANES), axis=0)
    acc = lax.fori_loop(0, n_units, body, jnp.zeros((SUBLANES, LANES), I32))
    return jnp.sum(acc, axis=0, keepdims=True)


def _select_topk(key_ref, n_units, unit, k, n_rows):
    def bit_step(i, tau):
        cand = tau ^ lax.shift_left(jnp.int32(1), 31 - i)
        cnt = _count(key_ref, n_units, unit, lambda keys, r0: keys >= cand)
        return jnp.where(cnt >= k, cand, tau)
    tau = lax.fori_loop(0, 32, bit_step, jnp.full((1, LANES), INT_MIN, I32))
    c_gt = _count(key_ref, n_units, unit, lambda keys, r0: keys > tau)
    c_ge = _count(key_ref, n_units, unit, lambda keys, r0: keys >= tau)
    real = tau > INT_MIN
    tie = real & (c_ge > k)
    need = k - c_gt
    row = lax.broadcasted_iota(I32, (unit, LANES), 0)
    n_bits = max(1, (n_rows - 1).bit_length())

    def break_ties():
        def pos_step(i, x):
            cand = x | lax.shift_left(jnp.int32(1), n_bits - 1 - i)
            cnt = _count(key_ref, n_units, unit, lambda keys, r0: (keys == tau) & ((r0 + row) < cand))
            return jnp.where(cnt < need, cand, x)
        return lax.fori_loop(0, n_bits, pos_step, jnp.zeros((1, LANES), I32))

    last_tie = lax.cond(jnp.max(tie.astype(I32)) > 0, break_ties, lambda: jnp.zeros((1, LANES), I32))
    last = jnp.where(tie, last_tie, jnp.where(real, jnp.int32(2 ** 30), jnp.int32(-1)))
    return tau, last


def _selected(keys, r0, tau, last):
    row = lax.broadcasted_iota(I32, keys.shape, 0)
    return (keys > tau) | ((keys == tau) & ((r0 + row) <= last))


IDX_SCALE = (IDX_HEADS * IDX_DIM) ** -0.5
MASKED = -1e30


def _dsa_prompt_kernel(aq_ref, iq_ref, iw_ref, ik_ref, k_ref, v_ref, o_ref, key_ref, m_ref, l_ref, acc_ref,
                       *, unit, topk, n_valid, n_heads, n_kv):
    b = pl.program_id(0)
    n_units = ((b + 1) * ROW_BLOCK + unit - 1) // unit
    rep = n_heads // n_kv
    row = lax.broadcasted_iota(I32, (unit, LANES), 0)
    q_pos = b * ROW_BLOCK + lax.broadcasted_iota(I32, (unit, LANES), 1)
    q_real = q_pos < n_valid

    iq_t = iq_ref[0]
    rhs = jnp.concatenate([iq_t[h * IDX_DIM:(h + 1) * IDX_DIM, :] for h in range(IDX_HEADS)], axis=1)
    w_t = iw_ref[0]

    def score_unit(u, carry):
        r0 = pl.multiple_of(u * unit, unit)
        dots = _dot(ik_ref[pl.ds(r0, unit), :], rhs)
        score = jnp.zeros((unit, LANES), F32)
        for h in range(IDX_HEADS):
            score = score + jnp.maximum(dots[:, h * LANES:(h + 1) * LANES], 0.0) * w_t[h:h + 1, :]
        k_pos = r0 + row
        adm = (k_pos <= q_pos) & (q_real | (k_pos == 0))
        key_ref[pl.ds(r0, unit), :] = jnp.where(adm, _sortable(score * IDX_SCALE), INT_MIN)
        return carry

    lax.fori_loop(0, n_units, score_unit, 0)
    tau, last = _select_topk(key_ref, n_units, unit, topk, key_ref.shape[0])

    m_ref[...] = jnp.full_like(m_ref, MASKED)
    l_ref[...] = jnp.zeros_like(l_ref)
    acc_ref[...] = jnp.zeros_like(acc_ref)
    aq_t = aq_ref[0]

    def attend_unit(u, carry):
        r0 = pl.multiple_of(u * unit, unit)
        sel = _selected(key_ref[pl.ds(r0, unit), :], r0, tau, last)
        for g in range(n_kv):
            kg = k_ref[pl.ds(r0, unit), g * HEAD_DIM:(g + 1) * HEAD_DIM]
            vg = v_ref[pl.ds(r0, unit), g * HEAD_DIM:(g + 1) * HEAD_DIM]
            qg = jnp.concatenate([aq_t[(g * rep + r) * HEAD_DIM:(g * rep + r + 1) * HEAD_DIM, :]
                                  for r in range(rep)], axis=1)
            logits = _dot(kg, qg) * (HEAD_DIM ** -0.5)
            ps = []
            for r in range(rep):
                h = g * rep + r
                lg = jnp.where(sel, logits[:, r * LANES:(r + 1) * LANES], MASKED)
                m_old = m_ref[h:h + 1, :]
                m_new = jnp.maximum(m_old, jnp.max(lg, axis=0, keepdims=True))
                alpha = jnp.exp(m_old - m_new)
                p = jnp.where(sel, jnp.exp(lg - m_new), 0.0)
                l_ref[h:h + 1, :] = alpha * l_ref[h:h + 1, :] + jnp.sum(p, axis=0, keepdims=True)
                m_ref[h:h + 1, :] = m_new
                acc_ref[h] = acc_ref[h] * alpha
                ps.append(p.astype(BF16))
            pv = _dot_tn(vg, jnp.concatenate(ps, axis=1))
            for r in range(rep):
                acc_ref[g * rep + r] += pv[:, r * LANES:(r + 1) * LANES]
        return carry

    lax.fori_loop(0, n_units, attend_unit, 0)
    for h in range(n_heads):
        out = acc_ref[h] * (1.0 / l_ref[h:h + 1, :])
        o_ref[:, h * HEAD_DIM:(h + 1) * HEAD_DIM] = jnp.transpose(out).astype(o_ref.dtype)


def _dsa_prompt(aq_t, iq_t, iw_t, ik, k, v, n_rows, n_valid, topk):
    n_blk = n_rows // ROW_BLOCK
    n_heads = aq_t.shape[1] // HEAD_DIM
    n_kv = k.shape[1] // HEAD_DIM
    unit = _row_tile(n_rows, 640)
    blk = lambda i: (i, 0, 0)
    const = lambda i: (0, 0)
    kern = functools.partial(_dsa_prompt_kernel, unit=unit, topk=topk, n_valid=n_valid, n_heads=n_heads, n_kv=n_kv)
    return pl.pallas_call(
        kern,
        out_shape=jax.ShapeDtypeStruct((n_rows, n_heads * HEAD_DIM), BF16),
        grid=(n_blk,),
        in_specs=[pl.BlockSpec((1,) + aq_t.shape[1:], blk), pl.BlockSpec((1,) + iq_t.shape[1:], blk),
                  pl.BlockSpec((1,) + iw_t.shape[1:], blk),
                  pl.BlockSpec((n_rows, ik.shape[1]), const), pl.BlockSpec((n_rows, k.shape[1]), const),
                  pl.BlockSpec((n_rows, v.shape[1]), const)],
        out_specs=pl.BlockSpec((ROW_BLOCK, n_heads * HEAD_DIM), lambda i: (i, 0)),
        scratch_shapes=[pltpu.VMEM((n_rows, LANES), I32),
                        pltpu.VMEM((n_heads, LANES), F32), pltpu.VMEM((n_heads, LANES), F32),
                        pltpu.VMEM((n_heads, HEAD_DIM, LANES), F32)],
        compiler_params=_params("parallel"),
        name="dsa_prompt",
    )(aq_t, iq_t, iw_t, ik, k, v)


Q_PAD = SUBLANES


def _idx_sample_kernel(pt_ref, iq_ref, w_ref, new_ref, *refs, n_pages):
    del pt_ref
    pages, o_ref = refs[:n_pages], refs[n_pages]
    iq = iq_ref[0]
    w = w_ref[0]

    def score(keys):
        s = jnp.maximum(_dot_nt(iq, keys), 0.0) * w
        return jnp.sum(s.reshape(IDX_HEADS, Q_PAD, LANES), axis=0) * IDX_SCALE

    for p in range(n_pages):
        o_ref[0, :, p * LANES:(p + 1) * LANES] = score(pages[p][0].astype(BF16))
    o_ref[0, :, n_pages * LANES:] = score(new_ref[0])


def _idx_sample(page_table, iq, w, new_ik, cache_idx):
    db = iq.shape[0]
    n_pages = page_table.shape[1]
    page = cache_idx.shape[1]
    assert page == LANES
    seq = lambda b, pt: (b, 0, 0)
    in_specs = [pl.BlockSpec((1,) + iq.shape[1:], seq), pl.BlockSpec((1,) + w.shape[1:], seq),
                pl.BlockSpec((1,) + new_ik.shape[1:], seq)]
    in_specs += [pl.BlockSpec((1, page, IDX_DIM), lambda b, pt, p=p: (pt[b, p], 0, 0)) for p in range(n_pages)]
    n_keys = (n_pages + 1) * LANES
    return pl.pallas_call(
        functools.partial(_idx_sample_kernel, n_pages=n_pages),
        out_shape=jax.ShapeDtypeStruct((db, Q_PAD, n_keys), F32),
        grid_spec=pltpu.PrefetchScalarGridSpec(
            num_scalar_prefetch=1, grid=(db,), in_specs=in_specs,
            out_specs=pl.BlockSpec((1, Q_PAD, n_keys), seq)),
        compiler_params=_params("parallel"),
        name="idx_sample",
    )(page_table, iq, w, new_ik, *([cache_idx] * n_pages))


def _topk_sample_kernel(s_ref, o_ref, key_ref, *, unit, topk, past_len, dec_seq):
    n_keys = s_ref.shape[0]
    n_units = n_keys // unit
    row = lax.broadcasted_iota(I32, (unit, LANES), 0)
    q = lax.broadcasted_iota(I32, (unit, LANES), 1) % Q_PAD
    for u in range(n_units):
        k_pos = u * unit + row
        adm = ((q < dec_seq) & (k_pos <= past_len + q)) | ((q >= dec_seq) & (k_pos == 0))
        key_ref[u * unit:(u + 1) * unit, :] = jnp.where(adm, _sortable(s_ref[u * unit:(u + 1) * unit, :]), INT_MIN)
    tau, last = _select_topk(key_ref, n_units, unit, topk, n_keys)
    for u in range(n_units):
        sel = _selected(key_ref[u * unit:(u + 1) * unit, :], u * unit, tau, last)
        o_ref[u * unit:(u + 1) * unit, :] = jnp.where(sel, 0.0, MASKED)


def _topk_sample(scores_t, topk, past_len, dec_seq):
    n_keys, n_q = scores_t.shape
    unit = _row_tile(n_keys, 640)
    kern = functools.partial(_topk_sample_kernel, unit=unit, topk=topk, past_len=past_len, dec_seq=dec_seq)
    return pl.pallas_call(
        kern,
        out_shape=jax.ShapeDtypeStruct((n_keys, n_q), F32),
        grid=(n_q // LANES,),
        in_specs=[pl.BlockSpec((n_keys, LANES), lambda i: (0, i))],
        out_specs=pl.BlockSpec((n_keys, LANES), lambda i: (0, i)),
        scratch_shapes=[pltpu.VMEM((n_keys, LANES), I32)],
        compiler_params=_params("parallel"),
        name="topk_sample",
    )(scores_t)


def _attn_sample_kernel(pt_ref, q_ref, bias_ref, bias_new_ref, k_new_ref, v_new_ref, *refs, n_kv, pages_per_step):
    del pt_ref
    k_pages = refs[:pages_per_step]
    v_pages = refs[pages_per_step:2 * pages_per_step]
    o_ref, m_ref, l_ref, acc_ref = refs[2 * pages_per_step:]
    c = pl.program_id(1)
    rep = q_ref.shape[2] // Q_PAD

    @pl.when(c == 0)
    def _():
        m_ref[...] = jnp.full_like(m_ref, MASKED)
        l_ref[...] = jnp.zeros_like(l_ref)
        acc_ref[...] = jnp.zeros_like(acc_ref)

    def update(g, kg, vg, bias):
        sel = jnp.concatenate([bias] * rep, axis=0) > 0.5 * MASKED
        lg = jnp.where(sel, _dot_nt(q_ref[0, g], kg) * (HEAD_DIM ** -0.5), MASKED)
        m_old = m_ref[g]
        m_new = jnp.maximum(m_old, jnp.max(lg, axis=1, keepdims=True))
        alpha = jnp.exp(m_old - m_new)
        p = jnp.where(sel, jnp.exp(lg - m_new), 0.0)
        l_ref[g] = alpha * l_ref[g] + jnp.sum(p, axis=1, keepdims=True)
        acc_ref[g] = alpha * acc_ref[g] + _dot(p.astype(BF16), vg)
        m_ref[g] = m_new

    def head(ref, g):
        return ref[0, :, g * HEAD_DIM:(g + 1) * HEAD_DIM].astype(BF16)

    @pl.when(c == 0)
    def _():
        for g in range(n_kv):
            update(g, head(k_new_ref, g), head(v_new_ref, g), bias_new_ref[0])

    for g in range(n_kv):
        kg = jnp.concatenate([head(r, g) for r in k_pages], axis=0)
        vg = jnp.concatenate([head(r, g) for r in v_pages], axis=0)
        update(g, kg, vg, bias_ref[0])

    @pl.when(c == pl.num_programs(1) - 1)
    def _():
        for g in range(n_kv):
            o_ref[0, g] = acc_ref[g] * (1.0 / l_ref[g])


def _attn_sample(page_table, q, bias, new_k, new_v, cache_k, cache_v):
    db, n_kv, rows, _ = q.shape
    n_pages = page_table.shape[1]
    page, ckv = cache_k.shape[1], cache_k.shape[2]
    per_step = max(d for d in range(1, 17) if n_pages % d == 0)
    seq4 = lambda b, c, pt: (b, 0, 0, 0)
    seq3 = lambda b, c, pt: (b, 0, 0)
    in_specs = [pl.BlockSpec((1, n_kv, rows, HEAD_DIM), seq4),
                pl.BlockSpec((1, Q_PAD, per_step * page), lambda b, c, pt: (b, 0, c)),
                pl.BlockSpec((1, Q_PAD, page), lambda b, c, pt: (b, 0, n_pages)),
                pl.BlockSpec((1, page, ckv), seq3), pl.BlockSpec((1, page, ckv), seq3)]
    for _ in range(2):
        in_specs += [pl.BlockSpec((1, page, ckv), lambda b, c, pt, i=i: (pt[b, c * per_step + i], 0, 0))
                     for i in range(per_step)]
    return pl.pallas_call(
        functools.partial(_attn_sample_kernel, n_kv=n_kv, pages_per_step=per_step),
        out_shape=jax.ShapeDtypeStruct((db, n_kv, rows, HEAD_DIM), F32),
        grid_spec=pltpu.PrefetchScalarGridSpec(
            num_scalar_prefetch=1, grid=(db, n_pages // per_step), in_specs=in_specs,
            out_specs=pl.BlockSpec((1, n_kv, rows, HEAD_DIM), seq4),
            scratch_shapes=[pltpu.VMEM((n_kv, rows, 1), F32), pltpu.VMEM((n_kv, rows, 1), F32),
                            pltpu.VMEM((n_kv, rows, HEAD_DIM), F32)]),
        compiler_params=_params("parallel", "arbitrary"),
        name="attn_sample",
    )(page_table, q, bias, bias, new_k, new_v, *([cache_k] * per_step), *([cache_v] * per_step))


SB_EAGER_PAGES = 4
SB_PAGES_PER_STEP = 12


def _sb_sample(page_table, q, new_k, new_v, cache_k, cache_v, dec_seq):
    n_pages = page_table.shape[1]
    eager = min(SB_EAGER_PAGES, n_pages)
    o, carry = _sb_sample_call(page_table, q, (new_k, new_v), cache_k, cache_v, None, dec_seq,
                               n_pages - 1, 1, eager)
    rest = n_pages - eager
    if rest == 0:
        return o
    per_step = max(d for d in range(1, SB_PAGES_PER_STEP + 1) if rest % d == 0)
    finish = lambda: _sb_sample_call(page_table, q, None, cache_k, cache_v, (o, carry), dec_seq,
                                     rest - 1, rest // per_step, per_step)[0]
    return lax.cond(jnp.max(carry) >= SB_DONE, finish, lambda: o)


def _pad_rows(a, rows):
    return jnp.pad(a, ((0, 0), (0, rows - a.shape[1])) + ((0, 0),) * (a.ndim - 2))


def kernel(x_prompt, x_sample, cache_a_k, cache_a_v, cache_a_idx_k, state_b, cache_c_k, cache_c_v, page_table,
           meta_tokens, norm_mix, norm_ffn, norm_final, w_in0, w_b_gate2, b_b_gate, b_head_norm, w_out0,
           w_in1, w_out1, w_ff1, w_ff2):
    batch, seq, d = x_prompt.shape
    assert batch == 1
    db, ds, _ = x_sample.shape
    assert ds <= Q_PAD and ds <= GLA_SUB
    t_p = N_META + seq
    t_pad = _round_up(t_p, ROW_BLOCK)
    n_s = db * ds
    m_s = _round_up(n_s, ROW_BLOCK)
    n_blk = t_pad // ROW_BLOCK
    n_pool, page = cache_a_k.shape[:2]
    n_pages = page_table.shape[1]
    past_len = n_pages * page
    topk_p = min(TOPK_MAX, seq // 4)
    topk_s = min(TOPK_MAX, (past_len + ds) // 4)
    srows = slice(t_pad, t_pad + n_s)

    h = jnp.concatenate([meta_tokens.astype(F32), x_prompt[0], jnp.zeros((t_pad - t_p, d), F32),
                         x_sample.reshape(n_s, d), jnp.zeros((m_s - n_s, d), F32)], axis=0)
    pos = jnp.concatenate([jnp.arange(t_pad), past_len + jnp.arange(n_s) % ds, jnp.zeros((m_s - n_s,), I32)])
    tab = _rope_tables(pos)

    lay, ref_off, total = _even_layout(d)
    w0 = _permute_w_in0(w_in0, lay, ref_off, total).astype(BF16)
    gate_w = _gate_weights(w_b_gate2, b_b_gate, lay)
    seg = lambda a, name: a[:, lay[name][0]:lay[name][0] + lay[name][1]]

    xn = _rmsnorm(h, norm_mix[0], BF16)
    p0 = _matmul(xn, w0, F32)
    aq, iq, ak, ik, k_bf, v_bf, ik_bf = _rope(p0, tab, lay)
    a_kv = ak.shape[1]
    n_heads_a = aq.shape[1] // HEAD_DIM
    rep_a = n_heads_a // A_KV_HEADS

    to_t = lambda a: a[:t_pad].reshape(n_blk, ROW_BLOCK, a.shape[1]).transpose(0, 2, 1)
    oa_p = _dsa_prompt(to_t(aq), to_t(iq), to_t(seg(p0, "iw")), ik_bf, k_bf, v_bf, t_pad, t_p, topk_p)
    ob_p, state_p = _gla_prompt(p0, lay, t_pad, t_p, gate_w, b_head_norm)

    iq_s = iq[srows].reshape(db, ds, IDX_HEADS, IDX_DIM).transpose(0, 2, 1, 3)
    iq_s = jnp.pad(iq_s, ((0, 0), (0, 0), (0, Q_PAD - ds), (0, 0))).reshape(db, IDX_HEADS * Q_PAD, IDX_DIM)
    iw_s = seg(p0, "iw")[srows].reshape(db, ds, IDX_HEADS).transpose(0, 2, 1)
    iw_s = jnp.pad(iw_s, ((0, 0), (0, 0), (0, Q_PAD - ds))).reshape(db, IDX_HEADS * Q_PAD, 1)
    iw_s = jnp.broadcast_to(iw_s, (db, IDX_HEADS * Q_PAD, LANES))
    new_ik = _pad_rows(ik_bf[srows].reshape(db, ds, IDX_DIM), page)
    scores = _idx_sample(page_table, iq_s, iw_s, new_ik, cache_a_idx_k)
    n_keys = scores.shape[2]
    n_q = _round_up(db * Q_PAD, LANES)
    scores_t = jnp.pad(scores.reshape(db * Q_PAD, n_keys).T, ((0, 0), (0, n_q - db * Q_PAD)))
    bias = _topk_sample(scores_t, topk_s, past_len, ds)[:, :db * Q_PAD].T.reshape(db, Q_PAD, n_keys)
    aq_s = aq[srows].reshape(db, ds, A_KV_HEADS, rep_a, HEAD_DIM).transpose(0, 2, 3, 1, 4)
    aq_s = jnp.pad(aq_s, ((0, 0), (0, 0), (0, 0), (0, Q_PAD - ds), (0, 0))).reshape(db, A_KV_HEADS, rep_a * Q_PAD, HEAD_DIM)
    new_k = _pad_rows(k_bf[srows].reshape(db, ds, a_kv), page)
    new_v = _pad_rows(v_bf[srows].reshape(db, ds, a_kv), page)
    oa_s = _attn_sample(page_table, aq_s, bias, new_k, new_v,
                        cache_a_k.reshape(n_pool, page, a_kv), cache_a_v.reshape(n_pool, page, a_kv))
    oa_s = oa_s.reshape(db, A_KV_HEADS, rep_a, Q_PAD, HEAD_DIM)[:, :, :, :ds].transpose(0, 3, 1, 2, 4)
    oa_s = oa_s.reshape(n_s, n_heads_a * HEAD_DIM).astype(BF16)

    chunk_s = lambda name: _pad_rows(seg(p0, name)[srows].reshape(db, ds, lay[name][1]), GLA_SUB)
    ob_s, state_s = _gla_sample(chunk_s("bq"), chunk_s("bk"), chunk_s("bv"), chunk_s("br"), chunk_s("misc"),
                                state_b, ds, gate_w, b_head_norm)
    ob_s = ob_s[:, :ds].reshape(n_s, -1)

    mix0 = jnp.concatenate([jnp.concatenate([oa_p, ob_p], axis=1), jnp.concatenate([oa_s, ob_s], axis=1),
                            jnp.zeros((m_s - n_s, d), BF16)], axis=0)
    h, xn = _proj_res_norm(mix0, w_out0.astype(BF16), h, norm_ffn[0])
    h, xn = _mlp(xn, h, w_ff1[0].astype(BF16), w_ff2[0].astype(BF16), norm_mix[1], BF16)

    c_kv = C_KV_HEADS * HEAD_DIM
    c_q = w_in1.shape[1] - 2 * c_kv
    rep_c = c_q // c_kv
    q1 = _matmul(xn, w_in1[:, :c_q].astype(BF16), BF16)
    kv1 = _matmul(xn, w_in1[:, c_q:].astype(BF16), F32)
    kv1_bf = kv1.astype(BF16)
    oc_p = _sb_prompt(q1, kv1_bf[:, :c_kv], kv1_bf[:, c_kv:], t_pad)
    q1_s = q1[srows].reshape(db, ds, C_KV_HEADS, rep_c, HEAD_DIM).transpose(0, 2, 3, 1, 4)
    q1_s = q1_s.reshape(db, C_KV_HEADS, rep_c * ds, HEAD_DIM)
    new_ck = _pad_rows(kv1_bf[srows, :c_kv].reshape(db, ds, c_kv), page)
    new_cv = _pad_rows(kv1_bf[srows, c_kv:].reshape(db, ds, c_kv), page)
    oc_s = _sb_sample(page_table, q1_s, new_ck, new_cv, cache_c_k.reshape(n_pool, page, c_kv),
                      cache_c_v.reshape(n_pool, page, c_kv), ds)
    oc_s = oc_s.reshape(db, C_KV_HEADS, rep_c, ds, HEAD_DIM).transpose(0, 3, 1, 2, 4).reshape(n_s, c_q).astype(BF16)
    mix1 = jnp.concatenate([oc_p, oc_s, jnp.zeros((m_s - n_s, c_q), BF16)], axis=0)
    h, xn = _proj_res_norm(mix1, w_out1.astype(BF16), h, norm_ffn[1])
    h, y = _mlp(xn, h, w_ff1[1].astype(BF16), w_ff2[1].astype(BF16), norm_final, F32)

    y_prompt = y[N_META:t_p][None]
    y_sample = y[srows].reshape(db, ds, d)
    kv_p = lambda a, heads: a[:t_p].reshape(1, t_p, heads, HEAD_DIM)
    kv_s = lambda a, heads: a[srows].reshape(db, ds, heads, HEAD_DIM)
    return (y_prompt, y_sample,
            kv_p(ak, A_KV_HEADS), kv_p(seg(p0, "av"), A_KV_HEADS), ik[:t_p, :IDX_DIM][None], state_p[None],
            kv_p(kv1[:, :c_kv], C_KV_HEADS), kv_p(kv1[:, c_kv:], C_KV_HEADS),
            kv_s(ak, A_KV_HEADS), kv_s(seg(p0, "av"), A_KV_HEADS), ik[srows, :IDX_DIM].reshape(db, ds, IDX_DIM),
            state_s, kv_s(kv1[:, :c_kv], C_KV_HEADS), kv_s(kv1[:, c_kv:], C_KV_HEADS))
```

```python
import functools
import math

import jax
import jax.numpy as jnp
from jax import lax
from jax.experimental import pallas as pl
from jax.experimental.pallas import tpu as pltpu

F32 = jnp.float32
BF16 = jnp.bfloat16
I32 = jnp.int32

N_META = 16
HEAD_DIM = 128
ROPE_THETA = 10000.0
EPS = 1e-6
A_KV_HEADS = 2
IDX_HEADS = 16
IDX_DIM = 64
TOPK_MAX = 256
B_HEADS = 4
B_GATE_RANK = 16
B_GATE_TAU = 16.0
C_KV_HEADS = 4

LANES = 128
SUBLANES = 8
VMEM_LIMIT_BYTES = 56 * 1024 * 1024

ROW_BLOCK = 128
INT_MIN = -(2 ** 31)


def _round_up(x, m):
    return (x + m - 1) // m * m


def _row_tile(m, cap):
    best = ROW_BLOCK
    for k in range(1, m // ROW_BLOCK + 1):
        t = k * ROW_BLOCK
        if t <= cap and m % t == 0:
            best = t
    return best


def _col_tile(n, cap):
    best = LANES
    for k in range(1, n // LANES + 1):
        t = k * LANES
        if t <= cap and n % t == 0:
            best = t
    return best


def _params(*sem):
    return pltpu.CompilerParams(dimension_semantics=sem, vmem_limit_bytes=VMEM_LIMIT_BYTES)


def _rms(x, g):
    ms = jnp.mean(x * x, axis=-1, keepdims=True)
    return x * lax.rsqrt(ms + EPS) * g


def _rmsnorm_kernel(x_ref, g_ref, o_ref):
    o_ref[...] = _rms(x_ref[...], g_ref[...]).astype(o_ref.dtype)


def _rmsnorm(x, g, out_dtype):
    m, d = x.shape
    tm = _row_tile(m, 512)
    return pl.pallas_call(
        _rmsnorm_kernel,
        out_shape=jax.ShapeDtypeStruct((m, d), out_dtype),
        grid=(m // tm,),
        in_specs=[pl.BlockSpec((tm, d), lambda i: (i, 0)), pl.BlockSpec((1, d), lambda i: (0, 0))],
        out_specs=pl.BlockSpec((tm, d), lambda i: (i, 0)),
        compiler_params=_params("parallel"),
        name="rmsnorm",
    )(x, g.reshape(1, d))


def _matmul_kernel(x_ref, w_ref, o_ref):
    o_ref[...] = jnp.dot(x_ref[...], w_ref[...], preferred_element_type=F32).astype(o_ref.dtype)


def _matmul(x, w, out_dtype):
    m, k = x.shape
    n = w.shape[1]
    tm = _row_tile(m, 768)
    tn = _col_tile(n, 1152)
    return pl.pallas_call(
        _matmul_kernel,
        out_shape=jax.ShapeDtypeStruct((m, n), out_dtype),
        grid=(m // tm, n // tn),
        in_specs=[pl.BlockSpec((tm, k), lambda i, j: (i, 0)), pl.BlockSpec((k, tn), lambda i, j: (0, j))],
        out_specs=pl.BlockSpec((tm, tn), lambda i, j: (i, j)),
        compiler_params=_params("parallel", "parallel"),
        name="matmul",
    )(x, w)


def _proj_res_norm_kernel(x_ref, w_ref, h_ref, g_ref, hout_ref, xn_ref):
    hn = h_ref[...] + jnp.dot(x_ref[...], w_ref[...], preferred_element_type=F32)
    hout_ref[...] = hn
    xn_ref[...] = _rms(hn, g_ref[...]).astype(xn_ref.dtype)


def _proj_res_norm(x, w, h, g):
    m, k = x.shape
    d = w.shape[1]
    tm = _row_tile(m, 384)
    return pl.pallas_call(
        _proj_res_norm_kernel,
        out_shape=(jax.ShapeDtypeStruct((m, d), F32), jax.ShapeDtypeStruct((m, d), BF16)),
        grid=(m // tm,),
        in_specs=[pl.BlockSpec((tm, k), lambda i: (i, 0)), pl.BlockSpec((k, d), lambda i: (0, 0)),
                  pl.BlockSpec((tm, d), lambda i: (i, 0)), pl.BlockSpec((1, d), lambda i: (0, 0))],
        out_specs=(pl.BlockSpec((tm, d), lambda i: (i, 0)), pl.BlockSpec((tm, d), lambda i: (i, 0))),
        compiler_params=_params("parallel"),
        name="proj_res_norm",
    )(x, w, h, g.reshape(1, d))


def _mlp_kernel(xn_ref, h_ref, w1_ref, w2_ref, g_ref, hout_ref, xnout_ref):
    f = pl.program_id(1)

    @pl.when(f == 0)
    def _():
        hout_ref[...] = h_ref[...]

    a = jnp.maximum(jnp.dot(xn_ref[...], w1_ref[...], preferred_element_type=F32), 0.0)
    hout_ref[...] += jnp.dot((a * a).astype(BF16), w2_ref[...], preferred_element_type=F32)

    @pl.when(f == pl.num_programs(1) - 1)
    def _():
        xnout_ref[...] = _rms(hout_ref[...], g_ref[...]).astype(xnout_ref.dtype)


def _mlp(xn, h, w1, w2, layer, g, norm_dtype):
    m, d = xn.shape
    dff = w1.shape[2]
    tm = _row_tile(m, 384)
    tf = _col_tile(dff, 1024)
    return pl.pallas_call(
        _mlp_kernel,
        out_shape=(jax.ShapeDtypeStruct((m, d), F32), jax.ShapeDtypeStruct((m, d), norm_dtype)),
        grid=(m // tm, dff // tf),
        in_specs=[pl.BlockSpec((tm, d), lambda i, f: (i, 0)), pl.BlockSpec((tm, d), lambda i, f: (i, 0)),
                  pl.BlockSpec((None, d, tf), lambda i, f: (layer, 0, f)),
                  pl.BlockSpec((None, tf, d), lambda i, f: (layer, f, 0)),
                  pl.BlockSpec((1, d), lambda i, f: (0, 0))],
        out_specs=(pl.BlockSpec((tm, d), lambda i, f: (i, 0)), pl.BlockSpec((tm, d), lambda i, f: (i, 0))),
        compiler_params=_params("parallel", "arbitrary"),
        name="mlp",
    )(xn, h, w1, w2, g.reshape(1, d))


def _even_layout(d_model):
    a_q = d_model // 2
    a_kv = A_KV_HEADS * HEAD_DIM
    idx_q = IDX_HEADS * IDX_DIM
    b_v = d_model // 2
    b_qk = b_v // 2
    ref_sizes = dict(aq=a_q, ak=a_kv, av=a_kv, iq=idx_q, ik=IDX_DIM, iw=IDX_HEADS, bq=b_qk, bk=b_qk, bv=b_v,
                     bglr=B_GATE_RANK, br=b_v)
    ref_off, off = {}, 0
    for name, size in ref_sizes.items():
        ref_off[name] = off
        off += size
    lay, off = {}, 0
    for name in ("aq", "iq", "bq", "bk", "bv", "br", "ak", "av", "ik", "iw", "bglr"):
        lay[name] = (off, ref_sizes[name])
        off += ref_sizes[name]
    lay["misc"] = (lay["ik"][0], LANES)
    total = lay["misc"][0] + LANES
    assert off <= total
    return lay, ref_off, total


def _permute_w_in0(w_in0, lay, ref_off, total):
    w = jnp.zeros((w_in0.shape[0], total), w_in0.dtype)
    for name, (off, size) in lay.items():
        if name != "misc":
            w = w.at[:, off:off + size].set(w_in0[:, ref_off[name]:ref_off[name] + size])
    return w


def _rope_tables(pos):
    def tab(d):
        half = d // 2
        inv = ROPE_THETA ** (-jnp.arange(half, dtype=F32) * (2.0 / d))
        ang = pos.astype(F32)[:, None] * inv[None, :]
        c, s = jnp.cos(ang), jnp.sin(ang)
        reps = LANES // d
        return jnp.tile(jnp.concatenate([c, c], 1), (1, reps)), jnp.tile(jnp.concatenate([-s, s], 1), (1, reps))
    c128, s128 = tab(HEAD_DIM)
    c64, s64 = tab(IDX_DIM)
    return jnp.concatenate([c128, s128, c64, s64], axis=1)


def _rope_kernel(q_ref, kv_ref, tab_ref, aq_ref, iq_ref, ak_ref, ik_ref, kbf_ref, vbf_ref, ikbf_ref,
                 *, n_aq, n_iq, n_ak):
    cos128 = tab_ref[:, 0:128]
    sin128 = tab_ref[:, 128:256]
    cos64 = tab_ref[:, 256:384]
    sin64 = tab_ref[:, 384:512]
    lane = lax.broadcasted_iota(I32, cos64.shape, 1)
    low_half = (lane % IDX_DIM) < (IDX_DIM // 2)

    def rope128(x):
        return x * cos128 + pltpu.roll(x, HEAD_DIM // 2, 1) * sin128

    def rope64(x):
        rot = jnp.where(low_half, pltpu.roll(x, LANES - IDX_DIM // 2, 1), pltpu.roll(x, IDX_DIM // 2, 1))
        return x * cos64 + rot * sin64

    for c in range(n_aq):
        sl = slice(c * LANES, (c + 1) * LANES)
        aq_ref[:, sl] = (rope128(q_ref[:, sl]) * EXP2_SCALE).astype(aq_ref.dtype)
    for c in range(n_iq):
        sl = slice(c * LANES, (c + 1) * LANES)
        iq_ref[:, sl] = rope64(q_ref[:, n_aq * LANES + c * LANES:n_aq * LANES + (c + 1) * LANES]).astype(iq_ref.dtype)
    for c in range(n_ak):
        sl = slice(c * LANES, (c + 1) * LANES)
        kr = rope128(kv_ref[:, sl])
        ak_ref[:, sl] = kr
        kbf_ref[:, sl] = kr.astype(BF16)
    vbf_ref[...] = kv_ref[:, n_ak * LANES:2 * n_ak * LANES].astype(BF16)
    ik = rope64(kv_ref[:, 2 * n_ak * LANES:(2 * n_ak + 1) * LANES])
    ik_ref[...] = ik
    ikbf_ref[...] = ik[:, :IDX_DIM].astype(BF16)


def _rope(p0, tab, lay):
    m = p0.shape[0]
    tr = _row_tile(m, 384)
    a_q, a_kv, idx_q = lay["aq"][1], lay["ak"][1], lay["iq"][1]
    qw = a_q + idx_q
    kvw = 2 * a_kv + LANES
    assert lay["aq"][0] == 0 and lay["iq"][0] == a_q and lay["ak"][0] % kvw == 0
    kern = functools.partial(_rope_kernel, n_aq=a_q // LANES, n_iq=idx_q // LANES, n_ak=a_kv // LANES)
    row = lambda i: (i, 0)
    return pl.pallas_call(
        kern,
        out_shape=(jax.ShapeDtypeStruct((m, a_q), BF16), jax.ShapeDtypeStruct((m, idx_q), BF16),
                   jax.ShapeDtypeStruct((m, a_kv), F32), jax.ShapeDtypeStruct((m, LANES), F32),
                   jax.ShapeDtypeStruct((m, a_kv), BF16), jax.ShapeDtypeStruct((m, a_kv), BF16),
                   jax.ShapeDtypeStruct((m, IDX_DIM), BF16)),
        grid=(m // tr,),
        in_specs=[pl.BlockSpec((tr, qw), row),
                  pl.BlockSpec((tr, kvw), lambda i: (i, lay["ak"][0] // kvw)),
                  pl.BlockSpec((tr, 4 * LANES), row)],
        out_specs=(pl.BlockSpec((tr, a_q), row), pl.BlockSpec((tr, idx_q), row), pl.BlockSpec((tr, a_kv), row),
                   pl.BlockSpec((tr, LANES), row), pl.BlockSpec((tr, a_kv), row), pl.BlockSpec((tr, a_kv), row),
                   pl.BlockSpec((tr, IDX_DIM), row)),
        compiler_params=_params("parallel"),
        name="rope",
    )(p0, p0, tab)


GLA_SUB = 16


def _split_bf16(x):
    hi = x.astype(BF16)
    return hi, (x - hi.astype(F32)).astype(BF16)


def _log_sigmoid(x):
    return jnp.minimum(x, 0.0) - jnp.log(1.0 + jnp.exp(-jnp.abs(x)))


def _dot(a, b):
    return jnp.dot(a, b, preferred_element_type=F32)


def _dot_nt(a, b):
    return lax.dot_general(a, b, (((1,), (1,)), ((), ())), preferred_element_type=F32)


def _dot_tn(a, b):
    return lax.dot_general(a, b, (((0,), (0,)), ((), ())), preferred_element_type=F32)


def _gla_gate(misc, w2_hi, w2_lo, bias, valid):
    m_hi, m_lo = _split_bf16(misc)
    x = _dot(m_hi, w2_hi) + _dot(m_lo, w2_hi) + _dot(m_hi, w2_lo) + bias
    return jnp.where(valid, _log_sigmoid(x) * (1.0 / B_GATE_TAU), 0.0)


def _gla_chunk(q, k, v, g, state_ref, dk, dv):
    c = q.shape[0]
    n_sub = c // GLA_SUB
    ri = lax.broadcasted_iota(I32, (c, c), 0)
    ci = lax.broadcasted_iota(I32, (c, c), 1)
    causal = ri >= ci
    tri = jnp.where(causal, 1.0, 0.0).astype(BF16)
    g_hi, g_lo = _split_bf16(g)
    gcum = _dot(tri, g_hi) + _dot(tri, g_lo)
    scale = dk ** -0.5
    pad = _round_up(c, LANES)
    outs = []
    for h in range(B_HEADS):
        gh = gcum[:, h * dk:(h + 1) * dk]
        qh = q[:, h * dk:(h + 1) * dk] * scale
        kh = k[:, h * dk:(h + 1) * dk]
        vh = v[:, h * dv:(h + 1) * dv].astype(BF16)
        q_parts, k_parts = [], []
        for i in range(n_sub):
            lo, hi = i * GLA_SUB, (i + 1) * GLA_SUB
            base = gh[lo - 1:lo, :] if i else jnp.zeros((1, dk), F32)
            q_parts.append(qh[lo:hi] * jnp.exp(gh[lo:hi] - base))
            k_parts.append(kh[:hi] * jnp.exp(base - gh[:hi]))
            if pad > hi:
                k_parts.append(jnp.zeros((pad - hi, dk), F32))
        qt = jnp.concatenate(q_parts, axis=0).astype(BF16)
        kt = jnp.concatenate(k_parts, axis=0).astype(BF16)
        r = _dot_nt(qt, kt)
        att = jnp.concatenate([r[i * GLA_SUB:(i + 1) * GLA_SUB, i * pad:i * pad + c] for i in range(n_sub)], axis=0)
        att = jnp.where(causal, att, 0.0).astype(BF16)
        s = state_ref[h]
        o = _dot(att, vh) + _dot((qh * jnp.exp(gh)).astype(BF16), s.astype(BF16))
        last = gh[c - 1:c, :]
        k_tail = (kh * jnp.exp(last - gh)).astype(BF16)
        decay = jnp.transpose(jnp.broadcast_to(jnp.exp(last), (dk, dk)))
        decay = jnp.concatenate([decay] * (dv // dk), axis=1)
        state_ref[h] = decay * s + _dot_tn(k_tail, vh)
        outs.append(o)
    return outs


def _gla_out(outs, r, head_norm, dv):
    res = []
    for h, o in enumerate(outs):
        gate = r[:, h * dv:(h + 1) * dv]
        gate = gate * (1.0 / (1.0 + jnp.exp(-gate)))
        res.append(_rms(o, head_norm) * gate)
    return jnp.concatenate(res, axis=1)


def _gla_prompt_kernel(q_ref, k_ref, v_ref, r_ref, misc_ref, w2h_ref, w2l_ref, b_ref, hn_ref, o_ref, state_ref,
                       *, chunk, n_valid, dk, dv):
    blk = pl.program_id(0)

    @pl.when(blk == 0)
    def _():
        state_ref[...] = jnp.zeros_like(state_ref)

    rows = q_ref.shape[0]
    for c0 in range(0, rows, chunk):
        sl = slice(c0, c0 + chunk)
        row = blk * rows + c0 + lax.broadcasted_iota(I32, (chunk, 1), 0)
        g = _gla_gate(misc_ref[sl, :], w2h_ref[...], w2l_ref[...], b_ref[...], row < n_valid)
        outs = _gla_chunk(q_ref[sl, :], k_ref[sl, :], v_ref[sl, :], g, state_ref, dk, dv)
        o_ref[sl, :] = _gla_out(outs, r_ref[sl, :], hn_ref[...], dv).astype(o_ref.dtype)


def _gla_sample_kernel(q_ref, k_ref, v_ref, r_ref, misc_ref, w2h_ref, w2l_ref, b_ref, hn_ref, s_ref, o_ref,
                       state_ref, *, n_valid, dk, dv):
    state_ref[0] = s_ref[0]
    rows = q_ref.shape[1]
    row = lax.broadcasted_iota(I32, (rows, 1), 0)
    g = _gla_gate(misc_ref[0], w2h_ref[...], w2l_ref[...], b_ref[...], row < n_valid)
    outs = _gla_chunk(q_ref[0], k_ref[0], v_ref[0], g, state_ref.at[0], dk, dv)
    o_ref[0] = _gla_out(outs, r_ref[0], hn_ref[...], dv).astype(o_ref.dtype)


def _gate_weights(w_b_gate2, b_b_gate, lay):
    off = lay["bglr"][0] - lay["misc"][0]
    w2 = jnp.zeros((LANES, w_b_gate2.shape[1]), F32).at[off:off + B_GATE_RANK].set(w_b_gate2)
    hi = w2.astype(BF16)
    lo = (w2 - hi.astype(F32)).astype(BF16)
    return hi, lo, b_b_gate.reshape(1, -1)


def _gla_prompt(p0, lay, n_rows, n_valid, gate_w, head_norm, chunk=64):
    bqk, bv = lay["bq"][1], lay["bv"][1]
    dk, dv = bqk // B_HEADS, bv // B_HEADS
    w2h, w2l, bias = gate_w
    col = lambda name: (lambda i, o=lay[name][0] // lay[name][1]: (i, o))
    const = lambda i: (0, 0)
    kern = functools.partial(_gla_prompt_kernel, chunk=chunk, n_valid=n_valid, dk=dk, dv=dv)
    return pl.pallas_call(
        kern,
        out_shape=(jax.ShapeDtypeStruct((n_rows, bv), BF16), jax.ShapeDtypeStruct((B_HEADS, dk, dv), F32)),
        grid=(n_rows // ROW_BLOCK,),
        in_specs=[pl.BlockSpec((ROW_BLOCK, bqk), col("bq")), pl.BlockSpec((ROW_BLOCK, bqk), col("bk")),
                  pl.BlockSpec((ROW_BLOCK, bv), col("bv")), pl.BlockSpec((ROW_BLOCK, bv), col("br")),
                  pl.BlockSpec((ROW_BLOCK, LANES), col("misc")),
                  pl.BlockSpec(w2h.shape, const), pl.BlockSpec(w2l.shape, const), pl.BlockSpec(bias.shape, const),
                  pl.BlockSpec((1, dv), const)],
        out_specs=(pl.BlockSpec((ROW_BLOCK, bv), lambda i: (i, 0)),
                   pl.BlockSpec((B_HEADS, dk, dv), lambda i: (0, 0, 0))),
        compiler_params=_params("arbitrary"),
        name="gla_prompt",
    )(p0, p0, p0, p0, p0, w2h, w2l, bias, head_norm.reshape(1, dv))


def _gla_sample(q, k, v, r, misc, state, n_valid, gate_w, head_norm):
    db, rows, bqk = q.shape
    bv = v.shape[2]
    dk, dv = bqk // B_HEADS, bv // B_HEADS
    w2h, w2l, bias = gate_w
    seq = lambda i: (i, 0, 0)
    const = lambda i: (0, 0)
    kern = functools.partial(_gla_sample_kernel, n_valid=n_valid, dk=dk, dv=dv)
    return pl.pallas_call(
        kern,
        out_shape=(jax.ShapeDtypeStruct((db, rows, bv), BF16), jax.ShapeDtypeStruct(state.shape, F32)),
        grid=(db,),
        in_specs=[pl.BlockSpec((1, rows, bqk), seq), pl.BlockSpec((1, rows, bqk), seq),
                  pl.BlockSpec((1, rows, bv), seq), pl.BlockSpec((1, rows, bv), seq),
                  pl.BlockSpec((1, rows, LANES), seq),
                  pl.BlockSpec(w2h.shape, const), pl.BlockSpec(w2l.shape, const), pl.BlockSpec(bias.shape, const),
                  pl.BlockSpec((1, dv), const),
                  pl.BlockSpec((1, B_HEADS, dk, dv), lambda i: (i, 0, 0, 0))],
        out_specs=(pl.BlockSpec((1, rows, bv), seq), pl.BlockSpec((1, B_HEADS, dk, dv), lambda i: (i, 0, 0, 0))),
        compiler_params=_params("parallel"),
        name="gla_sample",
    )(q, k, v, r, misc, w2h, w2l, bias, head_norm.reshape(1, dv), state)


SB_DONE = -120.0


def _sb_suffix_matrix():
    j = lax.broadcasted_iota(I32, (LANES, 2 * LANES), 0)
    s = lax.broadcasted_iota(I32, (LANES, 2 * LANES), 1)
    return jnp.where((j > s) | (s >= LANES), 1.0, 0.0).astype(BF16)


def _sb_block(q, k, v, strict, carry, acc, suffix):
    z = _dot_nt(q, k) * (HEAD_DIM ** -0.5)
    log_keep = _log_sigmoid(-z)
    lk = log_keep if strict is None else jnp.where(strict, log_keep, 0.0)
    hi, lo = _split_bf16(lk)
    sums = _dot(hi, suffix) + _dot(lo, suffix)
    later = sums[:, :LANES] + carry
    w = jnp.exp(z + log_keep + later)
    if strict is not None:
        w = jnp.where(strict, w, 0.0)
    return carry + sums[:, LANES:], acc + _dot(w.astype(BF16), v)


def _sb_prompt_kernel(q_ref, k_ref, v_ref, o_ref, carry_ref, acc_ref, *, n_kv, rep):
    b = pl.program_id(0)
    suffix = _sb_suffix_matrix()
    carry_ref[...] = jnp.zeros_like(carry_ref)
    acc_ref[...] = jnp.zeros_like(acc_ref)
    rows = rep * ROW_BLOCK
    q_row = lax.broadcasted_iota(I32, (rows, LANES), 0) % ROW_BLOCK
    col = lax.broadcasted_iota(I32, (rows, LANES), 1)

    def cond(st):
        return (st[0] >= 0) & (st[1] == 0)

    def body(st):
        kb = st[0]
        k0 = pl.multiple_of(kb * ROW_BLOCK, ROW_BLOCK)
        strict = (kb * ROW_BLOCK + col) < (b * ROW_BLOCK + q_row)
        top = jnp.full((), -jnp.inf, F32)
        for g in range(n_kv):
            qg = jnp.concatenate([q_ref[:, (g * rep + r) * HEAD_DIM:(g * rep + r + 1) * HEAD_DIM]
                                  for r in range(rep)], axis=0)
            kg = k_ref[pl.ds(k0, ROW_BLOCK), g * HEAD_DIM:(g + 1) * HEAD_DIM]
            vg = v_ref[pl.ds(k0, ROW_BLOCK), g * HEAD_DIM:(g + 1) * HEAD_DIM]
            c, a = _sb_block(qg, kg, vg, strict, carry_ref[g], acc_ref[g], suffix)
            carry_ref[g] = c
            acc_ref[g] = a
            top = jnp.maximum(top, jnp.max(c))
        return kb - 1, (top < SB_DONE).astype(I32)

    lax.while_loop(cond, body, (b, jnp.zeros((), I32)))
    for g in range(n_kv):
        for r in range(rep):
            h = g * rep + r
            o_ref[:, h * HEAD_DIM:(h + 1) * HEAD_DIM] = acc_ref[g, r * ROW_BLOCK:(r + 1) * ROW_BLOCK, :].astype(o_ref.dtype)


def _sb_prompt(q, k, v, n_rows):
    cq = q.shape[1]
    ckv = k.shape[1]
    n_kv = ckv // HEAD_DIM
    rep = cq // ckv
    kern = functools.partial(_sb_prompt_kernel, n_kv=n_kv, rep=rep)
    return pl.pallas_call(
        kern,
        out_shape=jax.ShapeDtypeStruct((n_rows, cq), BF16),
        grid=(n_rows // ROW_BLOCK,),
        in_specs=[pl.BlockSpec((ROW_BLOCK, cq), lambda i: (i, 0)),
                  pl.BlockSpec((n_rows, ckv), lambda i: (0, 0)),
                  pl.BlockSpec((n_rows, ckv), lambda i: (0, 0))],
        out_specs=pl.BlockSpec((ROW_BLOCK, cq), lambda i: (i, 0)),
        scratch_shapes=[pltpu.VMEM((n_kv, rep * ROW_BLOCK, LANES), F32),
                        pltpu.VMEM((n_kv, rep * ROW_BLOCK, HEAD_DIM), F32)],
        compiler_params=_params("parallel"),
        name="stick_prompt",
    )(q, k, v)


def _page_head(ref, g, n_heads):
    keys = ref.shape[1] // n_heads
    return ref[0, pl.ds(g, keys, stride=n_heads), :].astype(BF16)


def _sb_sample_kernel(pt_ref, q_ref, *refs, n_kv, n_new, dec_seq, n_pages_step, has_init):
    del pt_ref
    refs = list(refs)
    new_refs = [refs.pop(0), refs.pop(0)] if n_new else None
    k_pages = [refs.pop(0) for _ in range(n_pages_step)]
    v_pages = [refs.pop(0) for _ in range(n_pages_step)]
    init_refs = [refs.pop(0), refs.pop(0)] if has_init else None
    o_ref, carry_ref = refs
    suffix = _sb_suffix_matrix()
    rows = q_ref.shape[2]

    @pl.when(pl.program_id(1) == 0)
    def _():
        if has_init:
            o_ref[...] = init_refs[0][...]
            carry_ref[...] = init_refs[1][...]
        else:
            o_ref[...] = jnp.zeros_like(o_ref)
            carry_ref[...] = jnp.zeros_like(carry_ref)

    def visit(k_of, v_of, strict):
        @pl.when(jnp.max(carry_ref[...]) >= SB_DONE)
        def _():
            for g in range(n_kv):
                c, a = _sb_block(q_ref[0, g], k_of(g), v_of(g), strict, carry_ref[0, g], o_ref[0, g], suffix)
                carry_ref[0, g] = c
                o_ref[0, g] = a

    cols = lambda ref, g: ref[0, :, g * HEAD_DIM:(g + 1) * HEAD_DIM].astype(BF16)
    if n_new:
        q_idx = lax.broadcasted_iota(I32, (rows, LANES), 0) % dec_seq
        col = lax.broadcasted_iota(I32, (rows, LANES), 1)
        visit(lambda g: cols(new_refs[0], g), lambda g: cols(new_refs[1], g), col < q_idx)
    for kp, vp in zip(k_pages, v_pages):
        visit(lambda g, kp=kp: _page_head(kp, g, n_kv), lambda g, vp=vp: _page_head(vp, g, n_kv), None)


def _sb_sample_call(page_table, q, new_kv, cache_k, cache_v, init, dec_seq, first_page, n_steps, pages_per_step):
    db, n_kv, rows, _ = q.shape
    page_rows = cache_k.shape[1]
    assert page_rows == LANES * n_kv
    seq4 = lambda b, c, pt: (b, 0, 0, 0)
    in_specs = [pl.BlockSpec((1, n_kv, rows, HEAD_DIM), seq4)]
    args = [q]
    if new_kv is not None:
        in_specs += [pl.BlockSpec((1,) + new_kv[0].shape[1:], lambda b, c, pt: (b, 0, 0))] * 2
        args += list(new_kv)
    for cache in (cache_k, cache_v):
        for i in range(pages_per_step):
            in_specs.append(pl.BlockSpec(
                (1, page_rows, HEAD_DIM), lambda b, c, pt, i=i: (pt[b, first_page - c * pages_per_step - i], 0, 0)))
            args.append(cache)
    if init is not None:
        in_specs += [pl.BlockSpec((1, n_kv, rows, HEAD_DIM), seq4)] * 2
        args += list(init)
    kern = functools.partial(_sb_sample_kernel, n_kv=n_kv, n_new=new_kv is not None, dec_seq=dec_seq,
                             n_pages_step=pages_per_step, has_init=init is not None)
    out_sds = jax.ShapeDtypeStruct((db, n_kv, rows, HEAD_DIM), F32)
    return pl.pallas_call(
        kern,
        out_shape=(out_sds, out_sds),
        grid_spec=pltpu.PrefetchScalarGridSpec(
            num_scalar_prefetch=1,
            grid=(db, n_steps),
            in_specs=in_specs,
            out_specs=(pl.BlockSpec((1, n_kv, rows, HEAD_DIM), seq4), pl.BlockSpec((1, n_kv, rows, HEAD_DIM), seq4)),
        ),
        compiler_params=_params("parallel", "arbitrary"),
        name="stick_sample",
    )(page_table, *args)


def _sortable(score):
    bits = lax.bitcast_convert_type(score + 0.0, I32)
    key = bits ^ (lax.shift_right_arithmetic(bits, 31) & 0x7FFFFFFF)
    return jnp.maximum(key, INT_MIN + 1)


COUNT_ROWS = 64


def _count(key_ref, n_units, unit, pred):
    rows = COUNT_ROWS if unit % COUNT_ROWS == 0 else SUBLANES

    def body(u, acc):
        r0 = pl.multiple_of(u * unit, unit)
        hit = pred(key_ref[pl.ds(r0, unit), :], r0)
        ones = jnp.where(hit, jnp.ones(hit.shape, I32), jnp.zeros(hit.shape, I32))
        return acc + jnp.sum(ones.reshape(unit // rows, rows, LANES), axis=0)
    acc = lax.fori_loop(0, n_units, body, jnp.zeros((rows, LANES), I32))
    return jnp.sum(acc, axis=0, keepdims=True)


def _select_topk(key_ref, n_units, unit, k, n_rows):
    def bit_step(i, tau):
        cand = tau ^ lax.shift_left(jnp.int32(1), 31 - i)
        cnt = _count(key_ref, n_units, unit, lambda keys, r0: keys >= cand)
        return jnp.where(cnt >= k, cand, tau)
    tau = lax.fori_loop(0, 32, bit_step, jnp.full((1, LANES), INT_MIN, I32))
    c_gt = _count(key_ref, n_units, unit, lambda keys, r0: keys > tau)
    c_ge = _count(key_ref, n_units, unit, lambda keys, r0: keys >= tau)
    real = tau > INT_MIN
    tie = real & (c_ge > k)
    need = k - c_gt
    row = lax.broadcasted_iota(I32, (unit, LANES), 0)
    n_bits = max(1, (n_rows - 1).bit_length())

    def break_ties():
        def pos_step(i, x):
            cand = x | lax.shift_left(jnp.int32(1), n_bits - 1 - i)
            cnt = _count(key_ref, n_units, unit, lambda keys, r0: (keys == tau) & ((r0 + row) < cand))
            return jnp.where(cnt < need, cand, x)
        return lax.fori_loop(0, n_bits, pos_step, jnp.zeros((1, LANES), I32))

    last_tie = lax.cond(jnp.max(tie.astype(I32)) > 0, break_ties, lambda: jnp.zeros((1, LANES), I32))
    last = jnp.where(tie, last_tie, jnp.where(real, jnp.int32(2 ** 30), jnp.int32(-1)))
    return tau, last


def _col_reduce(x, fn):
    rows = x.shape[0]
    if rows > COUNT_ROWS and rows % COUNT_ROWS == 0:
        x = fn(x.reshape(rows // COUNT_ROWS, COUNT_ROWS, x.shape[1]), axis=0)
    return fn(x, axis=0, keepdims=True)


def _selected(keys, r0, tau, last):
    row = lax.broadcasted_iota(I32, keys.shape, 0)
    return (keys > tau) | ((keys == tau) & ((r0 + row) <= last))


IDX_SCALE = (IDX_HEADS * IDX_DIM) ** -0.5
MASKED = -1e30
EXP2_SCALE = HEAD_DIM ** -0.5 * math.log2(math.e)


def _dsa_prompt_kernel(aq_ref, iq_ref, iw_ref, ik_ref, k_ref, v_ref, o_ref, key_ref, m_ref, l_ref, acc_ref,
                       *, unit, topk, n_valid, n_heads, n_kv):
    b = pl.program_id(0)
    n_units = ((b + 1) * ROW_BLOCK + unit - 1) // unit
    rep = n_heads // n_kv
    row = lax.broadcasted_iota(I32, (unit, LANES), 0)
    q_pos = b * ROW_BLOCK + lax.broadcasted_iota(I32, (unit, LANES), 1)
    q_real = q_pos < n_valid

    iq_t = iq_ref[0]
    rhs = jnp.concatenate([iq_t[h * IDX_DIM:(h + 1) * IDX_DIM, :] for h in range(IDX_HEADS)], axis=1)
    w_t = iw_ref[0]

    def score_unit(u, carry):
        r0 = pl.multiple_of(u * unit, unit)
        dots = _dot(ik_ref[pl.ds(r0, unit), :], rhs)
        score = jnp.zeros((unit, LANES), F32)
        for h in range(IDX_HEADS):
            score = score + jnp.maximum(dots[:, h * LANES:(h + 1) * LANES], 0.0) * w_t[h:h + 1, :]
        k_pos = r0 + row
        adm = (k_pos <= q_pos) & (q_real | (k_pos == 0))
        key_ref[pl.ds(r0, unit), :] = jnp.where(adm, _sortable(score * IDX_SCALE), INT_MIN)
        return carry

    lax.fori_loop(0, n_units, score_unit, 0)
    tau, last = _select_topk(key_ref, n_units, unit, topk, key_ref.shape[0])

    m_ref[...] = jnp.full_like(m_ref, MASKED)
    l_ref[...] = jnp.zeros_like(l_ref)
    acc_ref[...] = jnp.zeros_like(acc_ref)
    aq_t = aq_ref[0]

    def attend_unit(u, carry):
        r0 = pl.multiple_of(u * unit, unit)
        sel = _selected(key_ref[pl.ds(r0, unit), :], r0, tau, last)
        for g in range(n_kv):
            kg = k_ref[pl.ds(r0, unit), g * HEAD_DIM:(g + 1) * HEAD_DIM]
            vg = v_ref[pl.ds(r0, unit), g * HEAD_DIM:(g + 1) * HEAD_DIM]
            qg = jnp.concatenate([aq_t[(g * rep + r) * HEAD_DIM:(g * rep + r + 1) * HEAD_DIM, :]
                                  for r in range(rep)], axis=1)
            logits = _dot(kg, qg)
            ps = []
            for r in range(rep):
                h = g * rep + r
                lg = jnp.where(sel, logits[:, r * LANES:(r + 1) * LANES], MASKED)
                m_old = m_ref[h:h + 1, :]
                m_new = jnp.maximum(m_old, _col_reduce(lg, jnp.max))
                alpha = jnp.exp2(m_old - m_new)
                p = jnp.exp2(lg - m_new)
                l_ref[h:h + 1, :] = alpha * l_ref[h:h + 1, :] + _col_reduce(p, jnp.sum)
                m_ref[h:h + 1, :] = m_new
                acc_ref[h] = acc_ref[h] * alpha
                ps.append(p.astype(BF16))
            pv = _dot_tn(vg, jnp.concatenate(ps, axis=1))
            for r in range(rep):
                acc_ref[g * rep + r] += pv[:, r * LANES:(r + 1) * LANES]
        return carry

    lax.fori_loop(0, n_units, attend_unit, 0)
    for h in range(n_heads):
        out = acc_ref[h] * (1.0 / l_ref[h:h + 1, :])
        o_ref[:, h * HEAD_DIM:(h + 1) * HEAD_DIM] = jnp.transpose(out).astype(o_ref.dtype)


def _dsa_prompt(aq_t, iq_t, iw_t, ik, k, v, n_rows, n_valid, topk):
    n_blk = n_rows // ROW_BLOCK
    n_heads = aq_t.shape[1] // HEAD_DIM
    n_kv = k.shape[1] // HEAD_DIM
    unit = _row_tile(n_rows, 640)
    blk = lambda i: (i, 0, 0)
    const = lambda i: (0, 0)
    kern = functools.partial(_dsa_prompt_kernel, unit=unit, topk=topk, n_valid=n_valid, n_heads=n_heads, n_kv=n_kv)
    return pl.pallas_call(
        kern,
        out_shape=jax.ShapeDtypeStruct((n_rows, n_heads * HEAD_DIM), BF16),
        grid=(n_blk,),
        in_specs=[pl.BlockSpec((1,) + aq_t.shape[1:], blk), pl.BlockSpec((1,) + iq_t.shape[1:], blk),
                  pl.BlockSpec((1,) + iw_t.shape[1:], blk),
                  pl.BlockSpec((n_rows, ik.shape[1]), const), pl.BlockSpec((n_rows, k.shape[1]), const),
                  pl.BlockSpec((n_rows, v.shape[1]), const)],
        out_specs=pl.BlockSpec((ROW_BLOCK, n_heads * HEAD_DIM), lambda i: (i, 0)),
        scratch_shapes=[pltpu.VMEM((n_rows, LANES), I32),
                        pltpu.VMEM((n_heads, LANES), F32), pltpu.VMEM((n_heads, LANES), F32),
                        pltpu.VMEM((n_heads, HEAD_DIM, LANES), F32)],
        compiler_params=_params("parallel"),
        name="dsa_prompt",
    )(aq_t, iq_t, iw_t, ik, k, v)


Q_PAD = SUBLANES


def _idx_sample_kernel(pt_ref, iq_ref, w_ref, new_ref, *refs, n_pages):
    del pt_ref
    pages, o_ref = refs[:n_pages], refs[n_pages]
    iq = iq_ref[0]
    w = w_ref[0]

    def score(keys_t):
        s = jnp.maximum(_dot(iq, keys_t), 0.0) * w
        return jnp.sum(s.reshape(IDX_HEADS, Q_PAD, LANES), axis=0) * IDX_SCALE

    for p in range(n_pages):
        o_ref[0, :, p * LANES:(p + 1) * LANES] = score(pages[p][0].astype(BF16))
    o_ref[0, :, n_pages * LANES:] = score(new_ref[0])


def _idx_sample(page_table, iq, w, new_ik_t, cache_idx_t):
    db = iq.shape[0]
    n_pages = page_table.shape[1]
    page = cache_idx_t.shape[2]
    assert page == LANES
    seq = lambda b, pt: (b, 0, 0)
    in_specs = [pl.BlockSpec((1,) + iq.shape[1:], seq), pl.BlockSpec((1,) + w.shape[1:], seq),
                pl.BlockSpec((1,) + new_ik_t.shape[1:], seq)]
    in_specs += [pl.BlockSpec((1, IDX_DIM, page), lambda b, pt, p=p: (pt[b, p], 0, 0)) for p in range(n_pages)]
    n_keys = (n_pages + 1) * LANES
    return pl.pallas_call(
        functools.partial(_idx_sample_kernel, n_pages=n_pages),
        out_shape=jax.ShapeDtypeStruct((db, Q_PAD, n_keys), F32),
        grid_spec=pltpu.PrefetchScalarGridSpec(
            num_scalar_prefetch=1, grid=(db,), in_specs=in_specs,
            out_specs=pl.BlockSpec((1, Q_PAD, n_keys), seq)),
        compiler_params=_params("parallel"),
        name="idx_sample",
    )(page_table, iq, w, new_ik_t, *([cache_idx_t] * n_pages))


def _topk_sample_kernel(s_ref, o_ref, key_ref, *, unit, topk, past_len, dec_seq):
    n_keys = s_ref.shape[0]
    n_units = n_keys // unit
    row = lax.broadcasted_iota(I32, (unit, LANES), 0)
    q = lax.broadcasted_iota(I32, (unit, LANES), 1) % Q_PAD
    for u in range(n_units):
        k_pos = u * unit + row
        adm = ((q < dec_seq) & (k_pos <= past_len + q)) | ((q >= dec_seq) & (k_pos == 0))
        key_ref[u * unit:(u + 1) * unit, :] = jnp.where(adm, _sortable(s_ref[u * unit:(u + 1) * unit, :]), INT_MIN)
    tau, last = _select_topk(key_ref, n_units, unit, topk, n_keys)
    for u in range(n_units):
        sel = _selected(key_ref[u * unit:(u + 1) * unit, :], u * unit, tau, last)
        o_ref[u * unit:(u + 1) * unit, :] = jnp.where(sel, 0.0, MASKED)


def _topk_sample(scores_t, topk, past_len, dec_seq):
    n_keys, n_q = scores_t.shape
    unit = _row_tile(n_keys, 640)
    kern = functools.partial(_topk_sample_kernel, unit=unit, topk=topk, past_len=past_len, dec_seq=dec_seq)
    return pl.pallas_call(
        kern,
        out_shape=jax.ShapeDtypeStruct((n_keys, n_q), F32),
        grid=(n_q // LANES,),
        in_specs=[pl.BlockSpec((n_keys, LANES), lambda i: (0, i))],
        out_specs=pl.BlockSpec((n_keys, LANES), lambda i: (0, i)),
        scratch_shapes=[pltpu.VMEM((n_keys, LANES), I32)],
        compiler_params=_params("parallel"),
        name="topk_sample",
    )(scores_t)


def _attn_sample_kernel(pt_ref, q_ref, bias_ref, bias_new_ref, k_new_ref, v_new_ref, *refs, n_kv, pages_per_step):
    del pt_ref
    k_pages = refs[:pages_per_step]
    v_pages = refs[pages_per_step:2 * pages_per_step]
    o_ref, m_ref, l_ref, acc_ref = refs[2 * pages_per_step:]
    c = pl.program_id(1)
    rep = q_ref.shape[2] // Q_PAD

    @pl.when(c == 0)
    def _():
        m_ref[...] = jnp.full_like(m_ref, MASKED)
        l_ref[...] = jnp.zeros_like(l_ref)
        acc_ref[...] = jnp.zeros_like(acc_ref)

    def update(g, kg, vg, bias):
        sel = jnp.concatenate([bias] * rep, axis=0) > 0.5 * MASKED
        lg = jnp.where(sel, _dot_nt(q_ref[0, g], kg), MASKED)
        m_old = m_ref[g]
        m_new = jnp.maximum(m_old, jnp.max(lg, axis=1, keepdims=True))
        alpha = jnp.exp2(m_old - m_new)
        p = jnp.where(sel, jnp.exp2(lg - m_new), 0.0)
        l_ref[g] = alpha * l_ref[g] + jnp.sum(p, axis=1, keepdims=True)
        acc_ref[g] = alpha * acc_ref[g] + _dot(p.astype(BF16), vg)
        m_ref[g] = m_new

    def head(ref, g):
        return ref[0, :, g * HEAD_DIM:(g + 1) * HEAD_DIM].astype(BF16)

    @pl.when(c == 0)
    def _():
        for g in range(n_kv):
            update(g, head(k_new_ref, g), head(v_new_ref, g), bias_new_ref[0])

    for g in range(n_kv):
        kg = jnp.concatenate([_page_head(r, g, n_kv) for r in k_pages], axis=0)
        vg = jnp.concatenate([_page_head(r, g, n_kv) for r in v_pages], axis=0)
        update(g, kg, vg, bias_ref[0])

    @pl.when(c == pl.num_programs(1) - 1)
    def _():
        for g in range(n_kv):
            o_ref[0, g] = acc_ref[g] * (1.0 / l_ref[g])


def _attn_sample(page_table, q, bias, new_k, new_v, cache_k, cache_v):
    db, n_kv, rows, _ = q.shape
    n_pages = page_table.shape[1]
    page_rows = cache_k.shape[1]
    page = page_rows // n_kv
    per_step = max(d for d in range(1, 17) if n_pages % d == 0)
    seq4 = lambda b, c, pt: (b, 0, 0, 0)
    seq3 = lambda b, c, pt: (b, 0, 0)
    in_specs = [pl.BlockSpec((1, n_kv, rows, HEAD_DIM), seq4),
                pl.BlockSpec((1, Q_PAD, per_step * page), lambda b, c, pt: (b, 0, c)),
                pl.BlockSpec((1, Q_PAD, page), lambda b, c, pt: (b, 0, n_pages)),
                pl.BlockSpec((1,) + new_k.shape[1:], seq3), pl.BlockSpec((1,) + new_v.shape[1:], seq3)]
    for _ in range(2):
        in_specs += [pl.BlockSpec((1, page_rows, HEAD_DIM), lambda b, c, pt, i=i: (pt[b, c * per_step + i], 0, 0))
                     for i in range(per_step)]
    return pl.pallas_call(
        functools.partial(_attn_sample_kernel, n_kv=n_kv, pages_per_step=per_step),
        out_shape=jax.ShapeDtypeStruct((db, n_kv, rows, HEAD_DIM), F32),
        grid_spec=pltpu.PrefetchScalarGridSpec(
            num_scalar_prefetch=1, grid=(db, n_pages // per_step), in_specs=in_specs,
            out_specs=pl.BlockSpec((1, n_kv, rows, HEAD_DIM), seq4),
            scratch_shapes=[pltpu.VMEM((n_kv, rows, 1), F32), pltpu.VMEM((n_kv, rows, 1), F32),
                            pltpu.VMEM((n_kv, rows, HEAD_DIM), F32)]),
        compiler_params=_params("parallel", "arbitrary"),
        name="attn_sample",
    )(page_table, q, bias, bias, new_k, new_v, *([cache_k] * per_step), *([cache_v] * per_step))


SB_EAGER_PAGES = 4
SB_PAGES_PER_STEP = 12


def _sb_sample(page_table, q, new_k, new_v, cache_k, cache_v, dec_seq):
    n_pages = page_table.shape[1]
    eager = min(SB_EAGER_PAGES, n_pages)
    o, carry = _sb_sample_call(page_table, q, (new_k, new_v), cache_k, cache_v, None, dec_seq,
                               n_pages - 1, 1, eager)
    rest = n_pages - eager
    if rest == 0:
        return o
    per_step = max(d for d in range(1, SB_PAGES_PER_STEP + 1) if rest % d == 0)
    finish = lambda: _sb_sample_call(page_table, q, None, cache_k, cache_v, (o, carry), dec_seq,
                                     rest - 1, rest // per_step, per_step)[0]
    return lax.cond(jnp.max(carry) >= SB_DONE, finish, lambda: o)


def _pad_rows(a, rows):
    return jnp.pad(a, ((0, 0), (0, rows - a.shape[1])) + ((0, 0),) * (a.ndim - 2))


def kernel(x_prompt, x_sample, cache_a_k, cache_a_v, cache_a_idx_k, state_b, cache_c_k, cache_c_v, page_table,
           meta_tokens, norm_mix, norm_ffn, norm_final, w_in0, w_b_gate2, b_b_gate, b_head_norm, w_out0,
           w_in1, w_out1, w_ff1, w_ff2):
    batch, seq, d = x_prompt.shape
    assert batch == 1
    db, ds, _ = x_sample.shape
    assert ds <= Q_PAD and ds <= GLA_SUB
    t_p = N_META + seq
    t_pad = _round_up(t_p, ROW_BLOCK)
    n_s = db * ds
    m_s = _round_up(n_s, ROW_BLOCK)
    n_blk = t_pad // ROW_BLOCK
    n_pool, page = cache_a_k.shape[:2]
    n_pages = page_table.shape[1]
    past_len = n_pages * page
    topk_p = min(TOPK_MAX, seq // 4)
    topk_s = min(TOPK_MAX, (past_len + ds) // 4)
    srows = slice(t_pad, t_pad + n_s)

    h = jnp.concatenate([meta_tokens.astype(F32), x_prompt[0], jnp.zeros((t_pad - t_p, d), F32),
                         x_sample.reshape(n_s, d), jnp.zeros((m_s - n_s, d), F32)], axis=0)
    pos = jnp.concatenate([jnp.arange(t_pad), past_len + jnp.arange(n_s) % ds, jnp.zeros((m_s - n_s,), I32)])
    tab = _rope_tables(pos)

    lay, ref_off, total = _even_layout(d)
    w0 = _permute_w_in0(w_in0, lay, ref_off, total).astype(BF16)
    gate_w = _gate_weights(w_b_gate2, b_b_gate, lay)
    seg = lambda a, name: a[:, lay[name][0]:lay[name][0] + lay[name][1]]

    xn = _rmsnorm(h, norm_mix[0], BF16)
    p0 = _matmul(xn, w0, F32)
    aq, iq, ak, ik, k_bf, v_bf, ik_bf = _rope(p0, tab, lay)
    a_kv = ak.shape[1]
    n_heads_a = aq.shape[1] // HEAD_DIM
    rep_a = n_heads_a // A_KV_HEADS

    to_t = lambda a: a[:t_pad].reshape(n_blk, ROW_BLOCK, a.shape[1]).transpose(0, 2, 1)
    oa_p = _dsa_prompt(to_t(aq), to_t(iq), to_t(seg(p0, "iw")), ik_bf, k_bf, v_bf, t_pad, t_p, topk_p)
    ob_p, state_p = _gla_prompt(p0, lay, t_pad, t_p, gate_w, b_head_norm)

    iq_s = iq[srows].reshape(db, ds, IDX_HEADS, IDX_DIM).transpose(0, 2, 1, 3)
    iq_s = jnp.pad(iq_s, ((0, 0), (0, 0), (0, Q_PAD - ds), (0, 0))).reshape(db, IDX_HEADS * Q_PAD, IDX_DIM)
    iw_s = seg(p0, "iw")[srows].reshape(db, ds, IDX_HEADS).transpose(0, 2, 1)
    iw_s = jnp.pad(iw_s, ((0, 0), (0, 0), (0, Q_PAD - ds))).reshape(db, IDX_HEADS * Q_PAD, 1)
    iw_s = jnp.broadcast_to(iw_s, (db, IDX_HEADS * Q_PAD, LANES))
    new_ik_t = _pad_rows(ik_bf[srows].reshape(db, ds, IDX_DIM), page).transpose(0, 2, 1)
    scores = _idx_sample(page_table, iq_s, iw_s, new_ik_t, cache_a_idx_k.transpose(0, 2, 1))
    n_keys = scores.shape[2]
    n_q = _round_up(db * Q_PAD, LANES)
    scores_t = jnp.pad(scores.reshape(db * Q_PAD, n_keys).T, ((0, 0), (0, n_q - db * Q_PAD)))
    bias = _topk_sample(scores_t, topk_s, past_len, ds)[:, :db * Q_PAD].T.reshape(db, Q_PAD, n_keys)
    aq_s = aq[srows].reshape(db, ds, A_KV_HEADS, rep_a, HEAD_DIM).transpose(0, 2, 3, 1, 4)
    aq_s = jnp.pad(aq_s, ((0, 0), (0, 0), (0, 0), (0, Q_PAD - ds), (0, 0))).reshape(db, A_KV_HEADS, rep_a * Q_PAD, HEAD_DIM)
    new_k = _pad_rows(k_bf[srows].reshape(db, ds, a_kv), page)
    new_v = _pad_rows(v_bf[srows].reshape(db, ds, a_kv), page)
    oa_s = _attn_sample(page_table, aq_s, bias, new_k, new_v,
                        cache_a_k.reshape(n_pool, page * A_KV_HEADS, HEAD_DIM),
                        cache_a_v.reshape(n_pool, page * A_KV_HEADS, HEAD_DIM))
    oa_s = oa_s.reshape(db, A_KV_HEADS, rep_a, Q_PAD, HEAD_DIM)[:, :, :, :ds].transpose(0, 3, 1, 2, 4)
    oa_s = oa_s.reshape(n_s, n_heads_a * HEAD_DIM).astype(BF16)

    chunk_s = lambda name: _pad_rows(seg(p0, name)[srows].reshape(db, ds, lay[name][1]), GLA_SUB)
    ob_s, state_s = _gla_sample(chunk_s("bq"), chunk_s("bk"), chunk_s("bv"), chunk_s("br"), chunk_s("misc"),
                                state_b, ds, gate_w, b_head_norm)
    ob_s = ob_s[:, :ds].reshape(n_s, -1)

    mix0 = jnp.concatenate([jnp.concatenate([oa_p, ob_p], axis=1), jnp.concatenate([oa_s, ob_s], axis=1),
                            jnp.zeros((m_s - n_s, d), BF16)], axis=0)
    h, xn = _proj_res_norm(mix0, w_out0.astype(BF16), h, norm_ffn[0])
    w_ff1_bf, w_ff2_bf = w_ff1.astype(BF16), w_ff2.astype(BF16)
    h, xn = _mlp(xn, h, w_ff1_bf, w_ff2_bf, 0, norm_mix[1], BF16)

    c_kv = C_KV_HEADS * HEAD_DIM
    c_q = w_in1.shape[1] - 2 * c_kv
    rep_c = c_q // c_kv
    q1 = _matmul(xn, w_in1[:, :c_q].astype(BF16), BF16)
    kv1 = _matmul(xn, w_in1[:, c_q:].astype(BF16), F32)
    kv1_bf = kv1.astype(BF16)
    oc_p = _sb_prompt(q1, kv1_bf[:, :c_kv], kv1_bf[:, c_kv:], t_pad)
    q1_s = q1[srows].reshape(db, ds, C_KV_HEADS, rep_c, HEAD_DIM).transpose(0, 2, 3, 1, 4)
    q1_s = q1_s.reshape(db, C_KV_HEADS, rep_c * ds, HEAD_DIM)
    new_ck = _pad_rows(kv1_bf[srows, :c_kv].reshape(db, ds, c_kv), page)
    new_cv = _pad_rows(kv1_bf[srows, c_kv:].reshape(db, ds, c_kv), page)
    oc_s = _sb_sample(page_table, q1_s, new_ck, new_cv, cache_c_k.reshape(n_pool, page * C_KV_HEADS, HEAD_DIM),
                      cache_c_v.reshape(n_pool, page * C_KV_HEADS, HEAD_DIM), ds)
    oc_s = oc_s.reshape(db, C_KV_HEADS, rep_c, ds, HEAD_DIM).transpose(0, 3, 1, 2, 4).reshape(n_s, c_q).astype(BF16)
    mix1 = jnp.concatenate([oc_p, oc_s, jnp.zeros((m_s - n_s, c_q), BF16)], axis=0)
    h, xn = _proj_res_norm(mix1, w_out1.astype(BF16), h, norm_ffn[1])
    h, y = _mlp(xn, h, w_ff1_bf, w_ff2_bf, 1, norm_final, F32)

    y_prompt = y[N_META:t_p][None]
    y_sample = y[srows].reshape(db, ds, d)
    kv_p = lambda a, heads: a[:t_p].reshape(1, t_p, heads, HEAD_DIM)
    kv_s = lambda a, heads: a[srows].reshape(db, ds, heads, HEAD_DIM)
    return (y_prompt, y_sample,
            kv_p(ak, A_KV_HEADS), kv_p(seg(p0, "av"), A_KV_HEADS), ik[:t_p, :IDX_DIM][None], state_p[None],
            kv_p(kv1[:, :c_kv], C_KV_HEADS), kv_p(kv1[:, c_kv:], C_KV_HEADS),
            kv_s(ak, A_KV_HEADS), kv_s(seg(p0, "av"), A_KV_HEADS), ik[srows, :IDX_DIM].reshape(db, ds, IDX_DIM),
            state_s, kv_s(kv1[:, :c_kv], C_KV_HEADS), kv_s(kv1[:, c_kv:], C_KV_HEADS))
```

```python
import functools
import math

import jax
import jax.numpy as jnp
from jax import lax
from jax.experimental import pallas as pl
from jax.experimental.pallas import tpu as pltpu

F32 = jnp.float32
BF16 = jnp.bfloat16
I32 = jnp.int32

N_META = 16
HEAD_DIM = 128
ROPE_THETA = 10000.0
EPS = 1e-6
A_KV_HEADS = 2
IDX_HEADS = 16
IDX_DIM = 64
TOPK_MAX = 256
B_HEADS = 4
B_GATE_RANK = 16
B_GATE_TAU = 16.0
C_KV_HEADS = 4

LANES = 128
SUBLANES = 8
VMEM_LIMIT_BYTES = 56 * 1024 * 1024

ROW_BLOCK = 128
INT_MIN = -(2 ** 31)


def _round_up(x, m):
    return (x + m - 1) // m * m


def _row_tile(m, cap):
    best = ROW_BLOCK
    for k in range(1, m // ROW_BLOCK + 1):
        t = k * ROW_BLOCK
        if t <= cap and m % t == 0:
            best = t
    return best


def _col_tile(n, cap):
    best = LANES
    for k in range(1, n // LANES + 1):
        t = k * LANES
        if t <= cap and n % t == 0:
            best = t
    return best


def _params(*sem):
    return pltpu.CompilerParams(dimension_semantics=sem, vmem_limit_bytes=VMEM_LIMIT_BYTES)


def _rms(x, g):
    ms = jnp.mean(x * x, axis=-1, keepdims=True)
    return x * lax.rsqrt(ms + EPS) * g


def _rmsnorm_kernel(x_ref, g_ref, o_ref):
    o_ref[...] = _rms(x_ref[...], g_ref[...]).astype(o_ref.dtype)


def _rmsnorm(x, g, out_dtype):
    m, d = x.shape
    tm = _row_tile(m, 512)
    return pl.pallas_call(
        _rmsnorm_kernel,
        out_shape=jax.ShapeDtypeStruct((m, d), out_dtype),
        grid=(m // tm,),
        in_specs=[pl.BlockSpec((tm, d), lambda i: (i, 0)), pl.BlockSpec((1, d), lambda i: (0, 0))],
        out_specs=pl.BlockSpec((tm, d), lambda i: (i, 0)),
        compiler_params=_params("parallel"),
        name="rmsnorm",
    )(x, g.reshape(1, d))


def _matmul_kernel(x_ref, w_ref, o_ref, *, scale):
    acc = jnp.dot(x_ref[...], w_ref[...], preferred_element_type=F32)
    o_ref[...] = (acc if scale is None else acc * scale).astype(o_ref.dtype)


def _matmul(x, w, out_dtype, scale=None):
    m, k = x.shape
    n = w.shape[1]
    tm = _row_tile(m, 1408)
    tn = _col_tile(n, 1152)
    return pl.pallas_call(
        functools.partial(_matmul_kernel, scale=scale),
        out_shape=jax.ShapeDtypeStruct((m, n), out_dtype),
        grid=(m // tm, n // tn),
        in_specs=[pl.BlockSpec((tm, k), lambda i, j: (i, 0)), pl.BlockSpec((k, tn), lambda i, j: (0, j))],
        out_specs=pl.BlockSpec((tm, tn), lambda i, j: (i, j)),
        compiler_params=_params("parallel", "parallel"),
        name="matmul",
    )(x, w)


def _proj_res_norm_kernel(*refs, n_in):
    x_refs, w_refs = refs[:n_in], refs[n_in:2 * n_in]
    h_ref, g_ref, hout_ref, xn_ref = refs[2 * n_in:]
    hn = h_ref[...]
    for x_ref, w_ref in zip(x_refs, w_refs):
        hn = hn + jnp.dot(x_ref[...], w_ref[...], preferred_element_type=F32)
    hout_ref[...] = hn
    xn_ref[...] = _rms(hn, g_ref[...]).astype(xn_ref.dtype)


def _proj_res_norm(xs, w, h, g):
    m, k = xs[0].shape
    assert all(x.shape == (m, k) for x in xs) and w.shape[0] == k * len(xs)
    d = w.shape[1]
    tm = _row_tile(m, 384)
    row = lambda i: (i, 0)
    in_specs = [pl.BlockSpec((tm, k), row) for _ in xs]
    in_specs += [pl.BlockSpec((k, d), lambda i, j=j: (j, 0)) for j in range(len(xs))]
    in_specs += [pl.BlockSpec((tm, d), row), pl.BlockSpec((1, d), lambda i: (0, 0))]
    return pl.pallas_call(
        functools.partial(_proj_res_norm_kernel, n_in=len(xs)),
        out_shape=(jax.ShapeDtypeStruct((m, d), F32), jax.ShapeDtypeStruct((m, d), BF16)),
        grid=(m // tm,),
        in_specs=in_specs,
        out_specs=(pl.BlockSpec((tm, d), row), pl.BlockSpec((tm, d), row)),
        compiler_params=_params("parallel"),
        name="proj_res_norm",
    )(*xs, *([w] * len(xs)), h, g.reshape(1, d))


def _mlp_kernel(xn_ref, h_ref, w1_ref, w2_ref, g_ref, hout_ref, xnout_ref):
    f = pl.program_id(1)

    @pl.when(f == 0)
    def _():
        hout_ref[...] = h_ref[...]

    a = jnp.maximum(jnp.dot(xn_ref[...], w1_ref[...], preferred_element_type=F32), 0.0)
    hout_ref[...] += jnp.dot((a * a).astype(BF16), w2_ref[...], preferred_element_type=F32)

    @pl.when(f == pl.num_programs(1) - 1)
    def _():
        xnout_ref[...] = _rms(hout_ref[...], g_ref[...]).astype(xnout_ref.dtype)


def _mlp(xn, h, w1, w2, layer, g, norm_dtype):
    m, d = xn.shape
    dff = w1.shape[2]
    tm = _row_tile(m, 768)
    tf = _col_tile(dff, 512)
    return pl.pallas_call(
        _mlp_kernel,
        out_shape=(jax.ShapeDtypeStruct((m, d), F32), jax.ShapeDtypeStruct((m, d), norm_dtype)),
        grid=(m // tm, dff // tf),
        in_specs=[pl.BlockSpec((tm, d), lambda i, f: (i, 0)),
                  pl.BlockSpec((tm, d), lambda i, f: (i, 0), pipeline_mode=pl.Buffered(1)),
                  pl.BlockSpec((None, d, tf), lambda i, f: (layer, 0, f)),
                  pl.BlockSpec((None, tf, d), lambda i, f: (layer, f, 0)),
                  pl.BlockSpec((1, d), lambda i, f: (0, 0))],
        out_specs=(pl.BlockSpec((tm, d), lambda i, f: (i, 0)), pl.BlockSpec((tm, d), lambda i, f: (i, 0))),
        compiler_params=_params("parallel", "arbitrary"),
        name="mlp",
    )(xn, h, w1, w2, g.reshape(1, d))


def _even_layout(d_model):
    a_q = d_model // 2
    a_kv = A_KV_HEADS * HEAD_DIM
    idx_q = IDX_HEADS * IDX_DIM
    b_v = d_model // 2
    b_qk = b_v // 2
    ref_sizes = dict(aq=a_q, ak=a_kv, av=a_kv, iq=idx_q, ik=IDX_DIM, iw=IDX_HEADS, bq=b_qk, bk=b_qk, bv=b_v,
                     bglr=B_GATE_RANK, br=b_v)
    ref_off, off = {}, 0
    for name, size in ref_sizes.items():
        ref_off[name] = off
        off += size
    lay, off = {}, 0
    for name in ("aq", "iq", "bq", "bk", "bv", "br", "ak", "av", "ik", "iw", "bglr"):
        lay[name] = (off, ref_sizes[name])
        off += ref_sizes[name]
    lay["misc"] = (lay["ik"][0], LANES)
    total = lay["misc"][0] + LANES
    assert off <= total
    return lay, ref_off, total


def _permute_w_in0(w_in0, lay, ref_off, total):
    parts, end = [], 0
    for name, (off, size) in lay.items():
        if name != "misc":
            assert off == end
            parts.append(w_in0[:, ref_off[name]:ref_off[name] + size])
            end = off + size
    parts.append(jnp.zeros((w_in0.shape[0], total - end), w_in0.dtype))
    return jnp.concatenate(parts, axis=1)


def _rope_tables(pos):
    def tab(d):
        half = d // 2
        inv = ROPE_THETA ** (-jnp.arange(half, dtype=F32) * (2.0 / d))
        ang = pos.astype(F32)[:, None] * inv[None, :]
        c, s = jnp.cos(ang), jnp.sin(ang)
        reps = LANES // d
        return jnp.tile(jnp.concatenate([c, c], 1), (1, reps)), jnp.tile(jnp.concatenate([-s, s], 1), (1, reps))
    c128, s128 = tab(HEAD_DIM)
    c64, s64 = tab(IDX_DIM)
    return jnp.concatenate([c128, s128, c64, s64], axis=1)


def _rope_kernel(q_ref, kv_ref, tab_ref, aq_ref, iq_ref, misc_ref, ak_ref, ik_ref, kbf_ref, vbf_ref, ikbf_ref,
                 *, n_aq, n_iq, n_ak):
    cos128 = tab_ref[:, 0:128]
    sin128 = tab_ref[:, 128:256]
    cos64 = tab_ref[:, 256:384]
    sin64 = tab_ref[:, 384:512]
    lane = lax.broadcasted_iota(I32, cos64.shape, 1)
    low_half = (lane % IDX_DIM) < (IDX_DIM // 2)

    def rope128(x):
        return x * cos128 + pltpu.roll(x, HEAD_DIM // 2, 1) * sin128

    def rope64(x):
        rot = jnp.where(low_half, pltpu.roll(x, LANES - IDX_DIM // 2, 1), pltpu.roll(x, IDX_DIM // 2, 1))
        return x * cos64 + rot * sin64

    n_blk = q_ref.shape[0] // ROW_BLOCK

    def put_transposed(ref, c, x):
        for s in range(n_blk):
            tile = jnp.transpose(x[s * ROW_BLOCK:(s + 1) * ROW_BLOCK, :])
            ref[s, c * LANES:(c + 1) * LANES, :] = tile.astype(ref.dtype)

    for c in range(n_aq):
        sl = slice(c * LANES, (c + 1) * LANES)
        put_transposed(aq_ref, c, rope128(q_ref[:, sl]) * EXP2_SCALE)
    for c in range(n_iq):
        put_transposed(iq_ref, c, rope64(q_ref[:, n_aq * LANES + c * LANES:n_aq * LANES + (c + 1) * LANES]))
    put_transposed(misc_ref, 0, kv_ref[:, 2 * n_ak * LANES:(2 * n_ak + 1) * LANES])
    for c in range(n_ak):
        sl = slice(c * LANES, (c + 1) * LANES)
        kr = rope128(kv_ref[:, sl])
        ak_ref[:, sl] = kr
        kbf_ref[:, sl] = kr.astype(BF16)
    vbf_ref[...] = kv_ref[:, n_ak * LANES:2 * n_ak * LANES].astype(BF16)
    ik = rope64(kv_ref[:, 2 * n_ak * LANES:(2 * n_ak + 1) * LANES])
    ik_ref[...] = ik
    ikbf_ref[...] = ik[:, :IDX_DIM].astype(BF16)


def _rope(p0, tab, lay):
    m = p0.shape[0]
    tr = _row_tile(m, 384)
    a_q, a_kv, idx_q = lay["aq"][1], lay["ak"][1], lay["iq"][1]
    qw = a_q + idx_q
    kvw = 2 * a_kv + LANES
    assert lay["aq"][0] == 0 and lay["iq"][0] == a_q and lay["ak"][0] % kvw == 0
    kern = functools.partial(_rope_kernel, n_aq=a_q // LANES, n_iq=idx_q // LANES, n_ak=a_kv // LANES)
    row = lambda i: (i, 0)
    blk = lambda i: (i, 0, 0)
    nb, tb = m // ROW_BLOCK, tr // ROW_BLOCK
    return pl.pallas_call(
        kern,
        out_shape=(jax.ShapeDtypeStruct((nb, a_q, ROW_BLOCK), BF16), jax.ShapeDtypeStruct((nb, idx_q, ROW_BLOCK), BF16),
                   jax.ShapeDtypeStruct((nb, LANES, ROW_BLOCK), F32),
                   jax.ShapeDtypeStruct((m, a_kv), F32), jax.ShapeDtypeStruct((m, LANES), F32),
                   jax.ShapeDtypeStruct((m, a_kv), BF16), jax.ShapeDtypeStruct((m, a_kv), BF16),
                   jax.ShapeDtypeStruct((m, IDX_DIM), BF16)),
        grid=(m // tr,),
        in_specs=[pl.BlockSpec((tr, qw), row),
                  pl.BlockSpec((tr, kvw), lambda i: (i, lay["ak"][0] // kvw)),
                  pl.BlockSpec((tr, 4 * LANES), row)],
        out_specs=(pl.BlockSpec((tb, a_q, ROW_BLOCK), blk), pl.BlockSpec((tb, idx_q, ROW_BLOCK), blk),
                   pl.BlockSpec((tb, LANES, ROW_BLOCK), blk), pl.BlockSpec((tr, a_kv), row),
                   pl.BlockSpec((tr, LANES), row), pl.BlockSpec((tr, a_kv), row), pl.BlockSpec((tr, a_kv), row),
                   pl.BlockSpec((tr, IDX_DIM), row)),
        compiler_params=_params("parallel"),
        name="rope",
    )(p0, p0, tab)


GLA_SUB = 16


def _split_bf16(x):
    hi = x.astype(BF16)
    return hi, (x - hi.astype(F32)).astype(BF16)


def _log_sigmoid(x):
    return jnp.minimum(x, 0.0) - jnp.log(1.0 + jnp.exp(-jnp.abs(x)))


def _dot(a, b):
    return jnp.dot(a, b, preferred_element_type=F32)


def _dot_nt(a, b):
    return lax.dot_general(a, b, (((1,), (1,)), ((), ())), preferred_element_type=F32)


def _dot_tn(a, b):
    return lax.dot_general(a, b, (((0,), (0,)), ((), ())), preferred_element_type=F32)


def _gla_gate(misc, w2_hi, w2_lo, bias, valid):
    m_hi, m_lo = _split_bf16(misc)
    x = _dot(m_hi, w2_hi) + _dot(m_lo, w2_hi) + _dot(m_hi, w2_lo) + bias
    return jnp.where(valid, _log_sigmoid(x) * (1.0 / B_GATE_TAU), 0.0)


def _gla_chunk(q, k, v, g, state_ref, dk, dv):
    c = q.shape[0]
    n_sub = c // GLA_SUB
    ri = lax.broadcasted_iota(I32, (c, c), 0)
    ci = lax.broadcasted_iota(I32, (c, c), 1)
    causal = ri >= ci
    tri = jnp.where(causal, 1.0, 0.0).astype(BF16)
    gcum = _dot(jnp.concatenate([tri, tri], axis=1), jnp.concatenate(_split_bf16(g), axis=0))
    scale = dk ** -0.5
    pad = _round_up(c, LANES)
    outs = []
    for h in range(B_HEADS):
        gh = gcum[:, h * dk:(h + 1) * dk]
        qh = q[:, h * dk:(h + 1) * dk] * scale
        kh = k[:, h * dk:(h + 1) * dk]
        vh = v[:, h * dv:(h + 1) * dv].astype(BF16)
        q_parts, k_parts = [], []
        for i in range(n_sub):
            lo, hi = i * GLA_SUB, (i + 1) * GLA_SUB
            base = gh[lo - 1:lo, :] if i else jnp.zeros((1, dk), F32)
            q_parts.append(qh[lo:hi] * jnp.exp(gh[lo:hi] - base))
            k_parts.append(kh[:hi] * jnp.exp(base - gh[:hi]))
            if pad > hi:
                k_parts.append(jnp.zeros((pad - hi, dk), F32))
        qt = jnp.concatenate(q_parts, axis=0).astype(BF16)
        kt = jnp.concatenate(k_parts, axis=0).astype(BF16)
        r = _dot_nt(qt, kt)
        att = jnp.concatenate([r[i * GLA_SUB:(i + 1) * GLA_SUB, i * pad:i * pad + c] for i in range(n_sub)], axis=0)
        att = jnp.where(causal, att, 0.0).astype(BF16)
        s = state_ref[h]
        o = _dot(att, vh) + _dot((qh * jnp.exp(gh)).astype(BF16), s.astype(BF16))
        last = gh[c - 1:c, :]
        k_tail = (kh * jnp.exp(last - gh)).astype(BF16)
        decay = jnp.transpose(jnp.broadcast_to(jnp.exp(last), (dk, dk)))
        decay = jnp.concatenate([decay] * (dv // dk), axis=1)
        state_ref[h] = decay * s + _dot_tn(k_tail, vh)
        outs.append(o)
    return outs


def _gla_out(outs, r, head_norm, dv):
    res = []
    for h, o in enumerate(outs):
        gate = r[:, h * dv:(h + 1) * dv]
        gate = gate * (1.0 / (1.0 + jnp.exp(-gate)))
        res.append(_rms(o, head_norm) * gate)
    return jnp.concatenate(res, axis=1)


def _gla_prompt_kernel(q_ref, k_ref, v_ref, r_ref, misc_ref, w2h_ref, w2l_ref, b_ref, hn_ref, o_ref, state_ref,
                       *, chunk, n_valid, dk, dv):
    blk = pl.program_id(0)

    @pl.when(blk == 0)
    def _():
        state_ref[...] = jnp.zeros_like(state_ref)

    rows = q_ref.shape[0]
    for c0 in range(0, rows, chunk):
        sl = slice(c0, c0 + chunk)
        row = blk * rows + c0 + lax.broadcasted_iota(I32, (chunk, 1), 0)
        g = _gla_gate(misc_ref[sl, :], w2h_ref[...], w2l_ref[...], b_ref[...], row < n_valid)
        outs = _gla_chunk(q_ref[sl, :], k_ref[sl, :], v_ref[sl, :], g, state_ref, dk, dv)
        o_ref[sl, :] = _gla_out(outs, r_ref[sl, :], hn_ref[...], dv).astype(o_ref.dtype)


def _gla_sample_kernel(q_ref, k_ref, v_ref, r_ref, misc_ref, w2h_ref, w2l_ref, b_ref, hn_ref, s_ref, o_ref,
                       state_ref, *, n_valid, dk, dv):
    state_ref[0] = s_ref[0]
    rows = q_ref.shape[1]
    row = lax.broadcasted_iota(I32, (rows, 1), 0)
    g = _gla_gate(misc_ref[0], w2h_ref[...], w2l_ref[...], b_ref[...], row < n_valid)
    outs = _gla_chunk(q_ref[0], k_ref[0], v_ref[0], g, state_ref.at[0], dk, dv)
    o_ref[0] = _gla_out(outs, r_ref[0], hn_ref[...], dv).astype(o_ref.dtype)


def _gate_weights(w_b_gate2, b_b_gate, lay):
    off = lay["bglr"][0] - lay["misc"][0]
    w2 = jnp.zeros((LANES, w_b_gate2.shape[1]), F32).at[off:off + B_GATE_RANK].set(w_b_gate2)
    hi = w2.astype(BF16)
    lo = (w2 - hi.astype(F32)).astype(BF16)
    return hi, lo, b_b_gate.reshape(1, -1)


def _gla_prompt(p0, lay, n_rows, n_valid, gate_w, head_norm, chunk=64):
    bqk, bv = lay["bq"][1], lay["bv"][1]
    dk, dv = bqk // B_HEADS, bv // B_HEADS
    w2h, w2l, bias = gate_w
    col = lambda name: (lambda i, o=lay[name][0] // lay[name][1]: (i, o))
    const = lambda i: (0, 0)
    kern = functools.partial(_gla_prompt_kernel, chunk=chunk, n_valid=n_valid, dk=dk, dv=dv)
    return pl.pallas_call(
        kern,
        out_shape=(jax.ShapeDtypeStruct((p0.shape[0], bv), BF16), jax.ShapeDtypeStruct((B_HEADS, dk, dv), F32)),
        grid=(n_rows // ROW_BLOCK,),
        in_specs=[pl.BlockSpec((ROW_BLOCK, bqk), col("bq")), pl.BlockSpec((ROW_BLOCK, bqk), col("bk")),
                  pl.BlockSpec((ROW_BLOCK, bv), col("bv")), pl.BlockSpec((ROW_BLOCK, bv), col("br")),
                  pl.BlockSpec((ROW_BLOCK, LANES), col("misc")),
                  pl.BlockSpec(w2h.shape, const), pl.BlockSpec(w2l.shape, const), pl.BlockSpec(bias.shape, const),
                  pl.BlockSpec((1, dv), const)],
        out_specs=(pl.BlockSpec((ROW_BLOCK, bv), lambda i: (i, 0)),
                   pl.BlockSpec((B_HEADS, dk, dv), lambda i: (0, 0, 0))),
        compiler_params=_params("arbitrary"),
        name="gla_prompt",
    )(p0, p0, p0, p0, p0, w2h, w2l, bias, head_norm.reshape(1, dv))


def _gla_sample(q, k, v, r, misc, state, n_valid, gate_w, head_norm):
    db, rows, bqk = q.shape
    bv = v.shape[2]
    dk, dv = bqk // B_HEADS, bv // B_HEADS
    w2h, w2l, bias = gate_w
    seq = lambda i: (i, 0, 0)
    const = lambda i: (0, 0)
    kern = functools.partial(_gla_sample_kernel, n_valid=n_valid, dk=dk, dv=dv)
    return pl.pallas_call(
        kern,
        out_shape=(jax.ShapeDtypeStruct((db, rows, bv), BF16), jax.ShapeDtypeStruct(state.shape, F32)),
        grid=(db,),
        in_specs=[pl.BlockSpec((1, rows, bqk), seq), pl.BlockSpec((1, rows, bqk), seq),
                  pl.BlockSpec((1, rows, bv), seq), pl.BlockSpec((1, rows, bv), seq),
                  pl.BlockSpec((1, rows, LANES), seq),
                  pl.BlockSpec(w2h.shape, const), pl.BlockSpec(w2l.shape, const), pl.BlockSpec(bias.shape, const),
                  pl.BlockSpec((1, dv), const),
                  pl.BlockSpec((1, B_HEADS, dk, dv), lambda i: (i, 0, 0, 0))],
        out_specs=(pl.BlockSpec((1, rows, bv), seq), pl.BlockSpec((1, B_HEADS, dk, dv), lambda i: (i, 0, 0, 0))),
        compiler_params=_params("parallel"),
        name="gla_sample",
    )(q, k, v, r, misc, w2h, w2l, bias, head_norm.reshape(1, dv), state)


SB_DONE = -120.0
SB_Q_SCALE = HEAD_DIM ** -0.5 * math.log2(math.e)
SB_DONE2 = SB_DONE * math.log2(math.e)


def _sb_suffix_matrix():
    j = lax.broadcasted_iota(I32, (2 * LANES, 2 * LANES), 0) % LANES
    s = lax.broadcasted_iota(I32, (2 * LANES, 2 * LANES), 1)
    return jnp.where((j > s) | (s >= LANES), 1.0, 0.0).astype(BF16)


def _sb_blocks(qs, ks, vs, strict, carries, accs, suffix):
    zs = [_dot_nt(q, k) for q, k in zip(qs, ks)]
    keeps, takes = [], []
    for z in zs:
        soft = jnp.log2(1.0 + jnp.exp2(-jnp.abs(z)))
        log_keep = -(jnp.maximum(z, 0.0) + soft)
        keeps.append(log_keep if strict is None else jnp.where(strict, log_keep, 0.0))
        takes.append(jnp.minimum(z, 0.0) - soft)
    sums = []
    for lk in keeps:
        sums.append(_dot(jnp.concatenate(_split_bf16(lk), axis=1), suffix))
    ws = []
    for take, s, c in zip(takes, sums, carries):
        w = jnp.exp2(take + (s[:, :LANES] + c))
        ws.append((w if strict is None else jnp.where(strict, w, 0.0)).astype(BF16))
    new_carries = [c + s[:, LANES:] for c, s in zip(carries, sums)]
    new_accs = [a + _dot(w, v) for a, w, v in zip(accs, ws, vs)]
    return new_carries, new_accs


def _sb_prompt_kernel(q_ref, k_ref, v_ref, o_ref, *scratch, n_kv, rep):
    carry_refs, acc_refs = scratch[:n_kv], scratch[n_kv:]
    b = pl.program_id(0)
    suffix = _sb_suffix_matrix()
    for ref in scratch:
        ref[...] = jnp.zeros_like(ref)
    rows = rep * ROW_BLOCK
    q_row = lax.broadcasted_iota(I32, (rows, LANES), 0) % ROW_BLOCK
    col = lax.broadcasted_iota(I32, (rows, LANES), 1)
    head = lambda g: slice(g * HEAD_DIM, (g + 1) * HEAD_DIM)

    def visit(kb, strict):
        k0 = pl.multiple_of(kb * ROW_BLOCK, ROW_BLOCK)
        qs = [jnp.concatenate([q_ref[:, head(g * rep + r)] for r in range(rep)], axis=0) for g in range(n_kv)]
        ks = [k_ref[pl.ds(k0, ROW_BLOCK), head(g)] for g in range(n_kv)]
        vs = [v_ref[pl.ds(k0, ROW_BLOCK), head(g)] for g in range(n_kv)]
        carries, accs = _sb_blocks(qs, ks, vs, strict, [r[...] for r in carry_refs], [r[...] for r in acc_refs],
                                   suffix)
        top = None
        for g in range(n_kv):
            carry_refs[g][...] = carries[g]
            acc_refs[g][...] = accs[g]
            part = _col_reduce(carries[g], jnp.max)
            top = part if top is None else jnp.maximum(top, part)
        return (jnp.max(top) < SB_DONE2).astype(I32)

    done = visit(b, col < q_row)
    lax.while_loop(lambda st: (st[0] >= 0) & (st[1] == 0), lambda st: (st[0] - 1, visit(st[0], None)), (b - 1, done))
    for g in range(n_kv):
        for r in range(rep):
            o_ref[:, head(g * rep + r)] = acc_refs[g][r * ROW_BLOCK:(r + 1) * ROW_BLOCK, :].astype(o_ref.dtype)


def _sb_prompt(q, k, v, n_rows):
    cq = q.shape[1]
    ckv = k.shape[1]
    n_kv = ckv // HEAD_DIM
    rep = cq // ckv
    kern = functools.partial(_sb_prompt_kernel, n_kv=n_kv, rep=rep)
    return pl.pallas_call(
        kern,
        out_shape=jax.ShapeDtypeStruct(q.shape, BF16),
        grid=(n_rows // ROW_BLOCK,),
        in_specs=[pl.BlockSpec((ROW_BLOCK, cq), lambda i: (i, 0)),
                  pl.BlockSpec((n_rows, ckv), lambda i: (0, 0)),
                  pl.BlockSpec((n_rows, ckv), lambda i: (0, 0))],
        out_specs=pl.BlockSpec((ROW_BLOCK, cq), lambda i: (i, 0)),
        scratch_shapes=[pltpu.VMEM((rep * ROW_BLOCK, LANES), F32)] * n_kv
                       + [pltpu.VMEM((rep * ROW_BLOCK, HEAD_DIM), F32)] * n_kv,
        compiler_params=_params("parallel"),
        name="stick_prompt",
    )(q, k, v)


def _page_head(ref, g, n_heads):
    keys = ref.shape[1] // n_heads
    return ref[0, pl.ds(g, keys, stride=n_heads), :].astype(BF16)


def _sb_sample_kernel(pt_ref, q_ref, *refs, n_kv, n_new, dec_seq, n_pages_step, has_init):
    del pt_ref
    refs = list(refs)
    new_refs = [refs.pop(0), refs.pop(0)] if n_new else None
    k_pages = [refs.pop(0) for _ in range(n_pages_step)]
    v_pages = [refs.pop(0) for _ in range(n_pages_step)]
    init_refs = [refs.pop(0), refs.pop(0)] if has_init else None
    o_ref, carry_ref = refs
    suffix = _sb_suffix_matrix()
    rows = q_ref.shape[2]

    @pl.when(pl.program_id(1) == 0)
    def _():
        if has_init:
            o_ref[...] = init_refs[0][...]
            carry_ref[...] = init_refs[1][...]
        else:
            o_ref[...] = jnp.zeros_like(o_ref)
            carry_ref[...] = jnp.zeros_like(carry_ref)

    def visit(k_of, v_of, strict):
        @pl.when(jnp.max(carry_ref[...]) >= SB_DONE2)
        def _():
            heads = range(n_kv)
            carries, accs = _sb_blocks([q_ref[0, g] for g in heads], [k_of(g) for g in heads],
                                       [v_of(g) for g in heads], strict, [carry_ref[0, g] for g in heads],
                                       [o_ref[0, g] for g in heads], suffix)
            for g in heads:
                carry_ref[0, g] = carries[g]
                o_ref[0, g] = accs[g]

    cols = lambda ref, g: ref[0, :, g * HEAD_DIM:(g + 1) * HEAD_DIM].astype(BF16)
    if n_new:
        q_idx = lax.broadcasted_iota(I32, (rows, LANES), 0) % dec_seq
        col = lax.broadcasted_iota(I32, (rows, LANES), 1)
        visit(lambda g: cols(new_refs[0], g), lambda g: cols(new_refs[1], g), col < q_idx)
    for kp, vp in zip(k_pages, v_pages):
        visit(lambda g, kp=kp: _page_head(kp, g, n_kv), lambda g, vp=vp: _page_head(vp, g, n_kv), None)


def _sb_sample_call(page_table, q, new_kv, cache_k, cache_v, init, dec_seq, first_page, n_steps, pages_per_step):
    db, n_kv, rows, _ = q.shape
    page_rows = cache_k.shape[1]
    assert page_rows == LANES * n_kv
    seq4 = lambda b, c, pt: (b, 0, 0, 0)
    in_specs = [pl.BlockSpec((1, n_kv, rows, HEAD_DIM), seq4)]
    args = [q]
    if new_kv is not None:
        in_specs += [pl.BlockSpec((1,) + new_kv[0].shape[1:], lambda b, c, pt: (b, 0, 0))] * 2
        args += list(new_kv)
    for cache in (cache_k, cache_v):
        for i in range(pages_per_step):
            in_specs.append(pl.BlockSpec(
                (1, page_rows, HEAD_DIM), lambda b, c, pt, i=i: (pt[b, first_page - c * pages_per_step - i], 0, 0)))
            args.append(cache)
    if init is not None:
        in_specs += [pl.BlockSpec((1, n_kv, rows, HEAD_DIM), seq4)] * 2
        args += list(init)
    kern = functools.partial(_sb_sample_kernel, n_kv=n_kv, n_new=new_kv is not None, dec_seq=dec_seq,
                             n_pages_step=pages_per_step, has_init=init is not None)
    out_sds = jax.ShapeDtypeStruct((db, n_kv, rows, HEAD_DIM), F32)
    return pl.pallas_call(
        kern,
        out_shape=(out_sds, out_sds),
        grid_spec=pltpu.PrefetchScalarGridSpec(
            num_scalar_prefetch=1,
            grid=(db, n_steps),
            in_specs=in_specs,
            out_specs=(pl.BlockSpec((1, n_kv, rows, HEAD_DIM), seq4), pl.BlockSpec((1, n_kv, rows, HEAD_DIM), seq4)),
        ),
        compiler_params=_params("parallel", "arbitrary"),
        name="stick_sample",
    )(page_table, *args)


def _sortable(score):
    bits = lax.bitcast_convert_type(score + 0.0, I32)
    key = bits ^ (lax.shift_right_arithmetic(bits, 31) & 0x7FFFFFFF)
    return jnp.maximum(key, INT_MIN + 1)


COUNT_ROWS = 64


def _count(key_ref, n_units, unit, pred):
    rows = COUNT_ROWS if unit % COUNT_ROWS == 0 else SUBLANES

    def body(u, acc):
        r0 = pl.multiple_of(u * unit, unit)
        hit = pred(key_ref[pl.ds(r0, unit), :], r0)
        ones = jnp.where(hit, jnp.ones(hit.shape, I32), jnp.zeros(hit.shape, I32))
        return acc + jnp.sum(ones.reshape(unit // rows, rows, LANES), axis=0)
    acc = lax.fori_loop(0, n_units, body, jnp.zeros((rows, LANES), I32))
    return jnp.sum(acc, axis=0, keepdims=True)


def _select_topk(key_ref, n_units, unit, k, n_rows):
    def bit_step(i, tau):
        cand = tau ^ lax.shift_left(jnp.int32(1), 31 - i)
        cnt = _count(key_ref, n_units, unit, lambda keys, r0: keys >= cand)
        return jnp.where(cnt >= k, cand, tau)
    tau = lax.fori_loop(0, 32, bit_step, jnp.full((1, LANES), INT_MIN, I32))
    c_gt = _count(key_ref, n_units, unit, lambda keys, r0: keys > tau)
    c_ge = _count(key_ref, n_units, unit, lambda keys, r0: keys >= tau)
    real = tau > INT_MIN
    tie = real & (c_ge > k)
    need = k - c_gt
    row = lax.broadcasted_iota(I32, (unit, LANES), 0)
    n_bits = max(1, (n_rows - 1).bit_length())

    def break_ties():
        def pos_step(i, x):
            cand = x | lax.shift_left(jnp.int32(1), n_bits - 1 - i)
            cnt = _count(key_ref, n_units, unit, lambda keys, r0: (keys == tau) & ((r0 + row) < cand))
            return jnp.where(cnt < need, cand, x)
        return lax.fori_loop(0, n_bits, pos_step, jnp.zeros((1, LANES), I32))

    last_tie = lax.cond(jnp.max(tie.astype(I32)) > 0, break_ties, lambda: jnp.zeros((1, LANES), I32))
    last = jnp.where(tie, last_tie, jnp.where(real, jnp.int32(2 ** 30), jnp.int32(-1)))
    return tau, last


def _col_reduce(x, fn):
    rows = x.shape[0]
    if rows > COUNT_ROWS and rows % COUNT_ROWS == 0:
        x = fn(x.reshape(rows // COUNT_ROWS, COUNT_ROWS, x.shape[1]), axis=0)
    return fn(x, axis=0, keepdims=True)


def _selected(keys, r0, tau, last):
    row = lax.broadcasted_iota(I32, keys.shape, 0)
    return (keys > tau) | ((keys == tau) & ((r0 + row) <= last))


IDX_SCALE = (IDX_HEADS * IDX_DIM) ** -0.5
MASKED = -1e30
EXP2_SCALE = HEAD_DIM ** -0.5 * math.log2(math.e)


def _dsa_prompt_kernel(aq_ref, iq_ref, misc_ref, ik_ref, k_ref, v_ref, o_ref, key_ref, *scratch,
                       unit, topk, n_valid, n_heads, n_kv, iw_row):
    b = pl.program_id(0)
    n_units = ((b + 1) * ROW_BLOCK + unit - 1) // unit
    rep = n_heads // n_kv
    row = lax.broadcasted_iota(I32, (unit, LANES), 0)
    q_pos = b * ROW_BLOCK + lax.broadcasted_iota(I32, (unit, LANES), 1)
    q_real = q_pos < n_valid

    iq_t = iq_ref[0]
    rhs = jnp.concatenate([iq_t[h * IDX_DIM:(h + 1) * IDX_DIM, :] for h in range(IDX_HEADS)], axis=1)
    w_t = misc_ref[0, iw_row:iw_row + IDX_HEADS, :]

    def score_unit(u, carry):
        r0 = pl.multiple_of(u * unit, unit)
        dots = _dot(ik_ref[pl.ds(r0, unit), :], rhs)
        score = jnp.zeros((unit, LANES), F32)
        for h in range(IDX_HEADS):
            score = score + jnp.maximum(dots[:, h * LANES:(h + 1) * LANES], 0.0) * w_t[h:h + 1, :]
        k_pos = r0 + row
        adm = (k_pos <= q_pos) & (q_real | (k_pos == 0))
        key_ref[pl.ds(r0, unit), :] = jnp.where(adm, _sortable(score * IDX_SCALE), INT_MIN)
        return carry

    lax.fori_loop(0, n_units, score_unit, 0)
    tau, last = _select_topk(key_ref, n_units, unit, topk, key_ref.shape[0])

    m_refs, l_refs, acc_refs, lg_refs = (scratch[i * n_kv:(i + 1) * n_kv] for i in range(4))
    for g in range(n_kv):
        m_refs[g][...] = jnp.full_like(m_refs[g], MASKED)
        l_refs[g][...] = jnp.zeros_like(l_refs[g])
        acc_refs[g][...] = jnp.zeros_like(acc_refs[g])
    aq_t = aq_ref[0]

    qgs = [jnp.concatenate([aq_t[(g * rep + r) * HEAD_DIM:(g * rep + r + 1) * HEAD_DIM, :] for r in range(rep)],
                           axis=1) for g in range(n_kv)]
    n_sub = unit // ROW_BLOCK
    fold = lambda x, fn: fn(x.reshape(x.shape[0] // SUBLANES, SUBLANES, x.shape[1]), axis=0)

    def logits_of_unit(u, slot):
        r0 = pl.multiple_of(u * unit, unit)
        sel = _selected(key_ref[pl.ds(r0, unit), :], r0, tau, last)
        tops = [None] * n_kv
        for j in range(n_sub):
            sub = slice(j * ROW_BLOCK, (j + 1) * ROW_BLOCK)
            for g in range(n_kv):
                lg = _dot(k_ref[pl.ds(r0 + j * ROW_BLOCK, ROW_BLOCK), g * HEAD_DIM:(g + 1) * HEAD_DIM], qgs[g])
                parts = []
                for r in range(rep):
                    x = jnp.where(sel[sub], lg[:, r * LANES:(r + 1) * LANES], MASKED)
                    lg_refs[g][slot, sub, r * LANES:(r + 1) * LANES] = x
                    parts.append(fold(x, jnp.max))
                part = jnp.concatenate(parts, axis=1)
                tops[g] = part if tops[g] is None else jnp.maximum(tops[g], part)
        return tuple(tops)

    def attend_unit(u, tops):
        r0 = pl.multiple_of(u * unit, unit)
        slot = u % 2
        next_tops = logits_of_unit(jnp.minimum(u + 1, n_units - 1), 1 - slot)
        m_old = [m_refs[g][...] for g in range(n_kv)]
        m_new = [jnp.maximum(m_old[g], jnp.max(tops[g], axis=0, keepdims=True)) for g in range(n_kv)]
        sums, pvs = [None] * n_kv, [None] * n_kv
        for j in range(n_sub):
            sub = slice(j * ROW_BLOCK, (j + 1) * ROW_BLOCK)
            for g in range(n_kv):
                p = jnp.exp2(lg_refs[g][slot, sub, :] - m_new[g])
                pv = _dot_tn(v_ref[pl.ds(r0 + j * ROW_BLOCK, ROW_BLOCK), g * HEAD_DIM:(g + 1) * HEAD_DIM], p.astype(BF16))
                part = fold(p, jnp.sum)
                sums[g] = part if sums[g] is None else sums[g] + part
                pvs[g] = pv if pvs[g] is None else pvs[g] + pv
        for g in range(n_kv):
            alpha = jnp.exp2(m_old[g] - m_new[g])
            m_refs[g][...] = m_new[g]
            l_refs[g][...] = alpha * l_refs[g][...] + jnp.sum(sums[g], axis=0, keepdims=True)
            acc_refs[g][...] = acc_refs[g][...] * alpha + pvs[g]
        return next_tops

    lax.fori_loop(0, n_units, attend_unit, logits_of_unit(0, 0))
    for g in range(n_kv):
        for r in range(rep):
            sl = slice(r * LANES, (r + 1) * LANES)
            out = acc_refs[g][:, sl] * (1.0 / l_refs[g][:, sl])
            h = g * rep + r
            o_ref[:, h * HEAD_DIM:(h + 1) * HEAD_DIM] = jnp.transpose(out).astype(o_ref.dtype)


def _dsa_prompt(aq_t, iq_t, misc_t, iw_row, ik, k, v, n_rows, n_valid, topk):
    n_blk = n_rows // ROW_BLOCK
    n_heads = aq_t.shape[1] // HEAD_DIM
    n_kv = k.shape[1] // HEAD_DIM
    rep = n_heads // n_kv
    unit = _row_tile(n_rows, 640)
    blk = lambda i: (i, 0, 0)
    const = lambda i: (0, 0)
    kern = functools.partial(_dsa_prompt_kernel, unit=unit, topk=topk, n_valid=n_valid, n_heads=n_heads, n_kv=n_kv,
                             iw_row=iw_row)
    return pl.pallas_call(
        kern,
        out_shape=jax.ShapeDtypeStruct((aq_t.shape[0] * ROW_BLOCK, n_heads * HEAD_DIM), BF16),
        grid=(n_blk,),
        in_specs=[pl.BlockSpec((1,) + aq_t.shape[1:], blk), pl.BlockSpec((1,) + iq_t.shape[1:], blk),
                  pl.BlockSpec((1,) + misc_t.shape[1:], blk),
                  pl.BlockSpec((n_rows, ik.shape[1]), const), pl.BlockSpec((n_rows, k.shape[1]), const),
                  pl.BlockSpec((n_rows, v.shape[1]), const)],
        out_specs=pl.BlockSpec((ROW_BLOCK, n_heads * HEAD_DIM), lambda i: (i, 0)),
        scratch_shapes=[pltpu.VMEM((n_rows, LANES), I32)]
                       + [pltpu.VMEM((1, rep * LANES), F32)] * (2 * n_kv)
                       + [pltpu.VMEM((HEAD_DIM, rep * LANES), F32)] * n_kv
                       + [pltpu.VMEM((2, unit, rep * LANES), F32)] * n_kv,
        compiler_params=_params("parallel"),
        name="dsa_prompt",
    )(aq_t, iq_t, misc_t, ik, k, v)


Q_PAD = SUBLANES


def _idx_sample_kernel(pt_ref, iq_ref, w_ref, new_ref, *refs, n_pages):
    del pt_ref
    pages, o_ref = refs[:n_pages], refs[n_pages]
    iq = iq_ref[0]
    w = w_ref[0]

    def score(keys_t):
        s = jnp.maximum(_dot(iq, keys_t), 0.0) * w
        return jnp.sum(s.reshape(IDX_HEADS, Q_PAD, LANES), axis=0) * IDX_SCALE

    for p in range(n_pages):
        o_ref[0, :, p * LANES:(p + 1) * LANES] = score(pages[p][0].astype(BF16))
    o_ref[0, :, n_pages * LANES:] = score(new_ref[0])


def _idx_sample(page_table, iq, w, new_ik_t, cache_idx_t):
    db = iq.shape[0]
    n_pages = page_table.shape[1]
    page = cache_idx_t.shape[2]
    assert page == LANES
    seq = lambda b, pt: (b, 0, 0)
    in_specs = [pl.BlockSpec((1,) + iq.shape[1:], seq), pl.BlockSpec((1,) + w.shape[1:], seq),
                pl.BlockSpec((1,) + new_ik_t.shape[1:], seq)]
    in_specs += [pl.BlockSpec((1, IDX_DIM, page), lambda b, pt, p=p: (pt[b, p], 0, 0)) for p in range(n_pages)]
    n_keys = (n_pages + 1) * LANES
    return pl.pallas_call(
        functools.partial(_idx_sample_kernel, n_pages=n_pages),
        out_shape=jax.ShapeDtypeStruct((db, Q_PAD, n_keys), F32),
        grid_spec=pltpu.PrefetchScalarGridSpec(
            num_scalar_prefetch=1, grid=(db,), in_specs=in_specs,
            out_specs=pl.BlockSpec((1, Q_PAD, n_keys), seq)),
        compiler_params=_params("parallel"),
        name="idx_sample",
    )(page_table, iq, w, new_ik_t, *([cache_idx_t] * n_pages))


def _topk_sample_kernel(s_ref, o_ref, key_ref, *, unit, topk, past_len, dec_seq):
    n_keys = s_ref.shape[0]
    n_units = n_keys // unit
    row = lax.broadcasted_iota(I32, (unit, LANES), 0)
    q = lax.broadcasted_iota(I32, (unit, LANES), 1) % Q_PAD
    for u in range(n_units):
        k_pos = u * unit + row
        adm = ((q < dec_seq) & (k_pos <= past_len + q)) | ((q >= dec_seq) & (k_pos == 0))
        key_ref[u * unit:(u + 1) * unit, :] = jnp.where(adm, _sortable(s_ref[u * unit:(u + 1) * unit, :]), INT_MIN)
    tau, last = _select_topk(key_ref, n_units, unit, topk, n_keys)
    for u in range(n_units):
        sel = _selected(key_ref[u * unit:(u + 1) * unit, :], u * unit, tau, last)
        o_ref[u * unit:(u + 1) * unit, :] = jnp.where(sel, 0.0, MASKED)


def _topk_sample(scores_t, topk, past_len, dec_seq):
    n_keys, n_q = scores_t.shape
    unit = _row_tile(n_keys, 640)
    kern = functools.partial(_topk_sample_kernel, unit=unit, topk=topk, past_len=past_len, dec_seq=dec_seq)
    return pl.pallas_call(
        kern,
        out_shape=jax.ShapeDtypeStruct((n_keys, n_q), F32),
        grid=(n_q // LANES,),
        in_specs=[pl.BlockSpec((n_keys, LANES), lambda i: (0, i))],
        out_specs=pl.BlockSpec((n_keys, LANES), lambda i: (0, i)),
        scratch_shapes=[pltpu.VMEM((n_keys, LANES), I32)],
        compiler_params=_params("parallel"),
        name="topk_sample",
    )(scores_t)


def _attn_sample_kernel(pt_ref, q_ref, bias_ref, bias_new_ref, k_new_ref, v_new_ref, *refs, n_kv, pages_per_step):
    del pt_ref
    k_pages = refs[:pages_per_step]
    v_pages = refs[pages_per_step:2 * pages_per_step]
    o_ref, m_ref, l_ref, acc_ref = refs[2 * pages_per_step:]
    c = pl.program_id(1)
    rep = q_ref.shape[2] // Q_PAD

    @pl.when(c == 0)
    def _():
        m_ref[...] = jnp.full_like(m_ref, MASKED)
        l_ref[...] = jnp.zeros_like(l_ref)
        acc_ref[...] = jnp.zeros_like(acc_ref)

    def update(g, kg, vg, bias):
        sel = jnp.concatenate([bias] * rep, axis=0) > 0.5 * MASKED
        lg = jnp.where(sel, _dot_nt(q_ref[0, g], kg), MASKED)
        m_old = m_ref[g]
        m_new = jnp.maximum(m_old, jnp.max(lg, axis=1, keepdims=True))
        alpha = jnp.exp2(m_old - m_new)
        p = jnp.where(sel, jnp.exp2(lg - m_new), 0.0)
        l_ref[g] = alpha * l_ref[g] + jnp.sum(p, axis=1, keepdims=True)
        acc_ref[g] = alpha * acc_ref[g] + _dot(p.astype(BF16), vg)
        m_ref[g] = m_new

    def head(ref, g):
        return ref[0, :, g * HEAD_DIM:(g + 1) * HEAD_DIM].astype(BF16)

    @pl.when(c == 0)
    def _():
        for g in range(n_kv):
            update(g, head(k_new_ref, g), head(v_new_ref, g), bias_new_ref[0])

    for g in range(n_kv):
        kg = jnp.concatenate([_page_head(r, g, n_kv) for r in k_pages], axis=0)
        vg = jnp.concatenate([_page_head(r, g, n_kv) for r in v_pages], axis=0)
        update(g, kg, vg, bias_ref[0])

    @pl.when(c == pl.num_programs(1) - 1)
    def _():
        for g in range(n_kv):
            o_ref[0, g] = acc_ref[g] * (1.0 / l_ref[g])


def _attn_sample(page_table, q, bias, new_k, new_v, cache_k, cache_v):
    db, n_kv, rows, _ = q.shape
    n_pages = page_table.shape[1]
    page_rows = cache_k.shape[1]
    page = page_rows // n_kv
    per_step = max(d for d in range(1, 17) if n_pages % d == 0)
    seq4 = lambda b, c, pt: (b, 0, 0, 0)
    seq3 = lambda b, c, pt: (b, 0, 0)
    in_specs = [pl.BlockSpec((1, n_kv, rows, HEAD_DIM), seq4),
                pl.BlockSpec((1, Q_PAD, per_step * page), lambda b, c, pt: (b, 0, c)),
                pl.BlockSpec((1, Q_PAD, page), lambda b, c, pt: (b, 0, n_pages)),
                pl.BlockSpec((1,) + new_k.shape[1:], seq3), pl.BlockSpec((1,) + new_v.shape[1:], seq3)]
    for _ in range(2):
        in_specs += [pl.BlockSpec((1, page_rows, HEAD_DIM), lambda b, c, pt, i=i: (pt[b, c * per_step + i], 0, 0))
                     for i in range(per_step)]
    return pl.pallas_call(
        functools.partial(_attn_sample_kernel, n_kv=n_kv, pages_per_step=per_step),
        out_shape=jax.ShapeDtypeStruct((db, n_kv, rows, HEAD_DIM), F32),
        grid_spec=pltpu.PrefetchScalarGridSpec(
            num_scalar_prefetch=1, grid=(db, n_pages // per_step), in_specs=in_specs,
            out_specs=pl.BlockSpec((1, n_kv, rows, HEAD_DIM), seq4),
            scratch_shapes=[pltpu.VMEM((n_kv, rows, 1), F32), pltpu.VMEM((n_kv, rows, 1), F32),
                            pltpu.VMEM((n_kv, rows, HEAD_DIM), F32)]),
        compiler_params=_params("parallel", "arbitrary"),
        name="attn_sample",
    )(page_table, q, bias, bias, new_k, new_v, *([cache_k] * per_step), *([cache_v] * per_step))


SB_EAGER_PAGES = 4
SB_PAGES_PER_STEP = 12


def _sb_sample(page_table, q, new_k, new_v, cache_k, cache_v, dec_seq):
    n_pages = page_table.shape[1]
    eager = min(SB_EAGER_PAGES, n_pages)
    o, carry = _sb_sample_call(page_table, q, (new_k, new_v), cache_k, cache_v, None, dec_seq,
                               n_pages - 1, 1, eager)
    rest = n_pages - eager
    if rest == 0:
        return o
    per_step = max(d for d in range(1, SB_PAGES_PER_STEP + 1) if rest % d == 0)
    finish = lambda: _sb_sample_call(page_table, q, None, cache_k, cache_v, (o, carry), dec_seq,
                                     rest - 1, rest // per_step, per_step)[0]
    return lax.cond(jnp.max(carry) >= SB_DONE2, finish, lambda: o)


def _pad_rows(a, rows):
    return jnp.pad(a, ((0, 0), (0, rows - a.shape[1])) + ((0, 0),) * (a.ndim - 2))


def kernel(x_prompt, x_sample, cache_a_k, cache_a_v, cache_a_idx_k, state_b, cache_c_k, cache_c_v, page_table,
           meta_tokens, norm_mix, norm_ffn, norm_final, w_in0, w_b_gate2, b_b_gate, b_head_norm, w_out0,
           w_in1, w_out1, w_ff1, w_ff2):
    batch, seq, d = x_prompt.shape
    assert batch == 1
    db, ds, _ = x_sample.shape
    assert ds <= Q_PAD and ds <= GLA_SUB
    t_p = N_META + seq
    t_pad = _round_up(t_p, ROW_BLOCK)
    n_s = db * ds
    m_s = _round_up(n_s, ROW_BLOCK)
    n_blk = t_pad // ROW_BLOCK
    n_pool, page = cache_a_k.shape[:2]
    n_pages = page_table.shape[1]
    past_len = n_pages * page
    topk_p = min(TOPK_MAX, seq // 4)
    topk_s = min(TOPK_MAX, (past_len + ds) // 4)
    srows = slice(t_pad, t_pad + n_s)

    h = jnp.concatenate([meta_tokens.astype(F32), x_prompt[0], jnp.zeros((t_pad - t_p, d), F32),
                         x_sample.reshape(n_s, d), jnp.zeros((m_s - n_s, d), F32)], axis=0)
    pos = jnp.concatenate([jnp.arange(t_pad), past_len + jnp.arange(n_s) % ds, jnp.zeros((m_s - n_s,), I32)])
    tab = _rope_tables(pos)

    lay, ref_off, total = _even_layout(d)
    w0 = _permute_w_in0(w_in0, lay, ref_off, total).astype(BF16)
    gate_w = _gate_weights(w_b_gate2, b_b_gate, lay)
    seg = lambda a, name: a[:, lay[name][0]:lay[name][0] + lay[name][1]]

    xn = _rmsnorm(h, norm_mix[0], BF16)
    p0 = _matmul(xn, w0, F32)
    aq_t, iq_t, misc_t, ak, ik, k_bf, v_bf, ik_bf = _rope(p0, tab, lay)
    a_kv = ak.shape[1]
    n_heads_a = aq_t.shape[1] // HEAD_DIM
    rep_a = n_heads_a // A_KV_HEADS
    iw_row = lay["iw"][0] - lay["misc"][0]

    oa = _dsa_prompt(aq_t, iq_t, misc_t, iw_row, ik_bf, k_bf, v_bf, t_pad, t_p, topk_p)
    ob, state_p = _gla_prompt(p0, lay, t_pad, t_p, gate_w, b_head_norm)

    sample_rows = lambda a_t: a_t[n_blk:].transpose(0, 2, 1).reshape(m_s, a_t.shape[1])[:n_s]
    aq, iq = sample_rows(aq_t), sample_rows(iq_t)
    iw = sample_rows(misc_t[:, iw_row:iw_row + IDX_HEADS])
    iq_s = iq.reshape(db, ds, IDX_HEADS, IDX_DIM).transpose(0, 2, 1, 3)
    iq_s = jnp.pad(iq_s, ((0, 0), (0, 0), (0, Q_PAD - ds), (0, 0))).reshape(db, IDX_HEADS * Q_PAD, IDX_DIM)
    iw_s = iw.reshape(db, ds, IDX_HEADS).transpose(0, 2, 1)
    iw_s = jnp.pad(iw_s, ((0, 0), (0, 0), (0, Q_PAD - ds))).reshape(db, IDX_HEADS * Q_PAD, 1)
    iw_s = jnp.broadcast_to(iw_s, (db, IDX_HEADS * Q_PAD, LANES))
    new_ik_t = _pad_rows(ik_bf[srows].reshape(db, ds, IDX_DIM), page).transpose(0, 2, 1)
    scores = _idx_sample(page_table, iq_s, iw_s, new_ik_t, cache_a_idx_k.transpose(0, 2, 1))
    n_keys = scores.shape[2]
    n_q = _round_up(db * Q_PAD, LANES)
    scores_t = jnp.pad(scores.reshape(db * Q_PAD, n_keys).T, ((0, 0), (0, n_q - db * Q_PAD)))
    bias = _topk_sample(scores_t, topk_s, past_len, ds)[:, :db * Q_PAD].T.reshape(db, Q_PAD, n_keys)
    aq_s = aq.reshape(db, ds, A_KV_HEADS, rep_a, HEAD_DIM).transpose(0, 2, 3, 1, 4)
    aq_s = jnp.pad(aq_s, ((0, 0), (0, 0), (0, 0), (0, Q_PAD - ds), (0, 0))).reshape(db, A_KV_HEADS, rep_a * Q_PAD, HEAD_DIM)
    new_k = _pad_rows(k_bf[srows].reshape(db, ds, a_kv), page)
    new_v = _pad_rows(v_bf[srows].reshape(db, ds, a_kv), page)
    oa_s = _attn_sample(page_table, aq_s, bias, new_k, new_v,
                        cache_a_k.reshape(n_pool, page * A_KV_HEADS, HEAD_DIM),
                        cache_a_v.reshape(n_pool, page * A_KV_HEADS, HEAD_DIM))
    oa_s = oa_s.reshape(db, A_KV_HEADS, rep_a, Q_PAD, HEAD_DIM)[:, :, :, :ds].transpose(0, 3, 1, 2, 4)
    oa_s = oa_s.reshape(n_s, n_heads_a * HEAD_DIM).astype(BF16)

    chunk_s = lambda name: _pad_rows(seg(p0, name)[srows].reshape(db, ds, lay[name][1]), GLA_SUB)
    ob_s, state_s = _gla_sample(chunk_s("bq"), chunk_s("bk"), chunk_s("bv"), chunk_s("br"), chunk_s("misc"),
                                state_b, ds, gate_w, b_head_norm)
    ob_s = ob_s[:, :ds].reshape(n_s, -1)

    oa = oa.at[srows].set(oa_s)
    ob = ob.at[srows].set(ob_s)
    h, xn = _proj_res_norm([oa, ob], w_out0.astype(BF16), h, norm_ffn[0])
    w_ff1_bf, w_ff2_bf = w_ff1.astype(BF16), w_ff2.astype(BF16)
    h, xn = _mlp(xn, h, w_ff1_bf, w_ff2_bf, 0, norm_mix[1], BF16)

    c_kv = C_KV_HEADS * HEAD_DIM
    c_q = w_in1.shape[1] - 2 * c_kv
    rep_c = c_q // c_kv
    q1 = _matmul(xn, w_in1[:, :c_q].astype(BF16), BF16, scale=SB_Q_SCALE)
    kv1 = _matmul(xn, w_in1[:, c_q:].astype(BF16), F32)
    kv1_bf = kv1.astype(BF16)
    oc = _sb_prompt(q1, kv1_bf[:, :c_kv], kv1_bf[:, c_kv:], t_pad)
    q1_s = q1[srows].reshape(db, ds, C_KV_HEADS, rep_c, HEAD_DIM).transpose(0, 2, 3, 1, 4)
    q1_s = q1_s.reshape(db, C_KV_HEADS, rep_c * ds, HEAD_DIM)
    new_ck = _pad_rows(kv1_bf[srows, :c_kv].reshape(db, ds, c_kv), page)
    new_cv = _pad_rows(kv1_bf[srows, c_kv:].reshape(db, ds, c_kv), page)
    oc_s = _sb_sample(page_table, q1_s, new_ck, new_cv, cache_c_k.reshape(n_pool, page * C_KV_HEADS, HEAD_DIM),
                      cache_c_v.reshape(n_pool, page * C_KV_HEADS, HEAD_DIM), ds)
    oc_s = oc_s.reshape(db, C_KV_HEADS, rep_c, ds, HEAD_DIM).transpose(0, 3, 1, 2, 4).reshape(n_s, c_q).astype(BF16)
    h, xn = _proj_res_norm([oc.at[srows].set(oc_s)], w_out1.astype(BF16), h, norm_ffn[1])
    h, y = _mlp(xn, h, w_ff1_bf, w_ff2_bf, 1, norm_final, F32)

    y_prompt = y[N_META:t_p][None]
    y_sample = y[srows].reshape(db, ds, d)
    kv_p = lambda a, heads: a[:t_p].reshape(1, t_p, heads, HEAD_DIM)
    kv_s = lambda a, heads: a[srows].reshape(db, ds, heads, HEAD_DIM)
    return (y_prompt, y_sample,
            kv_p(ak, A_KV_HEADS), kv_p(seg(p0, "av"), A_KV_HEADS), ik[:t_p, :IDX_DIM][None], state_p[None],
            kv_p(kv1[:, :c_kv], C_KV_HEADS), kv_p(kv1[:, c_kv:], C_KV_HEADS),
            kv_s(ak, A_KV_HEADS), kv_s(seg(p0, "av"), A_KV_HEADS), ik[srows, :IDX_DIM].reshape(db, ds, IDX_DIM),
            state_s, kv_s(kv1[:, :c_kv], C_KV_HEADS), kv_s(kv1[:, c_kv:], C_KV_HEADS))
```

```python
import functools
import math

import jax
import jax.numpy as jnp
from jax import lax
from jax.experimental import pallas as pl
from jax.experimental.pallas import tpu as pltpu

F32 = jnp.float32
BF16 = jnp.bfloat16
I32 = jnp.int32

N_META = 16
HEAD_DIM = 128
ROPE_THETA = 10000.0
EPS = 1e-6
A_KV_HEADS = 2
IDX_HEADS = 16
IDX_DIM = 64
TOPK_MAX = 256
B_HEADS = 4
B_GATE_RANK = 16
B_GATE_TAU = 16.0
C_KV_HEADS = 4

LANES = 128
SUBLANES = 8
VMEM_LIMIT_BYTES = 56 * 1024 * 1024

ROW_BLOCK = 128
INT_MIN = -(2 ** 31)


def _round_up(x, m):
    return (x + m - 1) // m * m


def _row_tile(m, cap):
    best = ROW_BLOCK
    for k in range(1, m // ROW_BLOCK + 1):
        t = k * ROW_BLOCK
        if t <= cap and m % t == 0:
            best = t
    return best


def _col_tile(n, cap):
    best = LANES
    for k in range(1, n // LANES + 1):
        t = k * LANES
        if t <= cap and n % t == 0:
            best = t
    return best


def _params(*sem):
    return pltpu.CompilerParams(dimension_semantics=sem, vmem_limit_bytes=VMEM_LIMIT_BYTES)


def _rms(x, g):
    ms = jnp.mean(x * x, axis=-1, keepdims=True)
    return x * lax.rsqrt(ms + EPS) * g


def _rmsnorm_kernel(x_ref, g_ref, o_ref):
    o_ref[...] = _rms(x_ref[...], g_ref[...]).astype(o_ref.dtype)


def _rmsnorm(x, g, out_dtype):
    m, d = x.shape
    tm = _row_tile(m, 512)
    return pl.pallas_call(
        _rmsnorm_kernel,
        out_shape=jax.ShapeDtypeStruct((m, d), out_dtype),
        grid=(m // tm,),
        in_specs=[pl.BlockSpec((tm, d), lambda i: (i, 0)), pl.BlockSpec((1, d), lambda i: (0, 0))],
        out_specs=pl.BlockSpec((tm, d), lambda i: (i, 0)),
        compiler_params=_params("parallel"),
        name="rmsnorm",
    )(x, g.reshape(1, d))


def _matmul_kernel(x_ref, w_ref, o_ref, *, scale):
    acc = jnp.dot(x_ref[...], w_ref[...], preferred_element_type=F32)
    o_ref[...] = (acc if scale is None else acc * scale).astype(o_ref.dtype)


def _matmul(x, w, out_dtype, scale=None):
    m, k = x.shape
    n = w.shape[1]
    tm = _row_tile(m, 1408)
    tn = _col_tile(n, 1152)
    return pl.pallas_call(
        functools.partial(_matmul_kernel, scale=scale),
        out_shape=jax.ShapeDtypeStruct((m, n), out_dtype),
        grid=(m // tm, n // tn),
        in_specs=[pl.BlockSpec((tm, k), lambda i, j: (i, 0)), pl.BlockSpec((k, tn), lambda i, j: (0, j))],
        out_specs=pl.BlockSpec((tm, tn), lambda i, j: (i, j)),
        compiler_params=_params("parallel", "parallel"),
        name="matmul",
    )(x, w)


def _kv_proj_kernel(x_ref, w_ref, k_ref, v_ref, kbf_ref, vbf_ref, *, n_heads):
    acc = jnp.dot(x_ref[...], w_ref[...], preferred_element_type=F32)
    tm = x_ref.shape[0]
    ckv = n_heads * HEAD_DIM
    for h in range(n_heads):
        k_ref[pl.ds(h, tm, stride=n_heads), :] = acc[:, h * HEAD_DIM:(h + 1) * HEAD_DIM]
        v_ref[pl.ds(h, tm, stride=n_heads), :] = acc[:, ckv + h * HEAD_DIM:ckv + (h + 1) * HEAD_DIM]
    kbf_ref[...] = acc[:, :ckv].astype(BF16)
    vbf_ref[...] = acc[:, ckv:].astype(BF16)


def _kv_proj(x, w, n_heads):
    m, k = x.shape
    ckv = n_heads * HEAD_DIM
    assert w.shape == (k, 2 * ckv)
    tm = _row_tile(m, 768)
    row = lambda i: (i, 0)
    return pl.pallas_call(
        functools.partial(_kv_proj_kernel, n_heads=n_heads),
        out_shape=(jax.ShapeDtypeStruct((m * n_heads, HEAD_DIM), F32), jax.ShapeDtypeStruct((m * n_heads, HEAD_DIM), F32),
                   jax.ShapeDtypeStruct((m, ckv), BF16), jax.ShapeDtypeStruct((m, ckv), BF16)),
        grid=(m // tm,),
        in_specs=[pl.BlockSpec((tm, k), row), pl.BlockSpec((k, 2 * ckv), lambda i: (0, 0))],
        out_specs=(pl.BlockSpec((tm * n_heads, HEAD_DIM), row), pl.BlockSpec((tm * n_heads, HEAD_DIM), row),
                   pl.BlockSpec((tm, ckv), row), pl.BlockSpec((tm, ckv), row)),
        compiler_params=_params("parallel"),
        name="kv_proj",
    )(x, w)


def _proj_res_norm_kernel(*refs, n_in):
    x_refs, w_refs = refs[:n_in], refs[n_in:2 * n_in]
    h_ref, g_ref, hout_ref, xn_ref = refs[2 * n_in:]
    hn = h_ref[...]
    for x_ref, w_ref in zip(x_refs, w_refs):
        hn = hn + jnp.dot(x_ref[...], w_ref[...], preferred_element_type=F32)
    hout_ref[...] = hn
    xn_ref[...] = _rms(hn, g_ref[...]).astype(xn_ref.dtype)


def _proj_res_norm(xs, w, h, g):
    m, k = xs[0].shape
    assert all(x.shape == (m, k) for x in xs) and w.shape[0] == k * len(xs)
    d = w.shape[1]
    tm = _row_tile(m, 384)
    row = lambda i: (i, 0)
    in_specs = [pl.BlockSpec((tm, k), row) for _ in xs]
    in_specs += [pl.BlockSpec((k, d), lambda i, j=j: (j, 0)) for j in range(len(xs))]
    in_specs += [pl.BlockSpec((tm, d), row), pl.BlockSpec((1, d), lambda i: (0, 0))]
    return pl.pallas_call(
        functools.partial(_proj_res_norm_kernel, n_in=len(xs)),
        out_shape=(jax.ShapeDtypeStruct((m, d), F32), jax.ShapeDtypeStruct((m, d), BF16)),
        grid=(m // tm,),
        in_specs=in_specs,
        out_specs=(pl.BlockSpec((tm, d), row), pl.BlockSpec((tm, d), row)),
        compiler_params=_params("parallel"),
        name="proj_res_norm",
    )(*xs, *([w] * len(xs)), h, g.reshape(1, d))


def _mlp_kernel(xn_ref, h_ref, w1_ref, w2_ref, g_ref, hout_ref, xnout_ref):
    f = pl.program_id(1)

    @pl.when(f == 0)
    def _():
        hout_ref[...] = h_ref[...]

    a = jnp.maximum(jnp.dot(xn_ref[...], w1_ref[...], preferred_element_type=F32), 0.0)
    hout_ref[...] += jnp.dot((a * a).astype(BF16), w2_ref[...], preferred_element_type=F32)

    @pl.when(f == pl.num_programs(1) - 1)
    def _():
        xnout_ref[...] = _rms(hout_ref[...], g_ref[...]).astype(xnout_ref.dtype)


def _mlp(xn, h, w1, w2, layer, g, norm_dtype):
    m, d = xn.shape
    dff = w1.shape[2]
    tm = _row_tile(m, 768)
    tf = _col_tile(dff, 1024)
    return pl.pallas_call(
        _mlp_kernel,
        out_shape=(jax.ShapeDtypeStruct((m, d), F32), jax.ShapeDtypeStruct((m, d), norm_dtype)),
        grid=(m // tm, dff // tf),
        in_specs=[pl.BlockSpec((tm, d), lambda i, f: (i, 0)),
                  pl.BlockSpec((tm, d), lambda i, f: (i, 0), pipeline_mode=pl.Buffered(1)),
                  pl.BlockSpec((None, d, tf), lambda i, f: (layer, 0, f)),
                  pl.BlockSpec((None, tf, d), lambda i, f: (layer, f, 0)),
                  pl.BlockSpec((1, d), lambda i, f: (0, 0))],
        out_specs=(pl.BlockSpec((tm, d), lambda i, f: (i, 0)), pl.BlockSpec((tm, d), lambda i, f: (i, 0))),
        compiler_params=_params("parallel", "arbitrary"),
        name="mlp",
    )(xn, h, w1, w2, g.reshape(1, d))


def _even_layout(d_model):
    a_q = d_model // 2
    a_kv = A_KV_HEADS * HEAD_DIM
    idx_q = IDX_HEADS * IDX_DIM
    b_v = d_model // 2
    b_qk = b_v // 2
    ref_sizes = dict(aq=a_q, ak=a_kv, av=a_kv, iq=idx_q, ik=IDX_DIM, iw=IDX_HEADS, bq=b_qk, bk=b_qk, bv=b_v,
                     bglr=B_GATE_RANK, br=b_v)
    ref_off, off = {}, 0
    for name, size in ref_sizes.items():
        ref_off[name] = off
        off += size
    lay, off = {}, 0
    for name in ("aq", "iq", "bq", "bk", "bv", "br", "ak", "av", "ik", "iw", "bglr"):
        lay[name] = (off, ref_sizes[name])
        off += ref_sizes[name]
    lay["misc"] = (lay["ik"][0], LANES)
    total = lay["misc"][0] + LANES
    assert off <= total
    return lay, ref_off, total


def _permute_w_in0(w_in0, lay, ref_off, total):
    parts, end = [], 0
    for name, (off, size) in lay.items():
        if name != "misc":
            assert off == end
            parts.append(w_in0[:, ref_off[name]:ref_off[name] + size])
            end = off + size
    parts.append(jnp.zeros((w_in0.shape[0], total - end), w_in0.dtype))
    return jnp.concatenate(parts, axis=1)


def _rope_freqs():
    def inv(d):
        return ROPE_THETA ** (-jnp.arange(d // 2, dtype=F32) * (2.0 / d))
    pad = jnp.zeros((LANES - HEAD_DIM // 2 - IDX_DIM // 2,), F32)
    return jnp.concatenate([inv(HEAD_DIM), inv(IDX_DIM), pad]).reshape(1, LANES)


def _rope_kernel(q_ref, kv_ref, inv_ref, aq_ref, iq_ref, misc_ref, ak_ref, av_ref, ik_ref, kbf_ref, vbf_ref, ikbf_ref,
                 *, n_aq, n_iq, n_ak, n_prompt_rows, past_len, dec_seq):
    tr = q_ref.shape[0]
    half, quarter = HEAD_DIM // 2, IDX_DIM // 2
    row = pl.program_id(0) * tr + lax.broadcasted_iota(I32, (tr, 1), 0)
    pos = jnp.where(row < n_prompt_rows, row, past_len + (row - n_prompt_rows) % dec_seq).astype(F32)
    ang = pos * inv_ref[...]
    c, s = jnp.cos(ang), jnp.sin(ang)
    cos128 = jnp.concatenate([c[:, :half]] * 2, axis=1)
    sin128 = jnp.concatenate([-s[:, :half], s[:, :half]], axis=1)
    c_i, s_i = c[:, half:half + quarter], s[:, half:half + quarter]
    cos64 = jnp.concatenate([c_i] * (LANES // quarter), axis=1)
    sin64 = jnp.concatenate([-s_i, s_i] * (LANES // IDX_DIM), axis=1)

    def rope128(x):
        return x * cos128 + pltpu.roll(x, half, 1) * sin128

    lane = lax.broadcasted_iota(I32, (tr, LANES), 1)
    low_half = (lane % IDX_DIM) < quarter

    def rope64(x):
        rot = jnp.where(low_half, pltpu.roll(x, LANES - quarter, 1), pltpu.roll(x, quarter, 1))
        return x * cos64 + rot * sin64

    n_blk = tr // ROW_BLOCK

    def put_transposed(ref, c, x):
        for b in range(n_blk):
            tile = jnp.transpose(x[b * ROW_BLOCK:(b + 1) * ROW_BLOCK, :])
            ref[b, c * LANES:(c + 1) * LANES, :] = tile.astype(ref.dtype)

    for c in range(n_aq):
        sl = slice(c * LANES, (c + 1) * LANES)
        put_transposed(aq_ref, c, rope128(q_ref[:, sl]) * EXP2_SCALE)
    for c in range(n_iq):
        put_transposed(iq_ref, c, rope64(q_ref[:, n_aq * LANES + c * LANES:n_aq * LANES + (c + 1) * LANES]))
    put_transposed(misc_ref, 0, kv_ref[:, 2 * n_ak * LANES:(2 * n_ak + 1) * LANES])
    for c in range(n_ak):
        sl = slice(c * LANES, (c + 1) * LANES)
        kr = rope128(kv_ref[:, sl])
        vr = kv_ref[:, n_ak * LANES + c * LANES:n_ak * LANES + (c + 1) * LANES]
        ak_ref[pl.ds(c, tr, stride=n_ak), :] = kr
        av_ref[pl.ds(c, tr, stride=n_ak), :] = vr
        kbf_ref[:, sl] = kr.astype(BF16)
        vbf_ref[:, sl] = vr.astype(BF16)
    ik = rope64(kv_ref[:, 2 * n_ak * LANES:(2 * n_ak + 1) * LANES])
    ik_ref[...] = ik
    ikbf_ref[...] = ik[:, :IDX_DIM].astype(BF16)


def _rope(p0, lay, n_prompt_rows, past_len, dec_seq):
    m = p0.shape[0]
    tr = _row_tile(m, 384)
    a_q, a_kv, idx_q = lay["aq"][1], lay["ak"][1], lay["iq"][1]
    qw = a_q + idx_q
    kvw = 2 * a_kv + LANES
    assert lay["aq"][0] == 0 and lay["iq"][0] == a_q and lay["ak"][0] % kvw == 0
    n_ak = a_kv // LANES
    kern = functools.partial(_rope_kernel, n_aq=a_q // LANES, n_iq=idx_q // LANES, n_ak=n_ak,
                             n_prompt_rows=n_prompt_rows, past_len=past_len, dec_seq=dec_seq)
    row = lambda i: (i, 0)
    blk = lambda i: (i, 0, 0)
    nb, tb = m // ROW_BLOCK, tr // ROW_BLOCK
    return pl.pallas_call(
        kern,
        out_shape=(jax.ShapeDtypeStruct((nb, a_q, ROW_BLOCK), BF16), jax.ShapeDtypeStruct((nb, idx_q, ROW_BLOCK), BF16),
                   jax.ShapeDtypeStruct((nb, LANES, ROW_BLOCK), F32),
                   jax.ShapeDtypeStruct((m * n_ak, LANES), F32), jax.ShapeDtypeStruct((m * n_ak, LANES), F32),
                   jax.ShapeDtypeStruct((m, LANES), F32),
                   jax.ShapeDtypeStruct((m, a_kv), BF16), jax.ShapeDtypeStruct((m, a_kv), BF16),
                   jax.ShapeDtypeStruct((m, IDX_DIM), BF16)),
        grid=(m // tr,),
        in_specs=[pl.BlockSpec((tr, qw), row),
                  pl.BlockSpec((tr, kvw), lambda i: (i, lay["ak"][0] // kvw)),
                  pl.BlockSpec((1, LANES), lambda i: (0, 0))],
        out_specs=(pl.BlockSpec((tb, a_q, ROW_BLOCK), blk), pl.BlockSpec((tb, idx_q, ROW_BLOCK), blk),
                   pl.BlockSpec((tb, LANES, ROW_BLOCK), blk),
                   pl.BlockSpec((tr * n_ak, LANES), row), pl.BlockSpec((tr * n_ak, LANES), row),
                   pl.BlockSpec((tr, LANES), row), pl.BlockSpec((tr, a_kv), row), pl.BlockSpec((tr, a_kv), row),
                   pl.BlockSpec((tr, IDX_DIM), row)),
        compiler_params=_params("parallel"),
        name="rope",
    )(p0, p0, _rope_freqs())


GLA_SUB = 16


def _split_bf16(x):
    hi = x.astype(BF16)
    return hi, (x - hi.astype(F32)).astype(BF16)


def _log_sigmoid(x):
    return jnp.minimum(x, 0.0) - jnp.log(1.0 + jnp.exp(-jnp.abs(x)))


def _dot(a, b):
    return jnp.dot(a, b, preferred_element_type=F32)


def _dot_nt(a, b):
    return lax.dot_general(a, b, (((1,), (1,)), ((), ())), preferred_element_type=F32)


def _dot_tn(a, b):
    return lax.dot_general(a, b, (((0,), (0,)), ((), ())), preferred_element_type=F32)


def _gla_gate(misc, w2_hi, w2_lo, bias, valid):
    m_hi, m_lo = _split_bf16(misc)
    x = _dot(m_hi, w2_hi) + _dot(m_lo, w2_hi) + _dot(m_hi, w2_lo) + bias
    return jnp.where(valid, _log_sigmoid(x) * (1.0 / B_GATE_TAU), 0.0)


def _gla_chunk(q, k, v, g, state_ref, dk, dv):
    c = q.shape[0]
    n_sub = c // GLA_SUB
    ri = lax.broadcasted_iota(I32, (c, c), 0)
    ci = lax.broadcasted_iota(I32, (c, c), 1)
    causal = ri >= ci
    tri = jnp.where(causal, 1.0, 0.0).astype(BF16)
    gcum = _dot(jnp.concatenate([tri, tri], axis=1), jnp.concatenate(_split_bf16(g), axis=0))
    scale = dk ** -0.5
    pad = _round_up(c, LANES)
    outs = []
    for h in range(B_HEADS):
        gh = gcum[:, h * dk:(h + 1) * dk]
        qh = q[:, h * dk:(h + 1) * dk] * scale
        kh = k[:, h * dk:(h + 1) * dk]
        vh = v[:, h * dv:(h + 1) * dv].astype(BF16)
        q_parts, k_parts = [], []
        for i in range(n_sub):
            lo, hi = i * GLA_SUB, (i + 1) * GLA_SUB
            base = gh[lo - 1:lo, :] if i else jnp.zeros((1, dk), F32)
            q_parts.append(qh[lo:hi] * jnp.exp(gh[lo:hi] - base))
            k_parts.append(kh[:hi] * jnp.exp(base - gh[:hi]))
            if pad > hi:
                k_parts.append(jnp.zeros((pad - hi, dk), F32))
        qt = jnp.concatenate(q_parts, axis=0).astype(BF16)
        kt = jnp.concatenate(k_parts, axis=0).astype(BF16)
        r = _dot_nt(qt, kt)
        att = jnp.concatenate([r[i * GLA_SUB:(i + 1) * GLA_SUB, i * pad:i * pad + c] for i in range(n_sub)], axis=0)
        att = jnp.where(causal, att, 0.0).astype(BF16)
        s = state_ref[h]
        o = _dot(att, vh) + _dot((qh * jnp.exp(gh)).astype(BF16), s.astype(BF16))
        last = gh[c - 1:c, :]
        k_tail = (kh * jnp.exp(last - gh)).astype(BF16)
        decay = jnp.transpose(jnp.broadcast_to(jnp.exp(last), (dk, dk)))
        decay = jnp.concatenate([decay] * (dv // dk), axis=1)
        state_ref[h] = decay * s + _dot_tn(k_tail, vh)
        outs.append(o)
    return outs


def _gla_out(outs, r, head_norm, dv):
    res = []
    for h, o in enumerate(outs):
        gate = r[:, h * dv:(h + 1) * dv]
        gate = gate * (1.0 / (1.0 + jnp.exp(-gate)))
        res.append(_rms(o, head_norm) * gate)
    return jnp.concatenate(res, axis=1)


def _gla_prompt_kernel(q_ref, k_ref, v_ref, r_ref, misc_ref, w2h_ref, w2l_ref, b_ref, hn_ref, o_ref, state_ref,
                       *, chunk, n_valid, dk, dv):
    blk = pl.program_id(0)

    @pl.when(blk == 0)
    def _():
        state_ref[...] = jnp.zeros_like(state_ref)

    rows = q_ref.shape[0]
    for c0 in range(0, rows, chunk):
        sl = slice(c0, c0 + chunk)
        row = blk * rows + c0 + lax.broadcasted_iota(I32, (chunk, 1), 0)
        g = _gla_gate(misc_ref[sl, :], w2h_ref[...], w2l_ref[...], b_ref[...], row < n_valid)
        outs = _gla_chunk(q_ref[sl, :], k_ref[sl, :], v_ref[sl, :], g, state_ref, dk, dv)
        o_ref[sl, :] = _gla_out(outs, r_ref[sl, :], hn_ref[...], dv).astype(o_ref.dtype)


def _gla_sample_kernel(q_ref, k_ref, v_ref, r_ref, misc_ref, w2h_ref, w2l_ref, b_ref, hn_ref, s_ref, o_ref,
                       state_ref, *, n_valid, dk, dv):
    state_ref[0] = s_ref[0]
    rows = q_ref.shape[1]
    row = lax.broadcasted_iota(I32, (rows, 1), 0)
    g = _gla_gate(misc_ref[0], w2h_ref[...], w2l_ref[...], b_ref[...], row < n_valid)
    outs = _gla_chunk(q_ref[0], k_ref[0], v_ref[0], g, state_ref.at[0], dk, dv)
    o_ref[0] = _gla_out(outs, r_ref[0], hn_ref[...], dv).astype(o_ref.dtype)


def _gate_weights(w_b_gate2, b_b_gate, lay):
    off = lay["bglr"][0] - lay["misc"][0]
    w2 = jnp.zeros((LANES, w_b_gate2.shape[1]), F32).at[off:off + B_GATE_RANK].set(w_b_gate2)
    hi = w2.astype(BF16)
    lo = (w2 - hi.astype(F32)).astype(BF16)
    return hi, lo, b_b_gate.reshape(1, -1)


def _gla_prompt(p0, lay, n_rows, n_valid, gate_w, head_norm, chunk=64):
    bqk, bv = lay["bq"][1], lay["bv"][1]
    dk, dv = bqk // B_HEADS, bv // B_HEADS
    w2h, w2l, bias = gate_w
    col = lambda name: (lambda i, o=lay[name][0] // lay[name][1]: (i, o))
    const = lambda i: (0, 0)
    kern = functools.partial(_gla_prompt_kernel, chunk=chunk, n_valid=n_valid, dk=dk, dv=dv)
    return pl.pallas_call(
        kern,
        out_shape=(jax.ShapeDtypeStruct((p0.shape[0], bv), BF16), jax.ShapeDtypeStruct((B_HEADS, dk, dv), F32)),
        grid=(n_rows // ROW_BLOCK,),
        in_specs=[pl.BlockSpec((ROW_BLOCK, bqk), col("bq")), pl.BlockSpec((ROW_BLOCK, bqk), col("bk")),
                  pl.BlockSpec((ROW_BLOCK, bv), col("bv")), pl.BlockSpec((ROW_BLOCK, bv), col("br")),
                  pl.BlockSpec((ROW_BLOCK, LANES), col("misc")),
                  pl.BlockSpec(w2h.shape, const), pl.BlockSpec(w2l.shape, const), pl.BlockSpec(bias.shape, const),
                  pl.BlockSpec((1, dv), const)],
        out_specs=(pl.BlockSpec((ROW_BLOCK, bv), lambda i: (i, 0)),
                   pl.BlockSpec((B_HEADS, dk, dv), lambda i: (0, 0, 0))),
        compiler_params=_params("arbitrary"),
        name="gla_prompt",
    )(p0, p0, p0, p0, p0, w2h, w2l, bias, head_norm.reshape(1, dv))


def _gla_sample(q, k, v, r, misc, state, n_valid, gate_w, head_norm):
    db, rows, bqk = q.shape
    bv = v.shape[2]
    dk, dv = bqk // B_HEADS, bv // B_HEADS
    w2h, w2l, bias = gate_w
    seq = lambda i: (i, 0, 0)
    const = lambda i: (0, 0)
    kern = functools.partial(_gla_sample_kernel, n_valid=n_valid, dk=dk, dv=dv)
    return pl.pallas_call(
        kern,
        out_shape=(jax.ShapeDtypeStruct((db, rows, bv), BF16), jax.ShapeDtypeStruct(state.shape, F32)),
        grid=(db,),
        in_specs=[pl.BlockSpec((1, rows, bqk), seq), pl.BlockSpec((1, rows, bqk), seq),
                  pl.BlockSpec((1, rows, bv), seq), pl.BlockSpec((1, rows, bv), seq),
                  pl.BlockSpec((1, rows, LANES), seq),
                  pl.BlockSpec(w2h.shape, const), pl.BlockSpec(w2l.shape, const), pl.BlockSpec(bias.shape, const),
                  pl.BlockSpec((1, dv), const),
                  pl.BlockSpec((1, B_HEADS, dk, dv), lambda i: (i, 0, 0, 0))],
        out_specs=(pl.BlockSpec((1, rows, bv), seq), pl.BlockSpec((1, B_HEADS, dk, dv), lambda i: (i, 0, 0, 0))),
        compiler_params=_params("parallel"),
        name="gla_sample",
    )(q, k, v, r, misc, w2h, w2l, bias, head_norm.reshape(1, dv), state)


SB_DONE = -120.0
SB_Q_SCALE = HEAD_DIM ** -0.5 * math.log2(math.e)
SB_DONE2 = SB_DONE * math.log2(math.e)


def _sb_suffix_matrix():
    j = lax.broadcasted_iota(I32, (2 * LANES, 2 * LANES), 0) % LANES
    s = lax.broadcasted_iota(I32, (2 * LANES, 2 * LANES), 1)
    return jnp.where((j > s) | (s >= LANES), 1.0, 0.0).astype(BF16)


def _sb_blocks(qs, ks, vs, strict, carries, accs, suffix):
    zs = [_dot_nt(q, k) for q, k in zip(qs, ks)]
    keeps, takes = [], []
    for z in zs:
        soft = jnp.log2(1.0 + jnp.exp2(-jnp.abs(z)))
        log_keep = -(jnp.maximum(z, 0.0) + soft)
        keeps.append(log_keep if strict is None else jnp.where(strict, log_keep, 0.0))
        takes.append(jnp.minimum(z, 0.0) - soft)
    sums = []
    for lk in keeps:
        sums.append(_dot(jnp.concatenate(_split_bf16(lk), axis=1), suffix))
    ws = []
    for take, s, c in zip(takes, sums, carries):
        w = jnp.exp2(take + (s[:, :LANES] + c))
        ws.append((w if strict is None else jnp.where(strict, w, 0.0)).astype(BF16))
    new_carries = [c + s[:, LANES:] for c, s in zip(carries, sums)]
    new_accs = [a + _dot(w, v) for a, w, v in zip(accs, ws, vs)]
    return new_carries, new_accs


def _sb_prompt_kernel(q_ref, k_ref, v_ref, o_ref, *scratch, n_kv, rep):
    carry_refs, acc_refs = scratch[:n_kv], scratch[n_kv:]
    b = pl.program_id(0)
    suffix = _sb_suffix_matrix()
    for ref in scratch:
        ref[...] = jnp.zeros_like(ref)
    rows = rep * ROW_BLOCK
    q_row = lax.broadcasted_iota(I32, (rows, LANES), 0) % ROW_BLOCK
    col = lax.broadcasted_iota(I32, (rows, LANES), 1)
    head = lambda g: slice(g * HEAD_DIM, (g + 1) * HEAD_DIM)

    def visit(kb, strict):
        k0 = pl.multiple_of(kb * ROW_BLOCK, ROW_BLOCK)
        qs = [jnp.concatenate([q_ref[:, head(g * rep + r)] for r in range(rep)], axis=0) for g in range(n_kv)]
        ks = [k_ref[pl.ds(k0, ROW_BLOCK), head(g)] for g in range(n_kv)]
        vs = [v_ref[pl.ds(k0, ROW_BLOCK), head(g)] for g in range(n_kv)]
        carries, accs = _sb_blocks(qs, ks, vs, strict, [r[...] for r in carry_refs], [r[...] for r in acc_refs],
                                   suffix)
        top = None
        for g in range(n_kv):
            carry_refs[g][...] = carries[g]
            acc_refs[g][...] = accs[g]
            part = _col_reduce(carries[g], jnp.max)
            top = part if top is None else jnp.maximum(top, part)
        return (jnp.max(top) < SB_DONE2).astype(I32)

    done = visit(b, col < q_row)
    lax.while_loop(lambda st: (st[0] >= 0) & (st[1] == 0), lambda st: (st[0] - 1, visit(st[0], None)), (b - 1, done))
    for g in range(n_kv):
        for r in range(rep):
            o_ref[:, head(g * rep + r)] = acc_refs[g][r * ROW_BLOCK:(r + 1) * ROW_BLOCK, :].astype(o_ref.dtype)


def _sb_prompt(q, k, v, n_rows):
    cq = q.shape[1]
    ckv = k.shape[1]
    n_kv = ckv // HEAD_DIM
    rep = cq // ckv
    kern = functools.partial(_sb_prompt_kernel, n_kv=n_kv, rep=rep)
    return pl.pallas_call(
        kern,
        out_shape=jax.ShapeDtypeStruct(q.shape, BF16),
        grid=(n_rows // ROW_BLOCK,),
        in_specs=[pl.BlockSpec((ROW_BLOCK, cq), lambda i: (i, 0)),
                  pl.BlockSpec((n_rows, ckv), lambda i: (0, 0)),
                  pl.BlockSpec((n_rows, ckv), lambda i: (0, 0))],
        out_specs=pl.BlockSpec((ROW_BLOCK, cq), lambda i: (i, 0)),
        scratch_shapes=[pltpu.VMEM((rep * ROW_BLOCK, LANES), F32)] * n_kv
                       + [pltpu.VMEM((rep * ROW_BLOCK, HEAD_DIM), F32)] * n_kv,
        compiler_params=_params("parallel"),
        name="stick_prompt",
    )(q, k, v)


def _page_head(ref, g, n_heads):
    keys = ref.shape[1] // n_heads
    return ref[0, pl.ds(g, keys, stride=n_heads), :].astype(BF16)


def _sb_sample_kernel(pt_ref, q_ref, *refs, n_kv, n_new, dec_seq, n_pages_step, has_init):
    del pt_ref
    refs = list(refs)
    new_refs = [refs.pop(0), refs.pop(0)] if n_new else None
    k_pages = [refs.pop(0) for _ in range(n_pages_step)]
    v_pages = [refs.pop(0) for _ in range(n_pages_step)]
    init_refs = [refs.pop(0), refs.pop(0)] if has_init else None
    o_ref, carry_ref = refs
    suffix = _sb_suffix_matrix()
    rows = q_ref.shape[2]

    @pl.when(pl.program_id(1) == 0)
    def _():
        if has_init:
            o_ref[...] = init_refs[0][...]
            carry_ref[...] = init_refs[1][...]
        else:
            o_ref[...] = jnp.zeros_like(o_ref)
            carry_ref[...] = jnp.zeros_like(carry_ref)

    def visit(k_of, v_of, strict):
        @pl.when(jnp.max(carry_ref[...]) >= SB_DONE2)
        def _():
            heads = range(n_kv)
            carries, accs = _sb_blocks([q_ref[0, g] for g in heads], [k_of(g) for g in heads],
                                       [v_of(g) for g in heads], strict, [carry_ref[0, g] for g in heads],
                                       [o_ref[0, g] for g in heads], suffix)
            for g in heads:
                carry_ref[0, g] = carries[g]
                o_ref[0, g] = accs[g]

    cols = lambda ref, g: ref[0, :, g * HEAD_DIM:(g + 1) * HEAD_DIM].astype(BF16)
    if n_new:
        q_idx = lax.broadcasted_iota(I32, (rows, LANES), 0) % dec_seq
        col = lax.broadcasted_iota(I32, (rows, LANES), 1)
        visit(lambda g: cols(new_refs[0], g), lambda g: cols(new_refs[1], g), col < q_idx)
    for kp, vp in zip(k_pages, v_pages):
        visit(lambda g, kp=kp: _page_head(kp, g, n_kv), lambda g, vp=vp: _page_head(vp, g, n_kv), None)


def _sb_sample_call(page_table, q, new_kv, cache_k, cache_v, init, dec_seq, first_page, n_steps, pages_per_step):
    db, n_kv, rows, _ = q.shape
    page_rows = cache_k.shape[1]
    assert page_rows == LANES * n_kv
    seq4 = lambda b, c, pt: (b, 0, 0, 0)
    in_specs = [pl.BlockSpec((1, n_kv, rows, HEAD_DIM), seq4)]
    args = [q]
    if new_kv is not None:
        in_specs += [pl.BlockSpec((1,) + new_kv[0].shape[1:], lambda b, c, pt: (b, 0, 0))] * 2
        args += list(new_kv)
    for cache in (cache_k, cache_v):
        for i in range(pages_per_step):
            in_specs.append(pl.BlockSpec(
                (1, page_rows, HEAD_DIM), lambda b, c, pt, i=i: (pt[b, first_page - c * pages_per_step - i], 0, 0)))
            args.append(cache)
    if init is not None:
        in_specs += [pl.BlockSpec((1, n_kv, rows, HEAD_DIM), seq4)] * 2
        args += list(init)
    kern = functools.partial(_sb_sample_kernel, n_kv=n_kv, n_new=new_kv is not None, dec_seq=dec_seq,
                             n_pages_step=pages_per_step, has_init=init is not None)
    out_sds = jax.ShapeDtypeStruct((db, n_kv, rows, HEAD_DIM), F32)
    return pl.pallas_call(
        kern,
        out_shape=(out_sds, out_sds),
        grid_spec=pltpu.PrefetchScalarGridSpec(
            num_scalar_prefetch=1,
            grid=(db, n_steps),
            in_specs=in_specs,
            out_specs=(pl.BlockSpec((1, n_kv, rows, HEAD_DIM), seq4), pl.BlockSpec((1, n_kv, rows, HEAD_DIM), seq4)),
        ),
        compiler_params=_params("parallel", "arbitrary"),
        name="stick_sample",
    )(page_table, *args)


def _sortable(score):
    bits = lax.bitcast_convert_type(score + 0.0, I32)
    key = bits ^ (lax.shift_right_arithmetic(bits, 31) & 0x7FFFFFFF)
    return jnp.maximum(key, INT_MIN + 1)


COUNT_ROWS = 64


def _count(key_ref, n_units, unit, pred):
    rows = COUNT_ROWS if unit % COUNT_ROWS == 0 else SUBLANES

    def body(u, acc):
        r0 = pl.multiple_of(u * unit, unit)
        hit = pred(key_ref[pl.ds(r0, unit), :], r0)
        ones = jnp.where(hit, jnp.ones(hit.shape, I32), jnp.zeros(hit.shape, I32))
        return acc + jnp.sum(ones.reshape(unit // rows, rows, LANES), axis=0)
    acc = lax.fori_loop(0, n_units, body, jnp.zeros((rows, LANES), I32))
    return jnp.sum(acc, axis=0, keepdims=True)


def _select_topk(key_ref, n_units, unit, k, n_rows):
    def bit_step(i, tau):
        cand = tau ^ lax.shift_left(jnp.int32(1), 31 - i)
        cnt = _count(key_ref, n_units, unit, lambda keys, r0: keys >= cand)
        return jnp.where(cnt >= k, cand, tau)
    tau = lax.fori_loop(0, 32, bit_step, jnp.full((1, LANES), INT_MIN, I32))
    c_gt = _count(key_ref, n_units, unit, lambda keys, r0: keys > tau)
    c_ge = _count(key_ref, n_units, unit, lambda keys, r0: keys >= tau)
    real = tau > INT_MIN
    tie = real & (c_ge > k)
    need = k - c_gt
    row = lax.broadcasted_iota(I32, (unit, LANES), 0)
    n_bits = max(1, (n_rows - 1).bit_length())

    def break_ties():
        def pos_step(i, x):
            cand = x | lax.shift_left(jnp.int32(1), n_bits - 1 - i)
            cnt = _count(key_ref, n_units, unit, lambda keys, r0: (keys == tau) & ((r0 + row) < cand))
            return jnp.where(cnt < need, cand, x)
        return lax.fori_loop(0, n_bits, pos_step, jnp.zeros((1, LANES), I32))

    last_tie = lax.cond(jnp.max(tie.astype(I32)) > 0, break_ties, lambda: jnp.zeros((1, LANES), I32))
    last = jnp.where(tie, last_tie, jnp.where(real, jnp.int32(2 ** 30), jnp.int32(-1)))
    return tau, last


def _col_reduce(x, fn):
    rows = x.shape[0]
    if rows > COUNT_ROWS and rows % COUNT_ROWS == 0:
        x = fn(x.reshape(rows // COUNT_ROWS, COUNT_ROWS, x.shape[1]), axis=0)
    return fn(x, axis=0, keepdims=True)


def _selected(keys, r0, tau, last):
    row = lax.broadcasted_iota(I32, keys.shape, 0)
    return (keys > tau) | ((keys == tau) & ((r0 + row) <= last))


IDX_SCALE = (IDX_HEADS * IDX_DIM) ** -0.5
MASKED = -1e30
EXP2_SCALE = HEAD_DIM ** -0.5 * math.log2(math.e)


def _dsa_prompt_kernel(aq_ref, iq_ref, misc_ref, ik_ref, k_ref, v_ref, o_ref, key_ref, *scratch,
                       unit, topk, n_valid, n_heads, n_kv, iw_row):
    b = pl.program_id(0)
    n_units = ((b + 1) * ROW_BLOCK + unit - 1) // unit
    rep = n_heads // n_kv
    row = lax.broadcasted_iota(I32, (unit, LANES), 0)
    q_pos = b * ROW_BLOCK + lax.broadcasted_iota(I32, (unit, LANES), 1)
    q_real = q_pos < n_valid

    iq_t = iq_ref[0]
    rhs = jnp.concatenate([iq_t[h * IDX_DIM:(h + 1) * IDX_DIM, :] for h in range(IDX_HEADS)], axis=1)
    w_t = misc_ref[0, iw_row:iw_row + IDX_HEADS, :]

    def score_unit(u, carry):
        r0 = pl.multiple_of(u * unit, unit)
        dots = _dot(ik_ref[pl.ds(r0, unit), :], rhs)
        score = jnp.zeros((unit, LANES), F32)
        for h in range(IDX_HEADS):
            score = score + jnp.maximum(dots[:, h * LANES:(h + 1) * LANES], 0.0) * w_t[h:h + 1, :]
        k_pos = r0 + row
        adm = (k_pos <= q_pos) & (q_real | (k_pos == 0))
        key_ref[pl.ds(r0, unit), :] = jnp.where(adm, _sortable(score * IDX_SCALE), INT_MIN)
        return carry

    lax.fori_loop(0, n_units, score_unit, 0)
    tau, last = _select_topk(key_ref, n_units, unit, topk, key_ref.shape[0])

    m_refs, l_refs, acc_refs, lg_refs = (scratch[i * n_kv:(i + 1) * n_kv] for i in range(4))
    for g in range(n_kv):
        m_refs[g][...] = jnp.full_like(m_refs[g], MASKED)
        l_refs[g][...] = jnp.zeros_like(l_refs[g])
        acc_refs[g][...] = jnp.zeros_like(acc_refs[g])
    aq_t = aq_ref[0]

    qgs = [jnp.concatenate([aq_t[(g * rep + r) * HEAD_DIM:(g * rep + r + 1) * HEAD_DIM, :] for r in range(rep)],
                           axis=1) for g in range(n_kv)]
    n_sub = unit // ROW_BLOCK
    fold = lambda x, fn: fn(x.reshape(x.shape[0] // SUBLANES, SUBLANES, x.shape[1]), axis=0)

    def logits_of_unit(u, slot):
        r0 = pl.multiple_of(u * unit, unit)
        sel = _selected(key_ref[pl.ds(r0, unit), :], r0, tau, last)
        tops = [None] * n_kv
        for j in range(n_sub):
            sub = slice(j * ROW_BLOCK, (j + 1) * ROW_BLOCK)
            for g in range(n_kv):
                lg = _dot(k_ref[pl.ds(r0 + j * ROW_BLOCK, ROW_BLOCK), g * HEAD_DIM:(g + 1) * HEAD_DIM], qgs[g])
                parts = []
                for r in range(rep):
                    x = jnp.where(sel[sub], lg[:, r * LANES:(r + 1) * LANES], MASKED)
                    lg_refs[g][slot, sub, r * LANES:(r + 1) * LANES] = x
                    parts.append(fold(x, jnp.max))
                part = jnp.concatenate(parts, axis=1)
                tops[g] = part if tops[g] is None else jnp.maximum(tops[g], part)
        return tuple(tops)

    def attend_unit(u, tops):
        r0 = pl.multiple_of(u * unit, unit)
        slot = u % 2
        next_tops = logits_of_unit(jnp.minimum(u + 1, n_units - 1), 1 - slot)
        m_old = [m_refs[g][...] for g in range(n_kv)]
        m_new = [jnp.maximum(m_old[g], jnp.max(tops[g], axis=0, keepdims=True)) for g in range(n_kv)]
        sums, pvs = [None] * n_kv, [None] * n_kv
        for j in range(n_sub):
            sub = slice(j * ROW_BLOCK, (j + 1) * ROW_BLOCK)
            for g in range(n_kv):
                p = jnp.exp2(lg_refs[g][slot, sub, :] - m_new[g])
                pv = _dot_tn(v_ref[pl.ds(r0 + j * ROW_BLOCK, ROW_BLOCK), g * HEAD_DIM:(g + 1) * HEAD_DIM], p.astype(BF16))
                part = fold(p, jnp.sum)
                sums[g] = part if sums[g] is None else sums[g] + part
                pvs[g] = pv if pvs[g] is None else pvs[g] + pv
        for g in range(n_kv):
            alpha = jnp.exp2(m_old[g] - m_new[g])
            m_refs[g][...] = m_new[g]
            l_refs[g][...] = alpha * l_refs[g][...] + jnp.sum(sums[g], axis=0, keepdims=True)
            acc_refs[g][...] = acc_refs[g][...] * alpha + pvs[g]
        return next_tops

    lax.fori_loop(0, n_units, attend_unit, logits_of_unit(0, 0))
    for g in range(n_kv):
        for r in range(rep):
            sl = slice(r * LANES, (r + 1) * LANES)
            out = acc_refs[g][:, sl] * (1.0 / l_refs[g][:, sl])
            h = g * rep + r
            o_ref[:, h * HEAD_DIM:(h + 1) * HEAD_DIM] = jnp.transpose(out).astype(o_ref.dtype)


def _dsa_prompt(aq_t, iq_t, misc_t, iw_row, ik, k, v, n_rows, n_valid, topk):
    n_blk = n_rows // ROW_BLOCK
    n_heads = aq_t.shape[1] // HEAD_DIM
    n_kv = k.shape[1] // HEAD_DIM
    rep = n_heads // n_kv
    unit = _row_tile(n_rows, 640)
    blk = lambda i: (i, 0, 0)
    const = lambda i: (0, 0)
    kern = functools.partial(_dsa_prompt_kernel, unit=unit, topk=topk, n_valid=n_valid, n_heads=n_heads, n_kv=n_kv,
                             iw_row=iw_row)
    return pl.pallas_call(
        kern,
        out_shape=jax.ShapeDtypeStruct((aq_t.shape[0] * ROW_BLOCK, n_heads * HEAD_DIM), BF16),
        grid=(n_blk,),
        in_specs=[pl.BlockSpec((1,) + aq_t.shape[1:], blk), pl.BlockSpec((1,) + iq_t.shape[1:], blk),
                  pl.BlockSpec((1,) + misc_t.shape[1:], blk),
                  pl.BlockSpec((n_rows, ik.shape[1]), const), pl.BlockSpec((n_rows, k.shape[1]), const),
                  pl.BlockSpec((n_rows, v.shape[1]), const)],
        out_specs=pl.BlockSpec((ROW_BLOCK, n_heads * HEAD_DIM), lambda i: (i, 0)),
        scratch_shapes=[pltpu.VMEM((n_rows, LANES), I32)]
                       + [pltpu.VMEM((1, rep * LANES), F32)] * (2 * n_kv)
                       + [pltpu.VMEM((HEAD_DIM, rep * LANES), F32)] * n_kv
                       + [pltpu.VMEM((2, unit, rep * LANES), F32)] * n_kv,
        compiler_params=_params("parallel"),
        name="dsa_prompt",
    )(aq_t, iq_t, misc_t, ik, k, v)


Q_PAD = SUBLANES


def _idx_sample_kernel(pt_ref, iq_ref, w_ref, new_ref, *refs, n_pages):
    del pt_ref
    pages, o_ref = refs[:n_pages], refs[n_pages]
    iq = iq_ref[0]
    w = w_ref[0]

    def score(keys_t):
        s = jnp.maximum(_dot(iq, keys_t), 0.0) * w
        return jnp.sum(s.reshape(IDX_HEADS, Q_PAD, LANES), axis=0) * IDX_SCALE

    for p in range(n_pages):
        o_ref[0, :, p * LANES:(p + 1) * LANES] = score(pages[p][0].astype(BF16))
    o_ref[0, :, n_pages * LANES:] = score(new_ref[0])


def _idx_sample(page_table, iq, w, new_ik_t, cache_idx_t):
    db = iq.shape[0]
    n_pages = page_table.shape[1]
    page = cache_idx_t.shape[2]
    assert page == LANES
    seq = lambda b, pt: (b, 0, 0)
    in_specs = [pl.BlockSpec((1,) + iq.shape[1:], seq), pl.BlockSpec((1,) + w.shape[1:], seq),
                pl.BlockSpec((1,) + new_ik_t.shape[1:], seq)]
    in_specs += [pl.BlockSpec((1, IDX_DIM, page), lambda b, pt, p=p: (pt[b, p], 0, 0)) for p in range(n_pages)]
    n_keys = (n_pages + 1) * LANES
    return pl.pallas_call(
        functools.partial(_idx_sample_kernel, n_pages=n_pages),
        out_shape=jax.ShapeDtypeStruct((db, Q_PAD, n_keys), F32),
        grid_spec=pltpu.PrefetchScalarGridSpec(
            num_scalar_prefetch=1, grid=(db,), in_specs=in_specs,
            out_specs=pl.BlockSpec((1, Q_PAD, n_keys), seq)),
        compiler_params=_params("parallel"),
        name="idx_sample",
    )(page_table, iq, w, new_ik_t, *([cache_idx_t] * n_pages))


def _topk_sample_kernel(s_ref, o_ref, key_ref, *, unit, topk, past_len, dec_seq):
    n_keys = s_ref.shape[0]
    n_units = n_keys // unit
    row = lax.broadcasted_iota(I32, (unit, LANES), 0)
    q = lax.broadcasted_iota(I32, (unit, LANES), 1) % Q_PAD
    for u in range(n_units):
        k_pos = u * unit + row
        adm = ((q < dec_seq) & (k_pos <= past_len + q)) | ((q >= dec_seq) & (k_pos == 0))
        key_ref[u * unit:(u + 1) * unit, :] = jnp.where(adm, _sortable(s_ref[u * unit:(u + 1) * unit, :]), INT_MIN)
    tau, last = _select_topk(key_ref, n_units, unit, topk, n_keys)
    for u in range(n_units):
        sel = _selected(key_ref[u * unit:(u + 1) * unit, :], u * unit, tau, last)
        o_ref[u * unit:(u + 1) * unit, :] = jnp.where(sel, 0.0, MASKED)


def _topk_sample(scores_t, topk, past_len, dec_seq):
    n_keys, n_q = scores_t.shape
    unit = _row_tile(n_keys, 640)
    kern = functools.partial(_topk_sample_kernel, unit=unit, topk=topk, past_len=past_len, dec_seq=dec_seq)
    return pl.pallas_call(
        kern,
        out_shape=jax.ShapeDtypeStruct((n_keys, n_q), F32),
        grid=(n_q // LANES,),
        in_specs=[pl.BlockSpec((n_keys, LANES), lambda i: (0, i))],
        out_specs=pl.BlockSpec((n_keys, LANES), lambda i: (0, i)),
        scratch_shapes=[pltpu.VMEM((n_keys, LANES), I32)],
        compiler_params=_params("parallel"),
        name="topk_sample",
    )(scores_t)


def _attn_sample_kernel(pt_ref, q_ref, bias_ref, bias_new_ref, k_new_ref, v_new_ref, *refs, n_kv, pages_per_step):
    del pt_ref
    k_pages = refs[:pages_per_step]
    v_pages = refs[pages_per_step:2 * pages_per_step]
    o_ref, m_ref, l_ref, acc_ref = refs[2 * pages_per_step:]
    c = pl.program_id(1)
    rep = q_ref.shape[2] // Q_PAD

    @pl.when(c == 0)
    def _():
        m_ref[...] = jnp.full_like(m_ref, MASKED)
        l_ref[...] = jnp.zeros_like(l_ref)
        acc_ref[...] = jnp.zeros_like(acc_ref)

    def update(g, kg, vg, bias):
        sel = jnp.concatenate([bias] * rep, axis=0) > 0.5 * MASKED
        lg = jnp.where(sel, _dot_nt(q_ref[0, g], kg), MASKED)
        m_old = m_ref[g]
        m_new = jnp.maximum(m_old, jnp.max(lg, axis=1, keepdims=True))
        alpha = jnp.exp2(m_old - m_new)
        p = jnp.where(sel, jnp.exp2(lg - m_new), 0.0)
        l_ref[g] = alpha * l_ref[g] + jnp.sum(p, axis=1, keepdims=True)
        acc_ref[g] = alpha * acc_ref[g] + _dot(p.astype(BF16), vg)
        m_ref[g] = m_new

    def head(ref, g):
        return ref[0, :, g * HEAD_DIM:(g + 1) * HEAD_DIM].astype(BF16)

    @pl.when(c == 0)
    def _():
        for g in range(n_kv):
            update(g, head(k_new_ref, g), head(v_new_ref, g), bias_new_ref[0])

    for g in range(n_kv):
        kg = jnp.concatenate([_page_head(r, g, n_kv) for r in k_pages], axis=0)
        vg = jnp.concatenate([_page_head(r, g, n_kv) for r in v_pages], axis=0)
        update(g, kg, vg, bias_ref[0])

    @pl.when(c == pl.num_programs(1) - 1)
    def _():
        for g in range(n_kv):
            o_ref[0, g] = acc_ref[g] * (1.0 / l_ref[g])


ATTN_PAGES_PER_STEP = 64


def _attn_sample(page_table, q, bias, new_k, new_v, cache_k, cache_v):
    db, n_kv, rows, _ = q.shape
    n_pages = page_table.shape[1]
    page_rows = cache_k.shape[1]
    page = page_rows // n_kv
    per_step = max(d for d in range(1, ATTN_PAGES_PER_STEP + 1) if n_pages % d == 0)
    seq4 = lambda b, c, pt: (b, 0, 0, 0)
    seq3 = lambda b, c, pt: (b, 0, 0)
    in_specs = [pl.BlockSpec((1, n_kv, rows, HEAD_DIM), seq4),
                pl.BlockSpec((1, Q_PAD, per_step * page), lambda b, c, pt: (b, 0, c)),
                pl.BlockSpec((1, Q_PAD, page), lambda b, c, pt: (b, 0, n_pages)),
                pl.BlockSpec((1,) + new_k.shape[1:], seq3), pl.BlockSpec((1,) + new_v.shape[1:], seq3)]
    for _ in range(2):
        in_specs += [pl.BlockSpec((1, page_rows, HEAD_DIM), lambda b, c, pt, i=i: (pt[b, c * per_step + i], 0, 0))
                     for i in range(per_step)]
    return pl.pallas_call(
        functools.partial(_attn_sample_kernel, n_kv=n_kv, pages_per_step=per_step),
        out_shape=jax.ShapeDtypeStruct((db, n_kv, rows, HEAD_DIM), F32),
        grid_spec=pltpu.PrefetchScalarGridSpec(
            num_scalar_prefetch=1, grid=(db, n_pages // per_step), in_specs=in_specs,
            out_specs=pl.BlockSpec((1, n_kv, rows, HEAD_DIM), seq4),
            scratch_shapes=[pltpu.VMEM((n_kv, rows, 1), F32), pltpu.VMEM((n_kv, rows, 1), F32),
                            pltpu.VMEM((n_kv, rows, HEAD_DIM), F32)]),
        compiler_params=_params("parallel", "arbitrary"),
        name="attn_sample",
    )(page_table, q, bias, bias, new_k, new_v, *([cache_k] * per_step), *([cache_v] * per_step))


SB_EAGER_PAGES = 4
SB_PAGES_PER_STEP = 12


def _sb_sample(page_table, q, new_k, new_v, cache_k, cache_v, dec_seq):
    n_pages = page_table.shape[1]
    eager = min(SB_EAGER_PAGES, n_pages)
    o, carry = _sb_sample_call(page_table, q, (new_k, new_v), cache_k, cache_v, None, dec_seq,
                               n_pages - 1, 1, eager)
    rest = n_pages - eager
    if rest == 0:
        return o
    per_step = max(d for d in range(1, SB_PAGES_PER_STEP + 1) if rest % d == 0)
    finish = lambda: _sb_sample_call(page_table, q, None, cache_k, cache_v, (o, carry), dec_seq,
                                     rest - 1, rest // per_step, per_step)[0]
    return lax.cond(jnp.max(carry) >= SB_DONE2, finish, lambda: o)


def _pad_rows(a, rows):
    return jnp.pad(a, ((0, 0), (0, rows - a.shape[1])) + ((0, 0),) * (a.ndim - 2))


def kernel(x_prompt, x_sample, cache_a_k, cache_a_v, cache_a_idx_k, state_b, cache_c_k, cache_c_v, page_table,
           meta_tokens, norm_mix, norm_ffn, norm_final, w_in0, w_b_gate2, b_b_gate, b_head_norm, w_out0,
           w_in1, w_out1, w_ff1, w_ff2):
    batch, seq, d = x_prompt.shape
    assert batch == 1
    db, ds, _ = x_sample.shape
    assert ds <= Q_PAD and ds <= GLA_SUB
    t_p = N_META + seq
    t_pad = _round_up(t_p, ROW_BLOCK)
    n_s = db * ds
    m_s = _round_up(n_s, ROW_BLOCK)
    n_blk = t_pad // ROW_BLOCK
    n_pool, page = cache_a_k.shape[:2]
    n_pages = page_table.shape[1]
    past_len = n_pages * page
    topk_p = min(TOPK_MAX, seq // 4)
    topk_s = min(TOPK_MAX, (past_len + ds) // 4)
    srows = slice(t_pad, t_pad + n_s)

    h = jnp.concatenate([meta_tokens.astype(F32), x_prompt[0], jnp.zeros((t_pad - t_p, d), F32),
                         x_sample.reshape(n_s, d), jnp.zeros((m_s - n_s, d), F32)], axis=0)

    lay, ref_off, total = _even_layout(d)
    w0 = _permute_w_in0(w_in0, lay, ref_off, total).astype(BF16)
    gate_w = _gate_weights(w_b_gate2, b_b_gate, lay)
    seg = lambda a, name: a[:, lay[name][0]:lay[name][0] + lay[name][1]]

    xn = _rmsnorm(h, norm_mix[0], BF16)
    p0 = _matmul(xn, w0, F32)
    aq_t, iq_t, misc_t, ak, av, ik, k_bf, v_bf, ik_bf = _rope(p0, lay, t_pad, past_len, ds)
    a_kv = k_bf.shape[1]
    n_heads_a = aq_t.shape[1] // HEAD_DIM
    rep_a = n_heads_a // A_KV_HEADS
    iw_row = lay["iw"][0] - lay["misc"][0]

    oa = _dsa_prompt(aq_t, iq_t, misc_t, iw_row, ik_bf, k_bf, v_bf, t_pad, t_p, topk_p)
    ob, state_p = _gla_prompt(p0, lay, t_pad, t_p, gate_w, b_head_norm)

    sample_rows = lambda a_t: a_t[n_blk:].transpose(0, 2, 1).reshape(m_s, a_t.shape[1])[:n_s]
    aq, iq = sample_rows(aq_t), sample_rows(iq_t)
    iw = sample_rows(misc_t[:, iw_row:iw_row + IDX_HEADS])
    iq_s = iq.reshape(db, ds, IDX_HEADS, IDX_DIM).transpose(0, 2, 1, 3)
    iq_s = jnp.pad(iq_s, ((0, 0), (0, 0), (0, Q_PAD - ds), (0, 0))).reshape(db, IDX_HEADS * Q_PAD, IDX_DIM)
    iw_s = iw.reshape(db, ds, IDX_HEADS).transpose(0, 2, 1)
    iw_s = jnp.pad(iw_s, ((0, 0), (0, 0), (0, Q_PAD - ds))).reshape(db, IDX_HEADS * Q_PAD, 1)
    iw_s = jnp.broadcast_to(iw_s, (db, IDX_HEADS * Q_PAD, LANES))
    new_ik_t = _pad_rows(ik_bf[srows].reshape(db, ds, IDX_DIM), page).transpose(0, 2, 1)
    scores = _idx_sample(page_table, iq_s, iw_s, new_ik_t, cache_a_idx_k.transpose(0, 2, 1))
    n_keys = scores.shape[2]
    n_q = _round_up(db * Q_PAD, LANES)
    scores_t = jnp.pad(scores.reshape(db * Q_PAD, n_keys).T, ((0, 0), (0, n_q - db * Q_PAD)))
    bias = _topk_sample(scores_t, topk_s, past_len, ds)[:, :db * Q_PAD].T.reshape(db, Q_PAD, n_keys)
    aq_s = aq.reshape(db, ds, A_KV_HEADS, rep_a, HEAD_DIM).transpose(0, 2, 3, 1, 4)
    aq_s = jnp.pad(aq_s, ((0, 0), (0, 0), (0, 0), (0, Q_PAD - ds), (0, 0))).reshape(db, A_KV_HEADS, rep_a * Q_PAD, HEAD_DIM)
    new_k = _pad_rows(k_bf[srows].reshape(db, ds, a_kv), page)
    new_v = _pad_rows(v_bf[srows].reshape(db, ds, a_kv), page)
    oa_s = _attn_sample(page_table, aq_s, bias, new_k, new_v,
                        cache_a_k.reshape(n_pool, page * A_KV_HEADS, HEAD_DIM),
                        cache_a_v.reshape(n_pool, page * A_KV_HEADS, HEAD_DIM))
    oa_s = oa_s.reshape(db, A_KV_HEADS, rep_a, Q_PAD, HEAD_DIM)[:, :, :, :ds].transpose(0, 3, 1, 2, 4)
    oa_s = oa_s.reshape(n_s, n_heads_a * HEAD_DIM).astype(BF16)

    chunk_s = lambda name: _pad_rows(seg(p0, name)[srows].reshape(db, ds, lay[name][1]), GLA_SUB)
    ob_s, state_s = _gla_sample(chunk_s("bq"), chunk_s("bk"), chunk_s("bv"), chunk_s("br"), chunk_s("misc"),
                                state_b, ds, gate_w, b_head_norm)
    ob_s = ob_s[:, :ds].reshape(n_s, -1)

    oa = oa.at[srows].set(oa_s)
    ob = ob.at[srows].set(ob_s)
    h, xn = _proj_res_norm([oa, ob], w_out0.astype(BF16), h, norm_ffn[0])
    w_ff1_bf, w_ff2_bf = w_ff1.astype(BF16), w_ff2.astype(BF16)
    h, xn = _mlp(xn, h, w_ff1_bf, w_ff2_bf, 0, norm_mix[1], BF16)

    c_kv = C_KV_HEADS * HEAD_DIM
    c_q = w_in1.shape[1] - 2 * c_kv
    rep_c = c_q // c_kv
    q1 = _matmul(xn, w_in1[:, :c_q].astype(BF16), BF16, scale=SB_Q_SCALE)
    ck, cv, ck_bf, cv_bf = _kv_proj(xn, w_in1[:, c_q:].astype(BF16), C_KV_HEADS)
    oc = _sb_prompt(q1, ck_bf, cv_bf, t_pad)
    q1_s = q1[srows].reshape(db, ds, C_KV_HEADS, rep_c, HEAD_DIM).transpose(0, 2, 3, 1, 4)
    q1_s = q1_s.reshape(db, C_KV_HEADS, rep_c * ds, HEAD_DIM)
    new_ck = _pad_rows(ck_bf[srows].reshape(db, ds, c_kv), page)
    new_cv = _pad_rows(cv_bf[srows].reshape(db, ds, c_kv), page)
    oc_s = _sb_sample(page_table, q1_s, new_ck, new_cv, cache_c_k.reshape(n_pool, page * C_KV_HEADS, HEAD_DIM),
                      cache_c_v.reshape(n_pool, page * C_KV_HEADS, HEAD_DIM), ds)
    oc_s = oc_s.reshape(db, C_KV_HEADS, rep_c, ds, HEAD_DIM).transpose(0, 3, 1, 2, 4).reshape(n_s, c_q).astype(BF16)
    h, xn = _proj_res_norm([oc.at[srows].set(oc_s)], w_out1.astype(BF16), h, norm_ffn[1])
    h, y = _mlp(xn, h, w_ff1_bf, w_ff2_bf, 1, norm_final, F32)

    y_prompt = y[N_META:t_p][None]
    y_sample = y[srows].reshape(db, ds, d)
    kv_p = lambda a, heads: a[:t_p * heads].reshape(1, t_p, heads, HEAD_DIM)
    kv_s = lambda a, heads: a[t_pad * heads:(t_pad + n_s) * heads].reshape(db, ds, heads, HEAD_DIM)
    return (y_prompt, y_sample,
            kv_p(ak, A_KV_HEADS), kv_p(av, A_KV_HEADS), ik[:t_p, :IDX_DIM][None], state_p[None],
            kv_p(ck, C_KV_HEADS), kv_p(cv, C_KV_HEADS),
            kv_s(ak, A_KV_HEADS), kv_s(av, A_KV_HEADS), ik[srows, :IDX_DIM].reshape(db, ds, IDX_DIM),
            state_s, kv_s(ck, C_KV_HEADS), kv_s(cv, C_KV_HEADS))
```

```python
import functools
import math

import jax
import jax.numpy as jnp
from jax import lax
from jax.experimental import pallas as pl
from jax.experimental.pallas import tpu as pltpu

F32 = jnp.float32
BF16 = jnp.bfloat16
I32 = jnp.int32
I16 = jnp.int16

N_META = 16
HEAD_DIM = 128
ROPE_THETA = 10000.0
EPS = 1e-6
A_KV_HEADS = 2
IDX_HEADS = 16
IDX_DIM = 64
TOPK_MAX = 256
B_HEADS = 4
B_GATE_RANK = 16
B_GATE_TAU = 16.0
C_KV_HEADS = 4

LANES = 128
SUBLANES = 8
VMEM_LIMIT_BYTES = 56 * 1024 * 1024

ROW_BLOCK = 128
INT_MIN = -(2 ** 31)


def _round_up(x, m):
    return (x + m - 1) // m * m


def _row_tile(m, cap):
    best = ROW_BLOCK
    for k in range(1, m // ROW_BLOCK + 1):
        t = k * ROW_BLOCK
        if t <= cap and m % t == 0:
            best = t
    return best


def _col_tile(n, cap):
    best = LANES
    for k in range(1, n // LANES + 1):
        t = k * LANES
        if t <= cap and n % t == 0:
            best = t
    return best


def _params(*sem):
    return pltpu.CompilerParams(dimension_semantics=sem, vmem_limit_bytes=VMEM_LIMIT_BYTES)


def _rms(x, g):
    ms = jnp.mean(x * x, axis=-1, keepdims=True)
    return x * lax.rsqrt(ms + EPS) * g


def _rmsnorm_kernel(x_ref, g_ref, o_ref):
    o_ref[...] = _rms(x_ref[...], g_ref[...]).astype(o_ref.dtype)


def _rmsnorm(x, g, out_dtype):
    m, d = x.shape
    tm = _row_tile(m, 512)
    return pl.pallas_call(
        _rmsnorm_kernel,
        out_shape=jax.ShapeDtypeStruct((m, d), out_dtype),
        grid=(m // tm,),
        in_specs=[pl.BlockSpec((tm, d), lambda i: (i, 0)), pl.BlockSpec((1, d), lambda i: (0, 0))],
        out_specs=pl.BlockSpec((tm, d), lambda i: (i, 0)),
        compiler_params=_params("parallel"),
        name="rmsnorm",
    )(x, g.reshape(1, d))


def _matmul_kernel(x_ref, w_ref, o_ref, *, scale):
    acc = jnp.dot(x_ref[...], w_ref[...], preferred_element_type=F32)
    o_ref[...] = (acc if scale is None else acc * scale).astype(o_ref.dtype)


def _matmul(x, w, out_dtype, scale=None):
    m, k = x.shape
    n = w.shape[1]
    tm = _row_tile(m, 1408)
    tn = _col_tile(n, 1152)
    return pl.pallas_call(
        functools.partial(_matmul_kernel, scale=scale),
        out_shape=jax.ShapeDtypeStruct((m, n), out_dtype),
        grid=(m // tm, n // tn),
        in_specs=[pl.BlockSpec((tm, k), lambda i, j: (i, 0)), pl.BlockSpec((k, tn), lambda i, j: (0, j))],
        out_specs=pl.BlockSpec((tm, tn), lambda i, j: (i, j)),
        compiler_params=_params("parallel", "parallel"),
        name="matmul",
    )(x, w)


def _kv_proj_kernel(x_ref, w_ref, k_ref, v_ref, kbf_ref, vbf_ref, *, n_heads):
    acc = jnp.dot(x_ref[...], w_ref[...], preferred_element_type=F32)
    tm = x_ref.shape[0]
    ckv = n_heads * HEAD_DIM
    for h in range(n_heads):
        k_ref[pl.ds(h, tm, stride=n_heads), :] = acc[:, h * HEAD_DIM:(h + 1) * HEAD_DIM]
        v_ref[pl.ds(h, tm, stride=n_heads), :] = acc[:, ckv + h * HEAD_DIM:ckv + (h + 1) * HEAD_DIM]
    kbf_ref[...] = acc[:, :ckv].astype(BF16)
    vbf_ref[...] = acc[:, ckv:].astype(BF16)


def _kv_proj(x, w, n_heads):
    m, k = x.shape
    ckv = n_heads * HEAD_DIM
    assert w.shape == (k, 2 * ckv)
    tm = _row_tile(m, 768)
    row = lambda i: (i, 0)
    return pl.pallas_call(
        functools.partial(_kv_proj_kernel, n_heads=n_heads),
        out_shape=(jax.ShapeDtypeStruct((m * n_heads, HEAD_DIM), F32), jax.ShapeDtypeStruct((m * n_heads, HEAD_DIM), F32),
                   jax.ShapeDtypeStruct((m, ckv), BF16), jax.ShapeDtypeStruct((m, ckv), BF16)),
        grid=(m // tm,),
        in_specs=[pl.BlockSpec((tm, k), row), pl.BlockSpec((k, 2 * ckv), lambda i: (0, 0))],
        out_specs=(pl.BlockSpec((tm * n_heads, HEAD_DIM), row), pl.BlockSpec((tm * n_heads, HEAD_DIM), row),
                   pl.BlockSpec((tm, ckv), row), pl.BlockSpec((tm, ckv), row)),
        compiler_params=_params("parallel"),
        name="kv_proj",
    )(x, w)


def _proj_res_norm_kernel(*refs, n_in):
    x_refs, w_refs = refs[:n_in], refs[n_in:2 * n_in]
    h_ref, g_ref, hout_ref, xn_ref = refs[2 * n_in:]
    hn = h_ref[...]
    for x_ref, w_ref in zip(x_refs, w_refs):
        hn = hn + jnp.dot(x_ref[...], w_ref[...], preferred_element_type=F32)
    hout_ref[...] = hn
    xn_ref[...] = _rms(hn, g_ref[...]).astype(xn_ref.dtype)


def _proj_res_norm(xs, w, h, g):
    m, k = xs[0].shape
    assert all(x.shape == (m, k) for x in xs) and w.shape[0] == k * len(xs)
    d = w.shape[1]
    tm = _row_tile(m, 384)
    row = lambda i: (i, 0)
    in_specs = [pl.BlockSpec((tm, k), row) for _ in xs]
    in_specs += [pl.BlockSpec((k, d), lambda i, j=j: (j, 0)) for j in range(len(xs))]
    in_specs += [pl.BlockSpec((tm, d), row), pl.BlockSpec((1, d), lambda i: (0, 0))]
    return pl.pallas_call(
        functools.partial(_proj_res_norm_kernel, n_in=len(xs)),
        out_shape=(jax.ShapeDtypeStruct((m, d), F32), jax.ShapeDtypeStruct((m, d), BF16)),
        grid=(m // tm,),
        in_specs=in_specs,
        out_specs=(pl.BlockSpec((tm, d), row), pl.BlockSpec((tm, d), row)),
        compiler_params=_params("parallel"),
        name="proj_res_norm",
    )(*xs, *([w] * len(xs)), h, g.reshape(1, d))


def _mlp_kernel(xn_ref, h_ref, w1_ref, w2_ref, g_ref, hout_ref, xnout_ref):
    f = pl.program_id(1)

    @pl.when(f == 0)
    def _():
        hout_ref[...] = h_ref[...]

    a = jnp.maximum(jnp.dot(xn_ref[...], w1_ref[...], preferred_element_type=F32), 0.0)
    hout_ref[...] += jnp.dot((a * a).astype(BF16), w2_ref[...], preferred_element_type=F32)

    @pl.when(f == pl.num_programs(1) - 1)
    def _():
        xnout_ref[...] = _rms(hout_ref[...], g_ref[...]).astype(xnout_ref.dtype)


def _mlp(xn, h, w1, w2, layer, g, norm_dtype):
    m, d = xn.shape
    dff = w1.shape[2]
    tm = _row_tile(m, 768)
    tf = _col_tile(dff, 1024)
    return pl.pallas_call(
        _mlp_kernel,
        out_shape=(jax.ShapeDtypeStruct((m, d), F32), jax.ShapeDtypeStruct((m, d), norm_dtype)),
        grid=(m // tm, dff // tf),
        in_specs=[pl.BlockSpec((tm, d), lambda i, f: (i, 0)),
                  pl.BlockSpec((tm, d), lambda i, f: (i, 0), pipeline_mode=pl.Buffered(1)),
                  pl.BlockSpec((None, d, tf), lambda i, f: (layer, 0, f)),
                  pl.BlockSpec((None, tf, d), lambda i, f: (layer, f, 0)),
                  pl.BlockSpec((1, d), lambda i, f: (0, 0))],
        out_specs=(pl.BlockSpec((tm, d), lambda i, f: (i, 0)), pl.BlockSpec((tm, d), lambda i, f: (i, 0))),
        compiler_params=_params("parallel", "arbitrary"),
        name="mlp",
    )(xn, h, w1, w2, g.reshape(1, d))


def _even_layout(d_model):
    a_q = d_model // 2
    a_kv = A_KV_HEADS * HEAD_DIM
    idx_q = IDX_HEADS * IDX_DIM
    b_v = d_model // 2
    b_qk = b_v // 2
    ref_sizes = dict(aq=a_q, ak=a_kv, av=a_kv, iq=idx_q, ik=IDX_DIM, iw=IDX_HEADS, bq=b_qk, bk=b_qk, bv=b_v,
                     bglr=B_GATE_RANK, br=b_v)
    ref_off, off = {}, 0
    for name, size in ref_sizes.items():
        ref_off[name] = off
        off += size
    lay, off = {}, 0
    for name in ("aq", "iq", "bq", "bk", "bv", "br", "ak", "av", "ik", "iw", "bglr"):
        lay[name] = (off, ref_sizes[name])
        off += ref_sizes[name]
    lay["misc"] = (lay["ik"][0], LANES)
    total = lay["misc"][0] + LANES
    assert off <= total
    return lay, ref_off, total


def _permute_w_in0(w_in0, lay, ref_off, total):
    parts, end = [], 0
    for name, (off, size) in lay.items():
        if name != "misc":
            assert off == end
            parts.append(w_in0[:, ref_off[name]:ref_off[name] + size])
            end = off + size
    parts.append(jnp.zeros((w_in0.shape[0], total - end), w_in0.dtype))
    return jnp.concatenate(parts, axis=1)


def _rope_freqs():
    def inv(d):
        return ROPE_THETA ** (-jnp.arange(d // 2, dtype=F32) * (2.0 / d))
    pad = jnp.zeros((LANES - HEAD_DIM // 2 - IDX_DIM // 2,), F32)
    return jnp.concatenate([inv(HEAD_DIM), inv(IDX_DIM), pad]).reshape(1, LANES)


def _rope_kernel(q_ref, kv_ref, inv_ref, aq_ref, iq_ref, misc_ref, ak_ref, av_ref, ik_ref, kbf_ref, vbf_ref, ikbf_ref,
                 *, n_aq, n_iq, n_ak, n_prompt_rows, past_len, dec_seq):
    tr = q_ref.shape[0]
    half, quarter = HEAD_DIM // 2, IDX_DIM // 2
    row = pl.program_id(0) * tr + lax.broadcasted_iota(I32, (tr, 1), 0)
    pos = jnp.where(row < n_prompt_rows, row, past_len + (row - n_prompt_rows) % dec_seq).astype(F32)
    ang = pos * inv_ref[...]
    c, s = jnp.cos(ang), jnp.sin(ang)
    cos128 = jnp.concatenate([c[:, :half]] * 2, axis=1)
    sin128 = jnp.concatenate([-s[:, :half], s[:, :half]], axis=1)
    c_i, s_i = c[:, half:half + quarter], s[:, half:half + quarter]
    cos64 = jnp.concatenate([c_i] * (LANES // quarter), axis=1)
    sin64 = jnp.concatenate([-s_i, s_i] * (LANES // IDX_DIM), axis=1)

    def rope128(x):
        return x * cos128 + pltpu.roll(x, half, 1) * sin128

    lane = lax.broadcasted_iota(I32, (tr, LANES), 1)
    low_half = (lane % IDX_DIM) < quarter

    def rope64(x):
        rot = jnp.where(low_half, pltpu.roll(x, LANES - quarter, 1), pltpu.roll(x, quarter, 1))
        return x * cos64 + rot * sin64

    n_blk = tr // ROW_BLOCK

    def put_transposed(ref, c, x):
        for b in range(n_blk):
            tile = jnp.transpose(x[b * ROW_BLOCK:(b + 1) * ROW_BLOCK, :])
            ref[b, c * LANES:(c + 1) * LANES, :] = tile.astype(ref.dtype)

    for c in range(n_aq):
        sl = slice(c * LANES, (c + 1) * LANES)
        put_transposed(aq_ref, c, rope128(q_ref[:, sl]) * EXP2_SCALE)
    for c in range(n_iq):
        put_transposed(iq_ref, c, rope64(q_ref[:, n_aq * LANES + c * LANES:n_aq * LANES + (c + 1) * LANES]))
    put_transposed(misc_ref, 0, kv_ref[:, 2 * n_ak * LANES:(2 * n_ak + 1) * LANES])
    for c in range(n_ak):
        sl = slice(c * LANES, (c + 1) * LANES)
        kr = rope128(kv_ref[:, sl])
        vr = kv_ref[:, n_ak * LANES + c * LANES:n_ak * LANES + (c + 1) * LANES]
        ak_ref[pl.ds(c, tr, stride=n_ak), :] = kr
        av_ref[pl.ds(c, tr, stride=n_ak), :] = vr
        kbf_ref[:, sl] = kr.astype(BF16)
        vbf_ref[:, sl] = vr.astype(BF16)
    ik = rope64(kv_ref[:, 2 * n_ak * LANES:(2 * n_ak + 1) * LANES])
    ik_ref[...] = ik
    ikbf_ref[...] = ik[:, :IDX_DIM].astype(BF16)


def _rope(p0, lay, n_prompt_rows, past_len, dec_seq):
    m = p0.shape[0]
    tr = _row_tile(m, 384)
    a_q, a_kv, idx_q = lay["aq"][1], lay["ak"][1], lay["iq"][1]
    qw = a_q + idx_q
    kvw = 2 * a_kv + LANES
    assert lay["aq"][0] == 0 and lay["iq"][0] == a_q and lay["ak"][0] % kvw == 0
    n_ak = a_kv // LANES
    kern = functools.partial(_rope_kernel, n_aq=a_q // LANES, n_iq=idx_q // LANES, n_ak=n_ak,
                             n_prompt_rows=n_prompt_rows, past_len=past_len, dec_seq=dec_seq)
    row = lambda i: (i, 0)
    blk = lambda i: (i, 0, 0)
    nb, tb = m // ROW_BLOCK, tr // ROW_BLOCK
    return pl.pallas_call(
        kern,
        out_shape=(jax.ShapeDtypeStruct((nb, a_q, ROW_BLOCK), BF16), jax.ShapeDtypeStruct((nb, idx_q, ROW_BLOCK), BF16),
                   jax.ShapeDtypeStruct((nb, LANES, ROW_BLOCK), F32),
                   jax.ShapeDtypeStruct((m * n_ak, LANES), F32), jax.ShapeDtypeStruct((m * n_ak, LANES), F32),
                   jax.ShapeDtypeStruct((m, LANES), F32),
                   jax.ShapeDtypeStruct((m, a_kv), BF16), jax.ShapeDtypeStruct((m, a_kv), BF16),
                   jax.ShapeDtypeStruct((m, IDX_DIM), BF16)),
        grid=(m // tr,),
        in_specs=[pl.BlockSpec((tr, qw), row),
                  pl.BlockSpec((tr, kvw), lambda i: (i, lay["ak"][0] // kvw)),
                  pl.BlockSpec((1, LANES), lambda i: (0, 0))],
        out_specs=(pl.BlockSpec((tb, a_q, ROW_BLOCK), blk), pl.BlockSpec((tb, idx_q, ROW_BLOCK), blk),
                   pl.BlockSpec((tb, LANES, ROW_BLOCK), blk),
                   pl.BlockSpec((tr * n_ak, LANES), row), pl.BlockSpec((tr * n_ak, LANES), row),
                   pl.BlockSpec((tr, LANES), row), pl.BlockSpec((tr, a_kv), row), pl.BlockSpec((tr, a_kv), row),
                   pl.BlockSpec((tr, IDX_DIM), row)),
        compiler_params=_params("parallel"),
        name="rope",
    )(p0, p0, _rope_freqs())


GLA_SUB = 16


def _split_bf16(x):
    hi = x.astype(BF16)
    return hi, (x - hi.astype(F32)).astype(BF16)


def _log_sigmoid(x):
    return jnp.minimum(x, 0.0) - jnp.log(1.0 + jnp.exp(-jnp.abs(x)))


def _dot(a, b):
    return jnp.dot(a, b, preferred_element_type=F32)


def _dot_nt(a, b):
    return lax.dot_general(a, b, (((1,), (1,)), ((), ())), preferred_element_type=F32)


def _dot_tn(a, b):
    return lax.dot_general(a, b, (((0,), (0,)), ((), ())), preferred_element_type=F32)


def _gla_gate(misc, w2_hi, w2_lo, bias, valid):
    m_hi, m_lo = _split_bf16(misc)
    x = _dot(m_hi, w2_hi) + _dot(m_lo, w2_hi) + _dot(m_hi, w2_lo) + bias
    return jnp.where(valid, _log_sigmoid(x) * (1.0 / B_GATE_TAU), 0.0)


def _gla_chunk(q, k, v, g, state_ref, dk, dv):
    c = q.shape[0]
    n_sub = c // GLA_SUB
    ri = lax.broadcasted_iota(I32, (c, c), 0)
    ci = lax.broadcasted_iota(I32, (c, c), 1)
    causal = ri >= ci
    tri = jnp.where(causal, 1.0, 0.0).astype(BF16)
    gcum = _dot(jnp.concatenate([tri, tri], axis=1), jnp.concatenate(_split_bf16(g), axis=0))
    scale = dk ** -0.5
    pad = _round_up(c, LANES)
    outs = []
    for h in range(B_HEADS):
        gh = gcum[:, h * dk:(h + 1) * dk]
        qh = q[:, h * dk:(h + 1) * dk] * scale
        kh = k[:, h * dk:(h + 1) * dk]
        vh = v[:, h * dv:(h + 1) * dv].astype(BF16)
        q_parts, k_parts = [], []
        for i in range(n_sub):
            lo, hi = i * GLA_SUB, (i + 1) * GLA_SUB
            base = gh[lo - 1:lo, :] if i else jnp.zeros((1, dk), F32)
            q_parts.append(qh[lo:hi] * jnp.exp(gh[lo:hi] - base))
            k_parts.append(kh[:hi] * jnp.exp(base - gh[:hi]))
            if pad > hi:
                k_parts.append(jnp.zeros((pad - hi, dk), F32))
        qt = jnp.concatenate(q_parts, axis=0).astype(BF16)
        kt = jnp.concatenate(k_parts, axis=0).astype(BF16)
        r = _dot_nt(qt, kt)
        att = jnp.concatenate([r[i * GLA_SUB:(i + 1) * GLA_SUB, i * pad:i * pad + c] for i in range(n_sub)], axis=0)
        att = jnp.where(causal, att, 0.0).astype(BF16)
        s = state_ref[h]
        o = _dot(att, vh) + _dot((qh * jnp.exp(gh)).astype(BF16), s.astype(BF16))
        last = gh[c - 1:c, :]
        k_tail = (kh * jnp.exp(last - gh)).astype(BF16)
        decay = jnp.transpose(jnp.broadcast_to(jnp.exp(last), (dk, dk)))
        decay = jnp.concatenate([decay] * (dv // dk), axis=1)
        state_ref[h] = decay * s + _dot_tn(k_tail, vh)
        outs.append(o)
    return outs


def _gla_out(outs, r, head_norm, dv):
    res = []
    for h, o in enumerate(outs):
        gate = r[:, h * dv:(h + 1) * dv]
        gate = gate * (1.0 / (1.0 + jnp.exp(-gate)))
        res.append(_rms(o, head_norm) * gate)
    return jnp.concatenate(res, axis=1)


def _gla_prompt_kernel(q_ref, k_ref, v_ref, r_ref, misc_ref, w2h_ref, w2l_ref, b_ref, hn_ref, o_ref, state_ref,
                       *, chunk, n_valid, dk, dv):
    blk = pl.program_id(0)

    @pl.when(blk == 0)
    def _():
        state_ref[...] = jnp.zeros_like(state_ref)

    rows = q_ref.shape[0]
    for c0 in range(0, rows, chunk):
        sl = slice(c0, c0 + chunk)
        row = blk * rows + c0 + lax.broadcasted_iota(I32, (chunk, 1), 0)
        g = _gla_gate(misc_ref[sl, :], w2h_ref[...], w2l_ref[...], b_ref[...], row < n_valid)
        outs = _gla_chunk(q_ref[sl, :], k_ref[sl, :], v_ref[sl, :], g, state_ref, dk, dv)
        o_ref[sl, :] = _gla_out(outs, r_ref[sl, :], hn_ref[...], dv).astype(o_ref.dtype)


def _gla_sample_kernel(q_ref, k_ref, v_ref, r_ref, misc_ref, w2h_ref, w2l_ref, b_ref, hn_ref, s_ref, o_ref,
                       state_ref, *, n_valid, dk, dv):
    state_ref[0] = s_ref[0]
    rows = q_ref.shape[1]
    row = lax.broadcasted_iota(I32, (rows, 1), 0)
    g = _gla_gate(misc_ref[0], w2h_ref[...], w2l_ref[...], b_ref[...], row < n_valid)
    outs = _gla_chunk(q_ref[0], k_ref[0], v_ref[0], g, state_ref.at[0], dk, dv)
    o_ref[0] = _gla_out(outs, r_ref[0], hn_ref[...], dv).astype(o_ref.dtype)


def _gate_weights(w_b_gate2, b_b_gate, lay):
    off = lay["bglr"][0] - lay["misc"][0]
    w2 = jnp.zeros((LANES, w_b_gate2.shape[1]), F32).at[off:off + B_GATE_RANK].set(w_b_gate2)
    hi = w2.astype(BF16)
    lo = (w2 - hi.astype(F32)).astype(BF16)
    return hi, lo, b_b_gate.reshape(1, -1)


def _gla_prompt(p0, lay, n_rows, n_valid, gate_w, head_norm, chunk=64):
    bqk, bv = lay["bq"][1], lay["bv"][1]
    dk, dv = bqk // B_HEADS, bv // B_HEADS
    w2h, w2l, bias = gate_w
    col = lambda name: (lambda i, o=lay[name][0] // lay[name][1]: (i, o))
    const = lambda i: (0, 0)
    kern = functools.partial(_gla_prompt_kernel, chunk=chunk, n_valid=n_valid, dk=dk, dv=dv)
    return pl.pallas_call(
        kern,
        out_shape=(jax.ShapeDtypeStruct((p0.shape[0], bv), BF16), jax.ShapeDtypeStruct((B_HEADS, dk, dv), F32)),
        grid=(n_rows // ROW_BLOCK,),
        in_specs=[pl.BlockSpec((ROW_BLOCK, bqk), col("bq")), pl.BlockSpec((ROW_BLOCK, bqk), col("bk")),
                  pl.BlockSpec((ROW_BLOCK, bv), col("bv")), pl.BlockSpec((ROW_BLOCK, bv), col("br")),
                  pl.BlockSpec((ROW_BLOCK, LANES), col("misc")),
                  pl.BlockSpec(w2h.shape, const), pl.BlockSpec(w2l.shape, const), pl.BlockSpec(bias.shape, const),
                  pl.BlockSpec((1, dv), const)],
        out_specs=(pl.BlockSpec((ROW_BLOCK, bv), lambda i: (i, 0)),
                   pl.BlockSpec((B_HEADS, dk, dv), lambda i: (0, 0, 0))),
        compiler_params=_params("arbitrary"),
        name="gla_prompt",
    )(p0, p0, p0, p0, p0, w2h, w2l, bias, head_norm.reshape(1, dv))


def _gla_sample(q, k, v, r, misc, state, n_valid, gate_w, head_norm):
    db, rows, bqk = q.shape
    bv = v.shape[2]
    dk, dv = bqk // B_HEADS, bv // B_HEADS
    w2h, w2l, bias = gate_w
    seq = lambda i: (i, 0, 0)
    const = lambda i: (0, 0)
    kern = functools.partial(_gla_sample_kernel, n_valid=n_valid, dk=dk, dv=dv)
    return pl.pallas_call(
        kern,
        out_shape=(jax.ShapeDtypeStruct((db, rows, bv), BF16), jax.ShapeDtypeStruct(state.shape, F32)),
        grid=(db,),
        in_specs=[pl.BlockSpec((1, rows, bqk), seq), pl.BlockSpec((1, rows, bqk), seq),
                  pl.BlockSpec((1, rows, bv), seq), pl.BlockSpec((1, rows, bv), seq),
                  pl.BlockSpec((1, rows, LANES), seq),
                  pl.BlockSpec(w2h.shape, const), pl.BlockSpec(w2l.shape, const), pl.BlockSpec(bias.shape, const),
                  pl.BlockSpec((1, dv), const),
                  pl.BlockSpec((1, B_HEADS, dk, dv), lambda i: (i, 0, 0, 0))],
        out_specs=(pl.BlockSpec((1, rows, bv), seq), pl.BlockSpec((1, B_HEADS, dk, dv), lambda i: (i, 0, 0, 0))),
        compiler_params=_params("parallel"),
        name="gla_sample",
    )(q, k, v, r, misc, w2h, w2l, bias, head_norm.reshape(1, dv), state)


SB_DONE = -120.0
SB_Q_SCALE = HEAD_DIM ** -0.5 * math.log2(math.e)
SB_DONE2 = SB_DONE * math.log2(math.e)


def _sb_suffix_matrix():
    j = lax.broadcasted_iota(I32, (2 * LANES, 2 * LANES), 0) % LANES
    s = lax.broadcasted_iota(I32, (2 * LANES, 2 * LANES), 1)
    return jnp.where((j > s) | (s >= LANES), 1.0, 0.0).astype(BF16)


def _sb_blocks(qs, ks, vs, strict, carries, accs, suffix):
    zs = [_dot_nt(q, k) for q, k in zip(qs, ks)]
    keeps, takes = [], []
    for z in zs:
        soft = jnp.log2(1.0 + jnp.exp2(-jnp.abs(z)))
        log_keep = -(jnp.maximum(z, 0.0) + soft)
        keeps.append(log_keep if strict is None else jnp.where(strict, log_keep, 0.0))
        takes.append(jnp.minimum(z, 0.0) - soft)
    sums = []
    for lk in keeps:
        sums.append(_dot(jnp.concatenate(_split_bf16(lk), axis=1), suffix))
    ws = []
    for take, s, c in zip(takes, sums, carries):
        w = jnp.exp2(take + (s[:, :LANES] + c))
        ws.append((w if strict is None else jnp.where(strict, w, 0.0)).astype(BF16))
    new_carries = [c + s[:, LANES:] for c, s in zip(carries, sums)]
    new_accs = [a + _dot(w, v) for a, w, v in zip(accs, ws, vs)]
    return new_carries, new_accs


def _sb_prompt_kernel(q_ref, k_ref, v_ref, o_ref, *scratch, n_kv, rep):
    carry_refs, acc_refs = scratch[:n_kv], scratch[n_kv:]
    b = pl.program_id(0)
    suffix = _sb_suffix_matrix()
    for ref in scratch:
        ref[...] = jnp.zeros_like(ref)
    rows = rep * ROW_BLOCK
    q_row = lax.broadcasted_iota(I32, (rows, LANES), 0) % ROW_BLOCK
    col = lax.broadcasted_iota(I32, (rows, LANES), 1)
    head = lambda g: slice(g * HEAD_DIM, (g + 1) * HEAD_DIM)

    def visit(kb, strict):
        k0 = pl.multiple_of(kb * ROW_BLOCK, ROW_BLOCK)
        qs = [jnp.concatenate([q_ref[:, head(g * rep + r)] for r in range(rep)], axis=0) for g in range(n_kv)]
        ks = [k_ref[pl.ds(k0, ROW_BLOCK), head(g)] for g in range(n_kv)]
        vs = [v_ref[pl.ds(k0, ROW_BLOCK), head(g)] for g in range(n_kv)]
        carries, accs = _sb_blocks(qs, ks, vs, strict, [r[...] for r in carry_refs], [r[...] for r in acc_refs],
                                   suffix)
        top = None
        for g in range(n_kv):
            carry_refs[g][...] = carries[g]
            acc_refs[g][...] = accs[g]
            part = _col_reduce(carries[g], jnp.max)
            top = part if top is None else jnp.maximum(top, part)
        return (jnp.max(top) < SB_DONE2).astype(I32)

    done = visit(b, col < q_row)
    lax.while_loop(lambda st: (st[0] >= 0) & (st[1] == 0), lambda st: (st[0] - 1, visit(st[0], None)), (b - 1, done))
    for g in range(n_kv):
        for r in range(rep):
            o_ref[:, head(g * rep + r)] = acc_refs[g][r * ROW_BLOCK:(r + 1) * ROW_BLOCK, :].astype(o_ref.dtype)


def _sb_prompt(q, k, v, n_rows):
    cq = q.shape[1]
    ckv = k.shape[1]
    n_kv = ckv // HEAD_DIM
    rep = cq // ckv
    kern = functools.partial(_sb_prompt_kernel, n_kv=n_kv, rep=rep)
    return pl.pallas_call(
        kern,
        out_shape=jax.ShapeDtypeStruct(q.shape, BF16),
        grid=(n_rows // ROW_BLOCK,),
        in_specs=[pl.BlockSpec((ROW_BLOCK, cq), lambda i: (i, 0)),
                  pl.BlockSpec((n_rows, ckv), lambda i: (0, 0)),
                  pl.BlockSpec((n_rows, ckv), lambda i: (0, 0))],
        out_specs=pl.BlockSpec((ROW_BLOCK, cq), lambda i: (i, 0)),
        scratch_shapes=[pltpu.VMEM((rep * ROW_BLOCK, LANES), F32)] * n_kv
                       + [pltpu.VMEM((rep * ROW_BLOCK, HEAD_DIM), F32)] * n_kv,
        compiler_params=_params("parallel"),
        name="stick_prompt",
    )(q, k, v)


def _page_head(ref, g, n_heads):
    keys = ref.shape[1] // n_heads
    return ref[0, pl.ds(g, keys, stride=n_heads), :].astype(BF16)


def _sb_sample_kernel(pt_ref, q_ref, *refs, n_kv, n_new, dec_seq, n_pages_step, has_init):
    del pt_ref
    refs = list(refs)
    new_refs = [refs.pop(0), refs.pop(0)] if n_new else None
    k_pages = [refs.pop(0) for _ in range(n_pages_step)]
    v_pages = [refs.pop(0) for _ in range(n_pages_step)]
    init_refs = [refs.pop(0), refs.pop(0)] if has_init else None
    o_ref, carry_ref = refs
    suffix = _sb_suffix_matrix()
    rows = q_ref.shape[2]

    @pl.when(pl.program_id(1) == 0)
    def _():
        if has_init:
            o_ref[...] = init_refs[0][...]
            carry_ref[...] = init_refs[1][...]
        else:
            o_ref[...] = jnp.zeros_like(o_ref)
            carry_ref[...] = jnp.zeros_like(carry_ref)

    def visit(k_of, v_of, strict):
        @pl.when(jnp.max(carry_ref[...]) >= SB_DONE2)
        def _():
            heads = range(n_kv)
            carries, accs = _sb_blocks([q_ref[0, g] for g in heads], [k_of(g) for g in heads],
                                       [v_of(g) for g in heads], strict, [carry_ref[0, g] for g in heads],
                                       [o_ref[0, g] for g in heads], suffix)
            for g in heads:
                carry_ref[0, g] = carries[g]
                o_ref[0, g] = accs[g]

    cols = lambda ref, g: ref[0, :, g * HEAD_DIM:(g + 1) * HEAD_DIM].astype(BF16)
    if n_new:
        q_idx = lax.broadcasted_iota(I32, (rows, LANES), 0) % dec_seq
        col = lax.broadcasted_iota(I32, (rows, LANES), 1)
        visit(lambda g: cols(new_refs[0], g), lambda g: cols(new_refs[1], g), col < q_idx)
    for kp, vp in zip(k_pages, v_pages):
        visit(lambda g, kp=kp: _page_head(kp, g, n_kv), lambda g, vp=vp: _page_head(vp, g, n_kv), None)


def _sb_sample_call(page_table, q, new_kv, cache_k, cache_v, init, dec_seq, first_page, n_steps, pages_per_step):
    db, n_kv, rows, _ = q.shape
    page_rows = cache_k.shape[1]
    assert page_rows == LANES * n_kv
    seq4 = lambda b, c, pt: (b, 0, 0, 0)
    in_specs = [pl.BlockSpec((1, n_kv, rows, HEAD_DIM), seq4)]
    args = [q]
    if new_kv is not None:
        in_specs += [pl.BlockSpec((1,) + new_kv[0].shape[1:], lambda b, c, pt: (b, 0, 0))] * 2
        args += list(new_kv)
    for cache in (cache_k, cache_v):
        for i in range(pages_per_step):
            in_specs.append(pl.BlockSpec(
                (1, page_rows, HEAD_DIM), lambda b, c, pt, i=i: (pt[b, first_page - c * pages_per_step - i], 0, 0)))
            args.append(cache)
    if init is not None:
        in_specs += [pl.BlockSpec((1, n_kv, rows, HEAD_DIM), seq4)] * 2
        args += list(init)
    kern = functools.partial(_sb_sample_kernel, n_kv=n_kv, n_new=new_kv is not None, dec_seq=dec_seq,
                             n_pages_step=pages_per_step, has_init=init is not None)
    out_sds = jax.ShapeDtypeStruct((db, n_kv, rows, HEAD_DIM), F32)
    return pl.pallas_call(
        kern,
        out_shape=(out_sds, out_sds),
        grid_spec=pltpu.PrefetchScalarGridSpec(
            num_scalar_prefetch=1,
            grid=(db, n_steps),
            in_specs=in_specs,
            out_specs=(pl.BlockSpec((1, n_kv, rows, HEAD_DIM), seq4), pl.BlockSpec((1, n_kv, rows, HEAD_DIM), seq4)),
        ),
        compiler_params=_params("parallel", "arbitrary"),
        name="stick_sample",
    )(page_table, *args)


def _sortable(score):
    bits = lax.bitcast_convert_type(score + 0.0, I32)
    key = bits ^ (lax.shift_right_arithmetic(bits, 31) & 0x7FFFFFFF)
    return jnp.maximum(key, INT_MIN + 1)


COUNT_ROWS = 64


def _count(key_ref, n_units, unit, pred):
    rows = COUNT_ROWS if unit % COUNT_ROWS == 0 else SUBLANES

    def body(u, acc):
        r0 = pl.multiple_of(u * unit, unit)
        hit = pred(key_ref[pl.ds(r0, unit), :], r0)
        ones = jnp.where(hit, jnp.ones(hit.shape, I32), jnp.zeros(hit.shape, I32))
        return acc + jnp.sum(ones.reshape(unit // rows, rows, LANES), axis=0)
    acc = lax.fori_loop(0, n_units, body, jnp.zeros((rows, LANES), I32))
    return jnp.sum(acc, axis=0, keepdims=True)


HALF_BITS = 16
HALF_BIAS = 1 << (HALF_BITS - 1)


def _put_keys(key_ref, hi_ref, rows, keys):
    key_ref[rows, :] = keys
    hi_ref[rows, :] = lax.shift_right_arithmetic(keys, HALF_BITS).astype(I16)


def _count16(ref16, n_units, unit, pred):
    rows = LANES if unit % LANES == 0 else 2 * SUBLANES

    def body(u, acc):
        r0 = pl.multiple_of(u * unit, unit)
        x = ref16[pl.ds(r0, unit), :]
        ones = jnp.where(pred(x), jnp.ones(x.shape, I16), jnp.zeros(x.shape, I16))
        for c in range(unit // rows):
            acc = acc + ones[c * rows:(c + 1) * rows]
        return acc
    acc = lax.fori_loop(0, n_units, body, jnp.zeros((rows, LANES), I16))
    return jnp.sum(acc.astype(I32), axis=0, keepdims=True)


def _kth_largest(key_ref, hi_ref, lo_ref, n_units, unit, k):
    def search(ref16, base):
        def step(i, t):
            cand = t | lax.shift_left(jnp.int32(1), HALF_BITS - 1 - i)
            cand16 = (cand - HALF_BIAS).astype(I16)
            cnt = base + _count16(ref16, n_units, unit, lambda x: x >= cand16)
            return jnp.where(cnt >= k, cand, t)
        return lax.fori_loop(0, HALF_BITS, step, jnp.zeros((1, LANES), I32))

    tau_hi = search(hi_ref, 0) - HALF_BIAS
    tau_hi16 = tau_hi.astype(I16)
    above = _count16(hi_ref, n_units, unit, lambda x: x > tau_hi16)

    def residual(u, carry):
        r0 = pl.multiple_of(u * unit, unit)
        keys = key_ref[pl.ds(r0, unit), :]
        low = (keys & (2 * HALF_BIAS - 1)) - HALF_BIAS
        same = lax.shift_right_arithmetic(keys, HALF_BITS) == tau_hi
        lo_ref[pl.ds(r0, unit), :] = jnp.where(same, low, -HALF_BIAS).astype(I16)
        return carry
    lax.fori_loop(0, n_units, residual, 0)
    return tau_hi * (2 * HALF_BIAS) + search(lo_ref, above)


def _select_topk(key_ref, hi_ref, lo_ref, n_units, unit, k, n_rows):
    tau = _kth_largest(key_ref, hi_ref, lo_ref, n_units, unit, k)
    c_gt = _count(key_ref, n_units, unit, lambda keys, r0: keys > tau)
    c_ge = _count(key_ref, n_units, unit, lambda keys, r0: keys >= tau)
    real = tau > INT_MIN
    tie = real & (c_ge > k)
    need = k - c_gt
    row = lax.broadcasted_iota(I32, (unit, LANES), 0)
    n_bits = max(1, (n_rows - 1).bit_length())

    def break_ties():
        def pos_step(i, x):
            cand = x | lax.shift_left(jnp.int32(1), n_bits - 1 - i)
            cnt = _count(key_ref, n_units, unit, lambda keys, r0: (keys == tau) & ((r0 + row) < cand))
            return jnp.where(cnt < need, cand, x)
        return lax.fori_loop(0, n_bits, pos_step, jnp.zeros((1, LANES), I32))

    last_tie = lax.cond(jnp.max(tie.astype(I32)) > 0, break_ties, lambda: jnp.zeros((1, LANES), I32))
    last = jnp.where(tie, last_tie, jnp.where(real, jnp.int32(2 ** 30), jnp.int32(-1)))
    return tau, last


def _col_reduce(x, fn):
    rows = x.shape[0]
    if rows > COUNT_ROWS and rows % COUNT_ROWS == 0:
        x = fn(x.reshape(rows // COUNT_ROWS, COUNT_ROWS, x.shape[1]), axis=0)
    return fn(x, axis=0, keepdims=True)


def _selected(keys, r0, tau, last):
    row = lax.broadcasted_iota(I32, keys.shape, 0)
    return (keys > tau) | ((keys == tau) & ((r0 + row) <= last))


IDX_SCALE = (IDX_HEADS * IDX_DIM) ** -0.5
MASKED = -1e30
EXP2_SCALE = HEAD_DIM ** -0.5 * math.log2(math.e)


def _dsa_prompt_kernel(aq_ref, iq_ref, misc_ref, ik_ref, k_ref, v_ref, o_ref, key_ref, hi_ref, lo_ref, *scratch,
                       unit, topk, n_valid, n_heads, n_kv, iw_row):
    b = pl.program_id(0)
    n_units = ((b + 1) * ROW_BLOCK + unit - 1) // unit
    rep = n_heads // n_kv
    row = lax.broadcasted_iota(I32, (unit, LANES), 0)
    q_pos = b * ROW_BLOCK + lax.broadcasted_iota(I32, (unit, LANES), 1)
    q_real = q_pos < n_valid

    iq_t = iq_ref[0]
    rhs = jnp.concatenate([iq_t[h * IDX_DIM:(h + 1) * IDX_DIM, :] for h in range(IDX_HEADS)], axis=1)
    w_t = misc_ref[0, iw_row:iw_row + IDX_HEADS, :]

    def score_unit(u, carry):
        r0 = pl.multiple_of(u * unit, unit)
        dots = _dot(ik_ref[pl.ds(r0, unit), :], rhs)
        score = jnp.zeros((unit, LANES), F32)
        for h in range(IDX_HEADS):
            score = score + jnp.maximum(dots[:, h * LANES:(h + 1) * LANES], 0.0) * w_t[h:h + 1, :]
        k_pos = r0 + row
        adm = (k_pos <= q_pos) & (q_real | (k_pos == 0))
        _put_keys(key_ref, hi_ref, pl.ds(r0, unit), jnp.where(adm, _sortable(score * IDX_SCALE), INT_MIN))
        return carry

    lax.fori_loop(0, n_units, score_unit, 0)
    tau, last = _select_topk(key_ref, hi_ref, lo_ref, n_units, unit, topk, key_ref.shape[0])

    m_refs, l_refs, acc_refs, lg_refs = (scratch[i * n_kv:(i + 1) * n_kv] for i in range(4))
    for g in range(n_kv):
        m_refs[g][...] = jnp.full_like(m_refs[g], MASKED)
        l_refs[g][...] = jnp.zeros_like(l_refs[g])
        acc_refs[g][...] = jnp.zeros_like(acc_refs[g])
    aq_t = aq_ref[0]

    qgs = [jnp.concatenate([aq_t[(g * rep + r) * HEAD_DIM:(g * rep + r + 1) * HEAD_DIM, :] for r in range(rep)],
                           axis=1) for g in range(n_kv)]
    n_sub = unit // ROW_BLOCK
    fold = lambda x, fn: fn(x.reshape(x.shape[0] // SUBLANES, SUBLANES, x.shape[1]), axis=0)

    def logits_of_unit(u, slot):
        r0 = pl.multiple_of(u * unit, unit)
        sel = _selected(key_ref[pl.ds(r0, unit), :], r0, tau, last)
        tops = [None] * n_kv
        for j in range(n_sub):
            sub = slice(j * ROW_BLOCK, (j + 1) * ROW_BLOCK)
            for g in range(n_kv):
                lg = _dot(k_ref[pl.ds(r0 + j * ROW_BLOCK, ROW_BLOCK), g * HEAD_DIM:(g + 1) * HEAD_DIM], qgs[g])
                parts = []
                for r in range(rep):
                    x = jnp.where(sel[sub], lg[:, r * LANES:(r + 1) * LANES], MASKED)
                    lg_refs[g][slot, sub, r * LANES:(r + 1) * LANES] = x
                    parts.append(fold(x, jnp.max))
                part = jnp.concatenate(parts, axis=1)
                tops[g] = part if tops[g] is None else jnp.maximum(tops[g], part)
        return tuple(tops)

    def attend_unit(u, tops):
        r0 = pl.multiple_of(u * unit, unit)
        slot = u % 2
        next_tops = logits_of_unit(jnp.minimum(u + 1, n_units - 1), 1 - slot)
        m_old = [m_refs[g][...] for g in range(n_kv)]
        m_new = [jnp.maximum(m_old[g], jnp.max(tops[g], axis=0, keepdims=True)) for g in range(n_kv)]
        sums, pvs = [None] * n_kv, [None] * n_kv
        for j in range(n_sub):
            sub = slice(j * ROW_BLOCK, (j + 1) * ROW_BLOCK)
            for g in range(n_kv):
                p = jnp.exp2(lg_refs[g][slot, sub, :] - m_new[g])
                pv = _dot_tn(v_ref[pl.ds(r0 + j * ROW_BLOCK, ROW_BLOCK), g * HEAD_DIM:(g + 1) * HEAD_DIM], p.astype(BF16))
                part = fold(p, jnp.sum)
                sums[g] = part if sums[g] is None else sums[g] + part
                pvs[g] = pv if pvs[g] is None else pvs[g] + pv
        for g in range(n_kv):
            alpha = jnp.exp2(m_old[g] - m_new[g])
            m_refs[g][...] = m_new[g]
            l_refs[g][...] = alpha * l_refs[g][...] + jnp.sum(sums[g], axis=0, keepdims=True)
            acc_refs[g][...] = acc_refs[g][...] * alpha + pvs[g]
        return next_tops

    lax.fori_loop(0, n_units, attend_unit, logits_of_unit(0, 0))
    for g in range(n_kv):
        for r in range(rep):
            sl = slice(r * LANES, (r + 1) * LANES)
            out = acc_refs[g][:, sl] * (1.0 / l_refs[g][:, sl])
            h = g * rep + r
            o_ref[:, h * HEAD_DIM:(h + 1) * HEAD_DIM] = jnp.transpose(out).astype(o_ref.dtype)


def _dsa_prompt(aq_t, iq_t, misc_t, iw_row, ik, k, v, n_rows, n_valid, topk):
    n_blk = n_rows // ROW_BLOCK
    n_heads = aq_t.shape[1] // HEAD_DIM
    n_kv = k.shape[1] // HEAD_DIM
    rep = n_heads // n_kv
    unit = _row_tile(n_rows, 640)
    blk = lambda i: (i, 0, 0)
    const = lambda i: (0, 0)
    kern = functools.partial(_dsa_prompt_kernel, unit=unit, topk=topk, n_valid=n_valid, n_heads=n_heads, n_kv=n_kv,
                             iw_row=iw_row)
    return pl.pallas_call(
        kern,
        out_shape=jax.ShapeDtypeStruct((aq_t.shape[0] * ROW_BLOCK, n_heads * HEAD_DIM), BF16),
        grid=(n_blk,),
        in_specs=[pl.BlockSpec((1,) + aq_t.shape[1:], blk), pl.BlockSpec((1,) + iq_t.shape[1:], blk),
                  pl.BlockSpec((1,) + misc_t.shape[1:], blk),
                  pl.BlockSpec((n_rows, ik.shape[1]), const), pl.BlockSpec((n_rows, k.shape[1]), const),
                  pl.BlockSpec((n_rows, v.shape[1]), const)],
        out_specs=pl.BlockSpec((ROW_BLOCK, n_heads * HEAD_DIM), lambda i: (i, 0)),
        scratch_shapes=[pltpu.VMEM((n_rows, LANES), I32), pltpu.VMEM((n_rows, LANES), I16),
                        pltpu.VMEM((n_rows, LANES), I16)]
                       + [pltpu.VMEM((1, rep * LANES), F32)] * (2 * n_kv)
                       + [pltpu.VMEM((HEAD_DIM, rep * LANES), F32)] * n_kv
                       + [pltpu.VMEM((2, unit, rep * LANES), F32)] * n_kv,
        compiler_params=_params("parallel"),
        name="dsa_prompt",
    )(aq_t, iq_t, misc_t, ik, k, v)


Q_PAD = SUBLANES


def _idx_sample_kernel(pt_ref, iq_ref, w_ref, new_ref, *refs, n_pages):
    del pt_ref
    pages, o_ref = refs[:n_pages], refs[n_pages]
    iq = iq_ref[0]
    w = w_ref[0]

    def score(keys_t):
        s = jnp.maximum(_dot(iq, keys_t), 0.0) * w
        return jnp.sum(s.reshape(IDX_HEADS, Q_PAD, LANES), axis=0) * IDX_SCALE

    for p in range(n_pages):
        o_ref[0, :, p * LANES:(p + 1) * LANES] = score(pages[p][0].astype(BF16))
    o_ref[0, :, n_pages * LANES:] = score(new_ref[0])


def _idx_sample(page_table, iq, w, new_ik_t, cache_idx_t):
    db = iq.shape[0]
    n_pages = page_table.shape[1]
    page = cache_idx_t.shape[2]
    assert page == LANES
    seq = lambda b, pt: (b, 0, 0)
    in_specs = [pl.BlockSpec((1,) + iq.shape[1:], seq), pl.BlockSpec((1,) + w.shape[1:], seq),
                pl.BlockSpec((1,) + new_ik_t.shape[1:], seq)]
    in_specs += [pl.BlockSpec((1, IDX_DIM, page), lambda b, pt, p=p: (pt[b, p], 0, 0)) for p in range(n_pages)]
    n_keys = (n_pages + 1) * LANES
    return pl.pallas_call(
        functools.partial(_idx_sample_kernel, n_pages=n_pages),
        out_shape=jax.ShapeDtypeStruct((db, Q_PAD, n_keys), F32),
        grid_spec=pltpu.PrefetchScalarGridSpec(
            num_scalar_prefetch=1, grid=(db,), in_specs=in_specs,
            out_specs=pl.BlockSpec((1, Q_PAD, n_keys), seq)),
        compiler_params=_params("parallel"),
        name="idx_sample",
    )(page_table, iq, w, new_ik_t, *([cache_idx_t] * n_pages))


def _topk_sample_kernel(s_ref, o_ref, key_ref, hi_ref, lo_ref, *, unit, topk, past_len, dec_seq):
    n_keys = s_ref.shape[0]
    n_units = n_keys // unit
    row = lax.broadcasted_iota(I32, (unit, LANES), 0)
    q = lax.broadcasted_iota(I32, (unit, LANES), 1) % Q_PAD
    for u in range(n_units):
        k_pos = u * unit + row
        adm = ((q < dec_seq) & (k_pos <= past_len + q)) | ((q >= dec_seq) & (k_pos == 0))
        rows = slice(u * unit, (u + 1) * unit)
        _put_keys(key_ref, hi_ref, rows, jnp.where(adm, _sortable(s_ref[rows, :]), INT_MIN))
    tau, last = _select_topk(key_ref, hi_ref, lo_ref, n_units, unit, topk, n_keys)
    for u in range(n_units):
        sel = _selected(key_ref[u * unit:(u + 1) * unit, :], u * unit, tau, last)
        o_ref[u * unit:(u + 1) * unit, :] = jnp.where(sel, 0.0, MASKED)


def _topk_sample(scores_t, topk, past_len, dec_seq):
    n_keys, n_q = scores_t.shape
    unit = _row_tile(n_keys, 640)
    kern = functools.partial(_topk_sample_kernel, unit=unit, topk=topk, past_len=past_len, dec_seq=dec_seq)
    return pl.pallas_call(
        kern,
        out_shape=jax.ShapeDtypeStruct((n_keys, n_q), F32),
        grid=(n_q // LANES,),
        in_specs=[pl.BlockSpec((n_keys, LANES), lambda i: (0, i))],
        out_specs=pl.BlockSpec((n_keys, LANES), lambda i: (0, i)),
        scratch_shapes=[pltpu.VMEM((n_keys, LANES), I32), pltpu.VMEM((n_keys, LANES), I16),
                        pltpu.VMEM((n_keys, LANES), I16)],
        compiler_params=_params("parallel"),
        name="topk_sample",
    )(scores_t)


def _attn_sample_kernel(pt_ref, q_ref, bias_ref, bias_new_ref, k_new_ref, v_new_ref, *refs, n_kv, pages_per_step):
    del pt_ref
    k_pages = refs[:pages_per_step]
    v_pages = refs[pages_per_step:2 * pages_per_step]
    o_ref, m_ref, l_ref, acc_ref = refs[2 * pages_per_step:]
    c = pl.program_id(1)
    rep = q_ref.shape[2] // Q_PAD

    @pl.when(c == 0)
    def _():
        m_ref[...] = jnp.full_like(m_ref, MASKED)
        l_ref[...] = jnp.zeros_like(l_ref)
        acc_ref[...] = jnp.zeros_like(acc_ref)

    def update(g, kg, vg, bias):
        sel = jnp.concatenate([bias] * rep, axis=0) > 0.5 * MASKED
        lg = jnp.where(sel, _dot_nt(q_ref[0, g], kg), MASKED)
        m_old = m_ref[g]
        m_new = jnp.maximum(m_old, jnp.max(lg, axis=1, keepdims=True))
        alpha = jnp.exp2(m_old - m_new)
        p = jnp.where(sel, jnp.exp2(lg - m_new), 0.0)
        l_ref[g] = alpha * l_ref[g] + jnp.sum(p, axis=1, keepdims=True)
        acc_ref[g] = alpha * acc_ref[g] + _dot(p.astype(BF16), vg)
        m_ref[g] = m_new

    def head(ref, g):
        return ref[0, :, g * HEAD_DIM:(g + 1) * HEAD_DIM].astype(BF16)

    @pl.when(c == 0)
    def _():
        for g in range(n_kv):
            update(g, head(k_new_ref, g), head(v_new_ref, g), bias_new_ref[0])

    for g in range(n_kv):
        kg = jnp.concatenate([_page_head(r, g, n_kv) for r in k_pages], axis=0)
        vg = jnp.concatenate([_page_head(r, g, n_kv) for r in v_pages], axis=0)
        update(g, kg, vg, bias_ref[0])

    @pl.when(c == pl.num_programs(1) - 1)
    def _():
        for g in range(n_kv):
            o_ref[0, g] = acc_ref[g] * (1.0 / l_ref[g])


ATTN_PAGES_PER_STEP = 64


def _attn_sample(page_table, q, bias, new_k, new_v, cache_k, cache_v):
    db, n_kv, rows, _ = q.shape
    n_pages = page_table.shape[1]
    page_rows = cache_k.shape[1]
    page = page_rows // n_kv
    per_step = max(d for d in range(1, ATTN_PAGES_PER_STEP + 1) if n_pages % d == 0)
    seq4 = lambda b, c, pt: (b, 0, 0, 0)
    seq3 = lambda b, c, pt: (b, 0, 0)
    in_specs = [pl.BlockSpec((1, n_kv, rows, HEAD_DIM), seq4),
                pl.BlockSpec((1, Q_PAD, per_step * page), lambda b, c, pt: (b, 0, c)),
                pl.BlockSpec((1, Q_PAD, page), lambda b, c, pt: (b, 0, n_pages)),
                pl.BlockSpec((1,) + new_k.shape[1:], seq3), pl.BlockSpec((1,) + new_v.shape[1:], seq3)]
    for _ in range(2):
        in_specs += [pl.BlockSpec((1, page_rows, HEAD_DIM), lambda b, c, pt, i=i: (pt[b, c * per_step + i], 0, 0))
                     for i in range(per_step)]
    return pl.pallas_call(
        functools.partial(_attn_sample_kernel, n_kv=n_kv, pages_per_step=per_step),
        out_shape=jax.ShapeDtypeStruct((db, n_kv, rows, HEAD_DIM), F32),
        grid_spec=pltpu.PrefetchScalarGridSpec(
            num_scalar_prefetch=1, grid=(db, n_pages // per_step), in_specs=in_specs,
            out_specs=pl.BlockSpec((1, n_kv, rows, HEAD_DIM), seq4),
            scratch_shapes=[pltpu.VMEM((n_kv, rows, 1), F32), pltpu.VMEM((n_kv, rows, 1), F32),
                            pltpu.VMEM((n_kv, rows, HEAD_DIM), F32)]),
        compiler_params=_params("parallel", "arbitrary"),
        name="attn_sample",
    )(page_table, q, bias, bias, new_k, new_v, *([cache_k] * per_step), *([cache_v] * per_step))


SB_EAGER_PAGES = 4
SB_PAGES_PER_STEP = 12


def _sb_sample(page_table, q, new_k, new_v, cache_k, cache_v, dec_seq):
    n_pages = page_table.shape[1]
    eager = min(SB_EAGER_PAGES, n_pages)
    o, carry = _sb_sample_call(page_table, q, (new_k, new_v), cache_k, cache_v, None, dec_seq,
                               n_pages - 1, 1, eager)
    rest = n_pages - eager
    if rest == 0:
        return o
    per_step = max(d for d in range(1, SB_PAGES_PER_STEP + 1) if rest % d == 0)
    finish = lambda: _sb_sample_call(page_table, q, None, cache_k, cache_v, (o, carry), dec_seq,
                                     rest - 1, rest // per_step, per_step)[0]
    return lax.cond(jnp.max(carry) >= SB_DONE2, finish, lambda: o)


def _pad_rows(a, rows):
    return jnp.pad(a, ((0, 0), (0, rows - a.shape[1])) + ((0, 0),) * (a.ndim - 2))


def kernel(x_prompt, x_sample, cache_a_k, cache_a_v, cache_a_idx_k, state_b, cache_c_k, cache_c_v, page_table,
           meta_tokens, norm_mix, norm_ffn, norm_final, w_in0, w_b_gate2, b_b_gate, b_head_norm, w_out0,
           w_in1, w_out1, w_ff1, w_ff2):
    batch, seq, d = x_prompt.shape
    assert batch == 1
    db, ds, _ = x_sample.shape
    assert ds <= Q_PAD and ds <= GLA_SUB
    t_p = N_META + seq
    t_pad = _round_up(t_p, ROW_BLOCK)
    n_s = db * ds
    m_s = _round_up(n_s, ROW_BLOCK)
    n_blk = t_pad // ROW_BLOCK
    n_pool, page = cache_a_k.shape[:2]
    n_pages = page_table.shape[1]
    past_len = n_pages * page
    topk_p = min(TOPK_MAX, seq // 4)
    topk_s = min(TOPK_MAX, (past_len + ds) // 4)
    srows = slice(t_pad, t_pad + n_s)

    h = jnp.concatenate([meta_tokens.astype(F32), x_prompt[0], jnp.zeros((t_pad - t_p, d), F32),
                         x_sample.reshape(n_s, d), jnp.zeros((m_s - n_s, d), F32)], axis=0)

    lay, ref_off, total = _even_layout(d)
    w0 = _permute_w_in0(w_in0, lay, ref_off, total).astype(BF16)
    gate_w = _gate_weights(w_b_gate2, b_b_gate, lay)
    seg = lambda a, name: a[:, lay[name][0]:lay[name][0] + lay[name][1]]

    xn = _rmsnorm(h, norm_mix[0], BF16)
    p0 = _matmul(xn, w0, F32)
    aq_t, iq_t, misc_t, ak, av, ik, k_bf, v_bf, ik_bf = _rope(p0, lay, t_pad, past_len, ds)
    a_kv = k_bf.shape[1]
    n_heads_a = aq_t.shape[1] // HEAD_DIM
    rep_a = n_heads_a // A_KV_HEADS
    iw_row = lay["iw"][0] - lay["misc"][0]

    oa = _dsa_prompt(aq_t, iq_t, misc_t, iw_row, ik_bf, k_bf, v_bf, t_pad, t_p, topk_p)
    ob, state_p = _gla_prompt(p0, lay, t_pad, t_p, gate_w, b_head_norm)

    sample_rows = lambda a_t: a_t[n_blk:].transpose(0, 2, 1).reshape(m_s, a_t.shape[1])[:n_s]
    aq, iq = sample_rows(aq_t), sample_rows(iq_t)
    iw = sample_rows(misc_t[:, iw_row:iw_row + IDX_HEADS])
    iq_s = iq.reshape(db, ds, IDX_HEADS, IDX_DIM).transpose(0, 2, 1, 3)
    iq_s = jnp.pad(iq_s, ((0, 0), (0, 0), (0, Q_PAD - ds), (0, 0))).reshape(db, IDX_HEADS * Q_PAD, IDX_DIM)
    iw_s = iw.reshape(db, ds, IDX_HEADS).transpose(0, 2, 1)
    iw_s = jnp.pad(iw_s, ((0, 0), (0, 0), (0, Q_PAD - ds))).reshape(db, IDX_HEADS * Q_PAD, 1)
    iw_s = jnp.broadcast_to(iw_s, (db, IDX_HEADS * Q_PAD, LANES))
    new_ik_t = _pad_rows(ik_bf[srows].reshape(db, ds, IDX_DIM), page).transpose(0, 2, 1)
    scores = _idx_sample(page_table, iq_s, iw_s, new_ik_t, cache_a_idx_k.transpose(0, 2, 1))
    n_keys = scores.shape[2]
    n_q = _round_up(db * Q_PAD, LANES)
    scores_t = jnp.pad(scores.reshape(db * Q_PAD, n_keys).T, ((0, 0), (0, n_q - db * Q_PAD)))
    bias = _topk_sample(scores_t, topk_s, past_len, ds)[:, :db * Q_PAD].T.reshape(db, Q_PAD, n_keys)
    aq_s = aq.reshape(db, ds, A_KV_HEADS, rep_a, HEAD_DIM).transpose(0, 2, 3, 1, 4)
    aq_s = jnp.pad(aq_s, ((0, 0), (0, 0), (0, 0), (0, Q_PAD - ds), (0, 0))).reshape(db, A_KV_HEADS, rep_a * Q_PAD, HEAD_DIM)
    new_k = _pad_rows(k_bf[srows].reshape(db, ds, a_kv), page)
    new_v = _pad_rows(v_bf[srows].reshape(db, ds, a_kv), page)
    oa_s = _attn_sample(page_table, aq_s, bias, new_k, new_v,
                        cache_a_k.reshape(n_pool, page * A_KV_HEADS, HEAD_DIM),
                        cache_a_v.reshape(n_pool, page * A_KV_HEADS, HEAD_DIM))
    oa_s = oa_s.reshape(db, A_KV_HEADS, rep_a, Q_PAD, HEAD_DIM)[:, :, :, :ds].transpose(0, 3, 1, 2, 4)
    oa_s = oa_s.reshape(n_s, n_heads_a * HEAD_DIM).astype(BF16)

    chunk_s = lambda name: _pad_rows(seg(p0, name)[srows].reshape(db, ds, lay[name][1]), GLA_SUB)
    ob_s, state_s = _gla_sample(chunk_s("bq"), chunk_s("bk"), chunk_s("bv"), chunk_s("br"), chunk_s("misc"),
                                state_b, ds, gate_w, b_head_norm)
    ob_s = ob_s[:, :ds].reshape(n_s, -1)

    oa = oa.at[srows].set(oa_s)
    ob = ob.at[srows].set(ob_s)
    h, xn = _proj_res_norm([oa, ob], w_out0.astype(BF16), h, norm_ffn[0])
    w_ff1_bf, w_ff2_bf = w_ff1.astype(BF16), w_ff2.astype(BF16)
    h, xn = _mlp(xn, h, w_ff1_bf, w_ff2_bf, 0, norm_mix[1], BF16)

    c_kv = C_KV_HEADS * HEAD_DIM
    c_q = w_in1.shape[1] - 2 * c_kv
    rep_c = c_q // c_kv
    q1 = _matmul(xn, w_in1[:, :c_q].astype(BF16), BF16, scale=SB_Q_SCALE)
    ck, cv, ck_bf, cv_bf = _kv_proj(xn, w_in1[:, c_q:].astype(BF16), C_KV_HEADS)
    oc = _sb_prompt(q1, ck_bf, cv_bf, t_pad)
    q1_s = q1[srows].reshape(db, ds, C_KV_HEADS, rep_c, HEAD_DIM).transpose(0, 2, 3, 1, 4)
    q1_s = q1_s.reshape(db, C_KV_HEADS, rep_c * ds, HEAD_DIM)
    new_ck = _pad_rows(ck_bf[srows].reshape(db, ds, c_kv), page)
    new_cv = _pad_rows(cv_bf[srows].reshape(db, ds, c_kv), page)
    oc_s = _sb_sample(page_table, q1_s, new_ck, new_cv, cache_c_k.reshape(n_pool, page * C_KV_HEADS, HEAD_DIM),
                      cache_c_v.reshape(n_pool, page * C_KV_HEADS, HEAD_DIM), ds)
    oc_s = oc_s.reshape(db, C_KV_HEADS, rep_c, ds, HEAD_DIM).transpose(0, 3, 1, 2, 4).reshape(n_s, c_q).astype(BF16)
    h, xn = _proj_res_norm([oc.at[srows].set(oc_s)], w_out1.astype(BF16), h, norm_ffn[1])
    h, y = _mlp(xn, h, w_ff1_bf, w_ff2_bf, 1, norm_final, F32)

    y_prompt = y[N_META:t_p][None]
    y_sample = y[srows].reshape(db, ds, d)
    kv_p = lambda a, heads: a[:t_p * heads].reshape(1, t_p, heads, HEAD_DIM)
    kv_s = lambda a, heads: a[t_pad * heads:(t_pad + n_s) * heads].reshape(db, ds, heads, HEAD_DIM)
    return (y_prompt, y_sample,
            kv_p(ak, A_KV_HEADS), kv_p(av, A_KV_HEADS), ik[:t_p, :IDX_DIM][None], state_p[None],
            kv_p(ck, C_KV_HEADS), kv_p(cv, C_KV_HEADS),
            kv_s(ak, A_KV_HEADS), kv_s(av, A_KV_HEADS), ik[srows, :IDX_DIM].reshape(db, ds, IDX_DIM),
            state_s, kv_s(ck, C_KV_HEADS), kv_s(cv, C_KV_HEADS))
```

```python
import functools
import math

import jax
import jax.numpy as jnp
from jax import lax
from jax.experimental import pallas as pl
from jax.experimental.pallas import tpu as pltpu

F32 = jnp.float32
BF16 = jnp.bfloat16
I32 = jnp.int32

N_META = 16
HEAD_DIM = 128
ROPE_THETA = 10000.0
EPS = 1e-6
A_KV_HEADS = 2
IDX_HEADS = 16
IDX_DIM = 64
TOPK_MAX = 256
B_HEADS = 4
B_GATE_RANK = 16
B_GATE_TAU = 16.0
C_KV_HEADS = 4

LANES = 128
SUBLANES = 8
VMEM_LIMIT_BYTES = 56 * 1024 * 1024

ROW_BLOCK = 128
INT_MIN = -(2 ** 31)


def _round_up(x, m):
    return (x + m - 1) // m * m


def _row_tile(m, cap):
    best = ROW_BLOCK
    for k in range(1, m // ROW_BLOCK + 1):
        t = k * ROW_BLOCK
        if t <= cap and m % t == 0:
            best = t
    return best


def _col_tile(n, cap):
    best = LANES
    for k in range(1, n // LANES + 1):
        t = k * LANES
        if t <= cap and n % t == 0:
            best = t
    return best


def _params(*sem):
    return pltpu.CompilerParams(dimension_semantics=sem, vmem_limit_bytes=VMEM_LIMIT_BYTES)


def _rms(x, g):
    ms = jnp.mean(x * x, axis=-1, keepdims=True)
    return x * lax.rsqrt(ms + EPS) * g


def _rmsnorm_kernel(x_ref, g_ref, o_ref):
    o_ref[...] = _rms(x_ref[...], g_ref[...]).astype(o_ref.dtype)


def _rmsnorm(x, g, out_dtype):
    m, d = x.shape
    tm = _row_tile(m, 512)
    return pl.pallas_call(
        _rmsnorm_kernel,
        out_shape=jax.ShapeDtypeStruct((m, d), out_dtype),
        grid=(m // tm,),
        in_specs=[pl.BlockSpec((tm, d), lambda i: (i, 0)), pl.BlockSpec((1, d), lambda i: (0, 0))],
        out_specs=pl.BlockSpec((tm, d), lambda i: (i, 0)),
        compiler_params=_params("parallel"),
        name="rmsnorm",
    )(x, g.reshape(1, d))


MATMUL_COL_TILE = 1152


def _matmul_kernel(x_ref, w_ref, o_ref, *, scale):
    acc = jnp.dot(x_ref[...], w_ref[...], preferred_element_type=F32)
    o_ref[...] = (acc if scale is None else acc * scale).astype(o_ref.dtype)


def _matmul(x, w, out_dtype, scale=None):
    m, k = x.shape
    n = w.shape[1]
    tm = _row_tile(m, 1408)
    tn = _col_tile(n, MATMUL_COL_TILE)
    return pl.pallas_call(
        functools.partial(_matmul_kernel, scale=scale),
        out_shape=jax.ShapeDtypeStruct((m, n), out_dtype),
        grid=(m // tm, n // tn),
        in_specs=[pl.BlockSpec((tm, k), lambda i, j: (i, 0)), pl.BlockSpec((k, tn), lambda i, j: (0, j))],
        out_specs=pl.BlockSpec((tm, tn), lambda i, j: (i, j)),
        compiler_params=_params("parallel", "parallel"),
        name="matmul",
    )(x, w)


def _kv_proj_kernel(x_ref, w_ref, k_ref, v_ref, kbf_ref, vbf_ref, *, n_heads):
    acc = jnp.dot(x_ref[...], w_ref[...], preferred_element_type=F32)
    tm = x_ref.shape[0]
    ckv = n_heads * HEAD_DIM
    for h in range(n_heads):
        k_ref[pl.ds(h, tm, stride=n_heads), :] = acc[:, h * HEAD_DIM:(h + 1) * HEAD_DIM]
        v_ref[pl.ds(h, tm, stride=n_heads), :] = acc[:, ckv + h * HEAD_DIM:ckv + (h + 1) * HEAD_DIM]
    kbf_ref[...] = acc[:, :ckv].astype(BF16)
    vbf_ref[...] = acc[:, ckv:].astype(BF16)


def _kv_proj(x, w, n_heads):
    m, k = x.shape
    ckv = n_heads * HEAD_DIM
    assert w.shape == (k, 2 * ckv)
    tm = _row_tile(m, 768)
    row = lambda i: (i, 0)
    return pl.pallas_call(
        functools.partial(_kv_proj_kernel, n_heads=n_heads),
        out_shape=(jax.ShapeDtypeStruct((m * n_heads, HEAD_DIM), F32), jax.ShapeDtypeStruct((m * n_heads, HEAD_DIM), F32),
                   jax.ShapeDtypeStruct((m, ckv), BF16), jax.ShapeDtypeStruct((m, ckv), BF16)),
        grid=(m // tm,),
        in_specs=[pl.BlockSpec((tm, k), row), pl.BlockSpec((k, 2 * ckv), lambda i: (0, 0))],
        out_specs=(pl.BlockSpec((tm * n_heads, HEAD_DIM), row), pl.BlockSpec((tm * n_heads, HEAD_DIM), row),
                   pl.BlockSpec((tm, ckv), row), pl.BlockSpec((tm, ckv), row)),
        compiler_params=_params("parallel"),
        name="kv_proj",
    )(x, w)


def _proj_res_norm_kernel(*refs, n_in):
    x_refs, w_refs = refs[:n_in], refs[n_in:2 * n_in]
    h_ref, g_ref, hout_ref, xn_ref = refs[2 * n_in:]
    hn = h_ref[...]
    for x_ref, w_ref in zip(x_refs, w_refs):
        hn = hn + jnp.dot(x_ref[...], w_ref[...], preferred_element_type=F32)
    hout_ref[...] = hn
    xn_ref[...] = _rms(hn, g_ref[...]).astype(xn_ref.dtype)


def _proj_res_norm(xs, w, h, g):
    m, k = xs[0].shape
    assert all(x.shape == (m, k) for x in xs) and w.shape[0] == k * len(xs)
    d = w.shape[1]
    tm = _row_tile(m, 384)
    row = lambda i: (i, 0)
    in_specs = [pl.BlockSpec((tm, k), row) for _ in xs]
    in_specs += [pl.BlockSpec((k, d), lambda i, j=j: (j, 0)) for j in range(len(xs))]
    in_specs += [pl.BlockSpec((tm, d), row), pl.BlockSpec((1, d), lambda i: (0, 0))]
    return pl.pallas_call(
        functools.partial(_proj_res_norm_kernel, n_in=len(xs)),
        out_shape=(jax.ShapeDtypeStruct((m, d), F32), jax.ShapeDtypeStruct((m, d), BF16)),
        grid=(m // tm,),
        in_specs=in_specs,
        out_specs=(pl.BlockSpec((tm, d), row), pl.BlockSpec((tm, d), row)),
        compiler_params=_params("parallel"),
        name="proj_res_norm",
    )(*xs, *([w] * len(xs)), h, g.reshape(1, d))


def _mlp_kernel(xn_ref, h_ref, w1_ref, w2_ref, g_ref, hout_ref, xnout_ref):
    f = pl.program_id(1)

    @pl.when(f == 0)
    def _():
        hout_ref[...] = h_ref[...]

    a = jnp.maximum(jnp.dot(xn_ref[...], w1_ref[...], preferred_element_type=F32), 0.0)
    hout_ref[...] += jnp.dot((a * a).astype(BF16), w2_ref[...], preferred_element_type=F32)

    @pl.when(f == pl.num_programs(1) - 1)
    def _():
        xnout_ref[...] = _rms(hout_ref[...], g_ref[...]).astype(xnout_ref.dtype)


def _mlp(xn, h, w1, w2, layer, g, norm_dtype):
    m, d = xn.shape
    dff = w1.shape[2]
    tm = _row_tile(m, 768)
    tf = _col_tile(dff, 1024)
    return pl.pallas_call(
        _mlp_kernel,
        out_shape=(jax.ShapeDtypeStruct((m, d), F32), jax.ShapeDtypeStruct((m, d), norm_dtype)),
        grid=(m // tm, dff // tf),
        in_specs=[pl.BlockSpec((tm, d), lambda i, f: (i, 0)),
                  pl.BlockSpec((tm, d), lambda i, f: (i, 0), pipeline_mode=pl.Buffered(1)),
                  pl.BlockSpec((None, d, tf), lambda i, f: (layer, 0, f)),
                  pl.BlockSpec((None, tf, d), lambda i, f: (layer, f, 0)),
                  pl.BlockSpec((1, d), lambda i, f: (0, 0))],
        out_specs=(pl.BlockSpec((tm, d), lambda i, f: (i, 0)), pl.BlockSpec((tm, d), lambda i, f: (i, 0))),
        compiler_params=_params("parallel", "arbitrary"),
        name="mlp",
    )(xn, h, w1, w2, g.reshape(1, d))


def _even_layout(d_model):
    a_q = d_model // 2
    a_kv = A_KV_HEADS * HEAD_DIM
    idx_q = IDX_HEADS * IDX_DIM
    b_v = d_model // 2
    b_qk = b_v // 2
    sizes = dict(aq=a_q, ak=a_kv, av=a_kv, iq=idx_q, ik=IDX_DIM, iw=IDX_HEADS, bq=b_qk, bk=b_qk, bv=b_v,
                 bglr=B_GATE_RANK, br=b_v)
    start, off = {}, 0
    for name, size in sizes.items():
        start[name] = off
        off += size
    firsts = ("aq", "bq", "br")
    bounds = [start[n] for n in firsts] + [off]
    lay, groups = {}, []
    for g in range(len(firsts)):
        lo, hi = bounds[g], bounds[g + 1]
        width = _round_up(hi - lo, LANES)
        while _col_tile(width, MATMUL_COL_TILE) < MATMUL_COL_TILE // 2:
            width += LANES
        groups.append((lo, hi, width))
        for name, size in sizes.items():
            if lo <= start[name] < hi:
                lay[name] = (g, start[name] - lo, size)
    lay["misc"] = (lay["ik"][0], lay["ik"][1], LANES)
    lay["glr"] = (lay["bglr"][0], lay["bglr"][1], LANES)
    for name in ("aq", "ak", "av", "iq", "misc", "bq", "bk", "bv", "glr", "br"):
        assert lay[name][1] % LANES == 0
    return lay, groups


def _rope_freqs():
    def inv(d):
        return ROPE_THETA ** (-jnp.arange(d // 2, dtype=F32) * (2.0 / d))
    pad = jnp.zeros((LANES - HEAD_DIM // 2 - IDX_DIM // 2,), F32)
    return jnp.concatenate([inv(HEAD_DIM), inv(IDX_DIM), pad]).reshape(1, LANES)


def _rope_kernel(*refs, n_aq, n_iq, n_ak, n_iq_in, n_prompt_rows, past_len, dec_seq):
    q_ref, kv_ref = refs[:2]
    iq_in = refs[2:2 + n_iq_in]
    misc_in, inv_ref = refs[2 + n_iq_in:4 + n_iq_in]
    aq_ref, iq_ref, misc_ref, ak_ref, av_ref, ik_ref, kbf_ref, vbf_ref, ikbf_ref = refs[4 + n_iq_in:]
    tr = q_ref.shape[0]
    iq_per_in = n_iq // n_iq_in
    half, quarter = HEAD_DIM // 2, IDX_DIM // 2
    row = pl.program_id(0) * tr + lax.broadcasted_iota(I32, (tr, 1), 0)
    pos = jnp.where(row < n_prompt_rows, row, past_len + (row - n_prompt_rows) % dec_seq).astype(F32)
    ang = pos * inv_ref[...]
    c, s = jnp.cos(ang), jnp.sin(ang)
    cos128 = jnp.concatenate([c[:, :half]] * 2, axis=1)
    sin128 = jnp.concatenate([-s[:, :half], s[:, :half]], axis=1)
    c_i, s_i = c[:, half:half + quarter], s[:, half:half + quarter]
    cos64 = jnp.concatenate([c_i] * (LANES // quarter), axis=1)
    sin64 = jnp.concatenate([-s_i, s_i] * (LANES // IDX_DIM), axis=1)

    def rope128(x):
        return x * cos128 + pltpu.roll(x, half, 1) * sin128

    lane = lax.broadcasted_iota(I32, (tr, LANES), 1)
    low_half = (lane % IDX_DIM) < quarter

    def rope64(x):
        rot = jnp.where(low_half, pltpu.roll(x, LANES - quarter, 1), pltpu.roll(x, quarter, 1))
        return x * cos64 + rot * sin64

    n_blk = tr // ROW_BLOCK

    def put_transposed(ref, c, x):
        for b in range(n_blk):
            tile = jnp.transpose(x[b * ROW_BLOCK:(b + 1) * ROW_BLOCK, :])
            ref[b, c * LANES:(c + 1) * LANES, :] = tile.astype(ref.dtype)

    for c in range(n_aq):
        sl = slice(c * LANES, (c + 1) * LANES)
        put_transposed(aq_ref, c, rope128(q_ref[:, sl]) * EXP2_SCALE)
    for c in range(n_iq):
        src = iq_in[c // iq_per_in]
        put_transposed(iq_ref, c, rope64(src[:, (c % iq_per_in) * LANES:(c % iq_per_in + 1) * LANES]))
    put_transposed(misc_ref, 0, misc_in[...])
    for c in range(n_ak):
        sl = slice(c * LANES, (c + 1) * LANES)
        kr = rope128(kv_ref[:, sl])
        vr = kv_ref[:, n_ak * LANES + c * LANES:n_ak * LANES + (c + 1) * LANES]
        ak_ref[pl.ds(c, tr, stride=n_ak), :] = kr
        av_ref[pl.ds(c, tr, stride=n_ak), :] = vr
        kbf_ref[:, sl] = kr.astype(BF16)
        vbf_ref[:, sl] = vr.astype(BF16)
    ik = rope64(misc_in[...])
    ik_ref[...] = ik
    ikbf_ref[...] = ik[:, :IDX_DIM].astype(BF16)


def _rope(p0, lay, n_prompt_rows, past_len, dec_seq):
    m = p0.shape[0]
    tr = _row_tile(m, 384)
    a_q, a_kv, idx_q = lay["aq"][2], lay["ak"][2], lay["iq"][2]
    n_ak = a_kv // LANES
    assert lay["av"][1] == lay["ak"][1] + a_kv and lay["ak"][1] % (2 * a_kv) == 0 and lay["aq"][1] % a_q == 0
    iq_w = math.gcd(lay["iq"][1], idx_q)
    n_iq_in = idx_q // iq_w
    kern = functools.partial(_rope_kernel, n_aq=a_q // LANES, n_iq=idx_q // LANES, n_ak=n_ak, n_iq_in=n_iq_in,
                             n_prompt_rows=n_prompt_rows, past_len=past_len, dec_seq=dec_seq)
    cols = lambda off, width: (lambda i: (i, off // width))
    in_specs = [pl.BlockSpec((tr, a_q), cols(lay["aq"][1], a_q)),
                pl.BlockSpec((tr, 2 * a_kv), cols(lay["ak"][1], 2 * a_kv))]
    in_specs += [pl.BlockSpec((tr, iq_w), cols(lay["iq"][1] + j * iq_w, iq_w)) for j in range(n_iq_in)]
    in_specs += [pl.BlockSpec((tr, LANES), cols(lay["misc"][1], LANES)), pl.BlockSpec((1, LANES), lambda i: (0, 0))]
    row = lambda i: (i, 0)
    blk = lambda i: (i, 0, 0)
    nb, tb = m // ROW_BLOCK, tr // ROW_BLOCK
    return pl.pallas_call(
        kern,
        out_shape=(jax.ShapeDtypeStruct((nb, a_q, ROW_BLOCK), BF16), jax.ShapeDtypeStruct((nb, idx_q, ROW_BLOCK), BF16),
                   jax.ShapeDtypeStruct((nb, LANES, ROW_BLOCK), F32),
                   jax.ShapeDtypeStruct((m * n_ak, LANES), F32), jax.ShapeDtypeStruct((m * n_ak, LANES), F32),
                   jax.ShapeDtypeStruct((m, LANES), F32),
                   jax.ShapeDtypeStruct((m, a_kv), BF16), jax.ShapeDtypeStruct((m, a_kv), BF16),
                   jax.ShapeDtypeStruct((m, IDX_DIM), BF16)),
        grid=(m // tr,),
        in_specs=in_specs,
        out_specs=(pl.BlockSpec((tb, a_q, ROW_BLOCK), blk), pl.BlockSpec((tb, idx_q, ROW_BLOCK), blk),
                   pl.BlockSpec((tb, LANES, ROW_BLOCK), blk),
                   pl.BlockSpec((tr * n_ak, LANES), row), pl.BlockSpec((tr * n_ak, LANES), row),
                   pl.BlockSpec((tr, LANES), row), pl.BlockSpec((tr, a_kv), row), pl.BlockSpec((tr, a_kv), row),
                   pl.BlockSpec((tr, IDX_DIM), row)),
        compiler_params=_params("parallel"),
        name="rope",
    )(*([p0] * (3 + n_iq_in)), _rope_freqs())


GLA_SUB = 16


def _split_bf16(x):
    hi = x.astype(BF16)
    return hi, (x - hi.astype(F32)).astype(BF16)


def _log_sigmoid(x):
    return jnp.minimum(x, 0.0) - jnp.log(1.0 + jnp.exp(-jnp.abs(x)))


def _dot(a, b):
    return jnp.dot(a, b, preferred_element_type=F32)


def _dot_nt(a, b):
    return lax.dot_general(a, b, (((1,), (1,)), ((), ())), preferred_element_type=F32)


def _dot_tn(a, b):
    return lax.dot_general(a, b, (((0,), (0,)), ((), ())), preferred_element_type=F32)


def _gla_gate(misc, w2_hi, w2_lo, bias, valid):
    m_hi, m_lo = _split_bf16(misc)
    x = _dot(m_hi, w2_hi) + _dot(m_lo, w2_hi) + _dot(m_hi, w2_lo) + bias
    return jnp.where(valid, _log_sigmoid(x) * (1.0 / B_GATE_TAU), 0.0)


def _gla_chunks(chunks, state_ref, dk, dv):
    heads = range(B_HEADS)
    local = [_gla_local(q, k, v, g, dk, dv) for q, k, v, g in chunks]
    states = [state_ref[h] for h in heads]
    outs = []
    for intra, q_in, grow, decay in local:
        outs.append([intra[h] + _dot(q_in[h], states[h].astype(BF16)) for h in heads])
        states = [decay[h] * states[h] + grow[h] for h in heads]
    for h in heads:
        state_ref[h] = states[h]
    return outs


def _gla_local(q, k, v, g, dk, dv):
    c = q.shape[0]
    n_sub = c // GLA_SUB
    ri = lax.broadcasted_iota(I32, (c, c), 0)
    ci = lax.broadcasted_iota(I32, (c, c), 1)
    causal = ri >= ci
    tri = jnp.where(causal, 1.0, 0.0).astype(BF16)
    gcum = _dot(jnp.concatenate([tri, tri], axis=1), jnp.concatenate(_split_bf16(g), axis=0))
    scale = dk ** -0.5
    pad = _round_up(c, LANES)
    heads = range(B_HEADS)
    gh = [gcum[:, h * dk:(h + 1) * dk] for h in heads]
    qh = [q[:, h * dk:(h + 1) * dk] * scale for h in heads]
    kh = [k[:, h * dk:(h + 1) * dk] for h in heads]
    vh = [v[:, h * dv:(h + 1) * dv].astype(BF16) for h in heads]
    last = [gh[h][c - 1:c, :] for h in heads]
    qts, kts = [], []
    for h in heads:
        q_parts, k_parts = [], []
        for i in range(n_sub):
            lo, hi = i * GLA_SUB, (i + 1) * GLA_SUB
            base = gh[h][lo - 1:lo, :] if i else jnp.zeros((1, dk), F32)
            q_parts.append(qh[h][lo:hi] * jnp.exp(gh[h][lo:hi] - base))
            k_parts.append(kh[h][:hi] * jnp.exp(base - gh[h][:hi]))
            if pad > hi:
                k_parts.append(jnp.zeros((pad - hi, dk), F32))
        qts.append(jnp.concatenate(q_parts, axis=0).astype(BF16))
        kts.append(jnp.concatenate(k_parts, axis=0).astype(BF16))
    q_in = [(qh[h] * jnp.exp(gh[h])).astype(BF16) for h in heads]
    k_tail = [(kh[h] * jnp.exp(last[h] - gh[h])).astype(BF16) for h in heads]
    rs = [_dot_nt(qts[h], kts[h]) for h in heads]
    grow = [_dot_tn(k_tail[h], vh[h]) for h in heads]
    intra, decay = [], []
    for h in heads:
        att = jnp.concatenate([rs[h][i * GLA_SUB:(i + 1) * GLA_SUB, i * pad:i * pad + c] for i in range(n_sub)],
                              axis=0)
        att = jnp.where(causal, att, 0.0).astype(BF16)
        intra.append(_dot(att, vh[h]))
        d = jnp.transpose(jnp.broadcast_to(jnp.exp(last[h]), (dk, dk)))
        decay.append(jnp.concatenate([d] * (dv // dk), axis=1))
    return intra, q_in, grow, decay


def _gla_out(outs, r, head_norm, dv):
    res = []
    for h, o in enumerate(outs):
        gate = r[:, h * dv:(h + 1) * dv]
        gate = gate * (1.0 / (1.0 + jnp.exp(-gate)))
        res.append(_rms(o, head_norm) * gate)
    return jnp.concatenate(res, axis=1)


def _gla_prompt_kernel(q_ref, k_ref, v_ref, r_ref, misc_ref, w2h_ref, w2l_ref, b_ref, hn_ref, o_ref, state_ref,
                       *, chunk, n_valid, dk, dv):
    blk = pl.program_id(0)

    @pl.when(blk == 0)
    def _():
        state_ref[...] = jnp.zeros_like(state_ref)

    rows = q_ref.shape[0]
    row = blk * rows + lax.broadcasted_iota(I32, (rows, 1), 0)
    g = _gla_gate(misc_ref[...], w2h_ref[...], w2l_ref[...], b_ref[...], row < n_valid)
    slices = [slice(c0, c0 + chunk) for c0 in range(0, rows, chunk)]
    outs = _gla_chunks([(q_ref[sl, :], k_ref[sl, :], v_ref[sl, :], g[sl, :]) for sl in slices], state_ref, dk, dv)
    for sl, out in zip(slices, outs):
        o_ref[sl, :] = _gla_out(out, r_ref[sl, :], hn_ref[...], dv).astype(o_ref.dtype)


def _gla_sample_kernel(q_ref, k_ref, v_ref, r_ref, misc_ref, w2h_ref, w2l_ref, b_ref, hn_ref, s_ref, o_ref,
                       state_ref, *, n_valid, dk, dv):
    state_ref[0] = s_ref[0]
    rows = q_ref.shape[1]
    row = lax.broadcasted_iota(I32, (rows, 1), 0)
    g = _gla_gate(misc_ref[0], w2h_ref[...], w2l_ref[...], b_ref[...], row < n_valid)
    outs, = _gla_chunks([(q_ref[0], k_ref[0], v_ref[0], g)], state_ref.at[0], dk, dv)
    o_ref[0] = _gla_out(outs, r_ref[0], hn_ref[...], dv).astype(o_ref.dtype)


def _gate_weights(w_b_gate2, b_b_gate, lay):
    off = lay["bglr"][1] - lay["glr"][1]
    w2 = jnp.zeros((LANES, w_b_gate2.shape[1]), F32).at[off:off + B_GATE_RANK].set(w_b_gate2)
    hi = w2.astype(BF16)
    lo = (w2 - hi.astype(F32)).astype(BF16)
    return hi, lo, b_b_gate.reshape(1, -1)


def _gla_prompt(proj, lay, n_rows, n_valid, gate_w, head_norm, chunk=64):
    bqk, bv = lay["bq"][2], lay["bv"][2]
    dk, dv = bqk // B_HEADS, bv // B_HEADS
    w2h, w2l, bias = gate_w
    names = ("bq", "bk", "bv", "br", "glr")
    for name in names:
        assert lay[name][1] % lay[name][2] == 0
    col = lambda name: (lambda i, o=lay[name][1] // lay[name][2]: (i, o))
    const = lambda i: (0, 0)
    kern = functools.partial(_gla_prompt_kernel, chunk=chunk, n_valid=n_valid, dk=dk, dv=dv)
    return pl.pallas_call(
        kern,
        out_shape=(jax.ShapeDtypeStruct((proj[0].shape[0], bv), BF16), jax.ShapeDtypeStruct((B_HEADS, dk, dv), F32)),
        grid=(n_rows // ROW_BLOCK,),
        in_specs=[pl.BlockSpec((ROW_BLOCK, bqk), col("bq")), pl.BlockSpec((ROW_BLOCK, bqk), col("bk")),
                  pl.BlockSpec((ROW_BLOCK, bv), col("bv")), pl.BlockSpec((ROW_BLOCK, bv), col("br")),
                  pl.BlockSpec((ROW_BLOCK, LANES), col("glr")),
                  pl.BlockSpec(w2h.shape, const), pl.BlockSpec(w2l.shape, const), pl.BlockSpec(bias.shape, const),
                  pl.BlockSpec((1, dv), const)],
        out_specs=(pl.BlockSpec((ROW_BLOCK, bv), lambda i: (i, 0)),
                   pl.BlockSpec((B_HEADS, dk, dv), lambda i: (0, 0, 0))),
        compiler_params=_params("arbitrary"),
        name="gla_prompt",
    )(*[proj[lay[name][0]] for name in names], w2h, w2l, bias, head_norm.reshape(1, dv))


def _gla_sample(q, k, v, r, misc, state, n_valid, gate_w, head_norm):
    db, rows, bqk = q.shape
    bv = v.shape[2]
    dk, dv = bqk // B_HEADS, bv // B_HEADS
    w2h, w2l, bias = gate_w
    seq = lambda i: (i, 0, 0)
    const = lambda i: (0, 0)
    kern = functools.partial(_gla_sample_kernel, n_valid=n_valid, dk=dk, dv=dv)
    return pl.pallas_call(
        kern,
        out_shape=(jax.ShapeDtypeStruct((db, rows, bv), BF16), jax.ShapeDtypeStruct(state.shape, F32)),
        grid=(db,),
        in_specs=[pl.BlockSpec((1, rows, bqk), seq), pl.BlockSpec((1, rows, bqk), seq),
                  pl.BlockSpec((1, rows, bv), seq), pl.BlockSpec((1, rows, bv), seq),
                  pl.BlockSpec((1, rows, LANES), seq),
                  pl.BlockSpec(w2h.shape, const), pl.BlockSpec(w2l.shape, const), pl.BlockSpec(bias.shape, const),
                  pl.BlockSpec((1, dv), const),
                  pl.BlockSpec((1, B_HEADS, dk, dv), lambda i: (i, 0, 0, 0))],
        out_specs=(pl.BlockSpec((1, rows, bv), seq), pl.BlockSpec((1, B_HEADS, dk, dv), lambda i: (i, 0, 0, 0))),
        compiler_params=_params("parallel"),
        name="gla_sample",
    )(q, k, v, r, misc, w2h, w2l, bias, head_norm.reshape(1, dv), state)


SB_DONE = -105.0
SB_Q_SCALE = HEAD_DIM ** -0.5 * math.log2(math.e)
SB_DONE2 = SB_DONE * math.log2(math.e)


def _sb_suffix_matrix():
    j = lax.broadcasted_iota(I32, (2 * LANES, 2 * LANES), 0) % LANES
    s = lax.broadcasted_iota(I32, (2 * LANES, 2 * LANES), 1)
    return jnp.where((j > s) | (s >= LANES), 1.0, 0.0).astype(BF16)


def _sb_blocks(qs, ks, vs, strict, carries, accs, suffix):
    zs = [_dot_nt(q, k) for q, k in zip(qs, ks)]
    keeps, takes = [], []
    for z in zs:
        soft = jnp.log2(1.0 + jnp.exp2(-jnp.abs(z)))
        log_keep = -(jnp.maximum(z, 0.0) + soft)
        keeps.append(log_keep if strict is None else jnp.where(strict, log_keep, 0.0))
        takes.append(jnp.minimum(z, 0.0) - soft)
    sums = []
    for lk in keeps:
        sums.append(_dot(jnp.concatenate(_split_bf16(lk), axis=1), suffix))
    ws = []
    for take, s, c in zip(takes, sums, carries):
        w = jnp.exp2(take + (s[:, :LANES] + c))
        ws.append((w if strict is None else jnp.where(strict, w, 0.0)).astype(BF16))
    new_carries = [c + s[:, LANES:] for c, s in zip(carries, sums)]
    new_accs = [a + _dot(w, v) for a, w, v in zip(accs, ws, vs)]
    return new_carries, new_accs


def _sb_prompt_kernel(q_ref, k_ref, v_ref, o_ref, *scratch, n_kv, rep):
    carry_refs, acc_refs = scratch[:n_kv], scratch[n_kv:]
    b = pl.program_id(0)
    suffix = _sb_suffix_matrix()
    for ref in scratch:
        ref[...] = jnp.zeros_like(ref)
    rows = rep * ROW_BLOCK
    q_row = lax.broadcasted_iota(I32, (rows, LANES), 0) % ROW_BLOCK
    col = lax.broadcasted_iota(I32, (rows, LANES), 1)
    head = lambda g: slice(g * HEAD_DIM, (g + 1) * HEAD_DIM)

    def visit(kb, strict):
        k0 = pl.multiple_of(kb * ROW_BLOCK, ROW_BLOCK)
        qs = [jnp.concatenate([q_ref[:, head(g * rep + r)] for r in range(rep)], axis=0) for g in range(n_kv)]
        ks = [k_ref[pl.ds(k0, ROW_BLOCK), head(g)] for g in range(n_kv)]
        vs = [v_ref[pl.ds(k0, ROW_BLOCK), head(g)] for g in range(n_kv)]
        carries, accs = _sb_blocks(qs, ks, vs, strict, [r[...] for r in carry_refs], [r[...] for r in acc_refs],
                                   suffix)
        top = None
        for g in range(n_kv):
            carry_refs[g][...] = carries[g]
            acc_refs[g][...] = accs[g]
            part = _col_reduce(carries[g], jnp.max)
            top = part if top is None else jnp.maximum(top, part)
        return (jnp.max(top) < SB_DONE2).astype(I32)

    done = visit(b, col < q_row)
    lax.while_loop(lambda st: (st[0] >= 0) & (st[1] == 0), lambda st: (st[0] - 1, visit(st[0], None)), (b - 1, done))
    for g in range(n_kv):
        for r in range(rep):
            o_ref[:, head(g * rep + r)] = acc_refs[g][r * ROW_BLOCK:(r + 1) * ROW_BLOCK, :].astype(o_ref.dtype)


def _sb_prompt(q, k, v, n_rows):
    cq = q.shape[1]
    ckv = k.shape[1]
    n_kv = ckv // HEAD_DIM
    rep = cq // ckv
    kern = functools.partial(_sb_prompt_kernel, n_kv=n_kv, rep=rep)
    return pl.pallas_call(
        kern,
        out_shape=jax.ShapeDtypeStruct(q.shape, BF16),
        grid=(n_rows // ROW_BLOCK,),
        in_specs=[pl.BlockSpec((ROW_BLOCK, cq), lambda i: (i, 0)),
                  pl.BlockSpec((n_rows, ckv), lambda i: (0, 0)),
                  pl.BlockSpec((n_rows, ckv), lambda i: (0, 0))],
        out_specs=pl.BlockSpec((ROW_BLOCK, cq), lambda i: (i, 0)),
        scratch_shapes=[pltpu.VMEM((rep * ROW_BLOCK, LANES), F32)] * n_kv
                       + [pltpu.VMEM((rep * ROW_BLOCK, HEAD_DIM), F32)] * n_kv,
        compiler_params=_params("parallel"),
        name="stick_prompt",
    )(q, k, v)


def _page_head(ref, g, n_heads):
    keys = ref.shape[1] // n_heads
    return ref[0, pl.ds(g, keys, stride=n_heads), :].astype(BF16)


def _sb_sample_kernel(pt_ref, q_ref, *refs, n_kv, n_new, dec_seq, n_pages_step, has_init):
    del pt_ref
    refs = list(refs)
    new_refs = [refs.pop(0), refs.pop(0)] if n_new else None
    k_pages = [refs.pop(0) for _ in range(n_pages_step)]
    v_pages = [refs.pop(0) for _ in range(n_pages_step)]
    init_refs = [refs.pop(0), refs.pop(0)] if has_init else None
    o_ref, carry_ref = refs
    suffix = _sb_suffix_matrix()
    rows = q_ref.shape[2]

    @pl.when(pl.program_id(1) == 0)
    def _():
        if has_init:
            o_ref[...] = init_refs[0][...]
            carry_ref[...] = init_refs[1][...]
        else:
            o_ref[...] = jnp.zeros_like(o_ref)
            carry_ref[...] = jnp.zeros_like(carry_ref)

    def visit(k_of, v_of, strict):
        @pl.when(jnp.max(carry_ref[...]) >= SB_DONE2)
        def _():
            heads = range(n_kv)
            carries, accs = _sb_blocks([q_ref[0, g] for g in heads], [k_of(g) for g in heads],
                                       [v_of(g) for g in heads], strict, [carry_ref[0, g] for g in heads],
                                       [o_ref[0, g] for g in heads], suffix)
            for g in heads:
                carry_ref[0, g] = carries[g]
                o_ref[0, g] = accs[g]

    cols = lambda ref, g: ref[0, :, g * HEAD_DIM:(g + 1) * HEAD_DIM].astype(BF16)
    if n_new:
        q_idx = lax.broadcasted_iota(I32, (rows, LANES), 0) % dec_seq
        col = lax.broadcasted_iota(I32, (rows, LANES), 1)
        visit(lambda g: cols(new_refs[0], g), lambda g: cols(new_refs[1], g), col < q_idx)
    for kp, vp in zip(k_pages, v_pages):
        visit(lambda g, kp=kp: _page_head(kp, g, n_kv), lambda g, vp=vp: _page_head(vp, g, n_kv), None)


def _sb_sample_call(page_table, q, new_kv, cache_k, cache_v, init, dec_seq, first_page, n_steps, pages_per_step):
    db, n_kv, rows, _ = q.shape
    page_rows = cache_k.shape[1]
    assert page_rows == LANES * n_kv
    seq4 = lambda b, c, pt: (b, 0, 0, 0)
    in_specs = [pl.BlockSpec((1, n_kv, rows, HEAD_DIM), seq4)]
    args = [q]
    if new_kv is not None:
        in_specs += [pl.BlockSpec((1,) + new_kv[0].shape[1:], lambda b, c, pt: (b, 0, 0))] * 2
        args += list(new_kv)
    for cache in (cache_k, cache_v):
        for i in range(pages_per_step):
            in_specs.append(pl.BlockSpec(
                (1, page_rows, HEAD_DIM), lambda b, c, pt, i=i: (pt[b, first_page - c * pages_per_step - i], 0, 0)))
            args.append(cache)
    if init is not None:
        in_specs += [pl.BlockSpec((1, n_kv, rows, HEAD_DIM), seq4)] * 2
        args += list(init)
    kern = functools.partial(_sb_sample_kernel, n_kv=n_kv, n_new=new_kv is not None, dec_seq=dec_seq,
                             n_pages_step=pages_per_step, has_init=init is not None)
    out_sds = jax.ShapeDtypeStruct((db, n_kv, rows, HEAD_DIM), F32)
    return pl.pallas_call(
        kern,
        out_shape=(out_sds, out_sds),
        grid_spec=pltpu.PrefetchScalarGridSpec(
            num_scalar_prefetch=1,
            grid=(db, n_steps),
            in_specs=in_specs,
            out_specs=(pl.BlockSpec((1, n_kv, rows, HEAD_DIM), seq4), pl.BlockSpec((1, n_kv, rows, HEAD_DIM), seq4)),
        ),
        compiler_params=_params("parallel", "arbitrary"),
        name="stick_sample",
    )(page_table, *args)


def _sortable(score):
    bits = lax.bitcast_convert_type(score + 0.0, I32)
    key = bits ^ (lax.shift_right_arithmetic(bits, 31) & 0x7FFFFFFF)
    return jnp.maximum(key, INT_MIN + 1)


COUNT_ROWS = 64


def _count(key_ref, n_units, unit, pred):
    rows = COUNT_ROWS if unit % COUNT_ROWS == 0 else SUBLANES

    def body(u, acc):
        r0 = pl.multiple_of(u * unit, unit)
        hit = pred(key_ref[pl.ds(r0, unit), :], r0)
        ones = jnp.where(hit, jnp.ones(hit.shape, I32), jnp.zeros(hit.shape, I32))
        return acc + jnp.sum(ones.reshape(unit // rows, rows, LANES), axis=0)
    acc = lax.fori_loop(0, n_units, body, jnp.zeros((rows, LANES), I32))
    return jnp.sum(acc, axis=0, keepdims=True)


def _select_topk(key_ref, n_units, unit, k, n_rows):
    def bit_step(i, state):
        tau, c_ge = state
        cand = tau ^ lax.shift_left(jnp.int32(1), 31 - i)
        cnt = _count(key_ref, n_units, unit, lambda keys, r0: keys >= cand)
        return jnp.where(cnt >= k, cand, tau), jnp.where(cnt >= k, cnt, c_ge)
    start = (jnp.full((1, LANES), INT_MIN, I32), jnp.full((1, LANES), n_units * unit, I32))
    tau, c_ge = lax.fori_loop(0, 32, bit_step, start)
    c_gt = _count(key_ref, n_units, unit, lambda keys, r0: keys > tau)
    real = tau > INT_MIN
    tie = real & (c_ge > k)
    need = k - c_gt
    row = lax.broadcasted_iota(I32, (unit, LANES), 0)
    n_bits = max(1, (n_rows - 1).bit_length())

    def break_ties():
        def pos_step(i, x):
            cand = x | lax.shift_left(jnp.int32(1), n_bits - 1 - i)
            cnt = _count(key_ref, n_units, unit, lambda keys, r0: (keys == tau) & ((r0 + row) < cand))
            return jnp.where(cnt < need, cand, x)
        return lax.fori_loop(0, n_bits, pos_step, jnp.zeros((1, LANES), I32))

    last_tie = lax.cond(jnp.max(tie.astype(I32)) > 0, break_ties, lambda: jnp.zeros((1, LANES), I32))
    last = jnp.where(tie, last_tie, jnp.where(real, jnp.int32(2 ** 30), jnp.int32(-1)))
    return tau, last


def _col_reduce(x, fn):
    rows = x.shape[0]
    if rows > COUNT_ROWS and rows % COUNT_ROWS == 0:
        x = fn(x.reshape(rows // COUNT_ROWS, COUNT_ROWS, x.shape[1]), axis=0)
    return fn(x, axis=0, keepdims=True)


def _selected(keys, r0, tau, last):
    row = lax.broadcasted_iota(I32, keys.shape, 0)
    return (keys > tau) | ((keys == tau) & ((r0 + row) <= last))


IDX_SCALE = (IDX_HEADS * IDX_DIM) ** -0.5
MASKED = -1e30
EXP2_SCALE = HEAD_DIM ** -0.5 * math.log2(math.e)


def _dsa_prompt_kernel(aq_ref, iq_ref, misc_ref, ik_ref, k_ref, v_ref, o_ref, key_ref, *scratch,
                       unit, topk, n_valid, n_heads, n_kv, iw_row):
    b = pl.program_id(0)
    n_units = ((b + 1) * ROW_BLOCK + unit - 1) // unit
    rep = n_heads // n_kv
    row = lax.broadcasted_iota(I32, (unit, LANES), 0)
    q_pos = b * ROW_BLOCK + lax.broadcasted_iota(I32, (unit, LANES), 1)
    q_real = q_pos < n_valid

    iq_t = iq_ref[0]
    rhs = jnp.concatenate([iq_t[h * IDX_DIM:(h + 1) * IDX_DIM, :] for h in range(IDX_HEADS)], axis=1)
    w_t = misc_ref[0, iw_row:iw_row + IDX_HEADS, :]

    def score_unit(u, carry):
        r0 = pl.multiple_of(u * unit, unit)
        dots = _dot(ik_ref[pl.ds(r0, unit), :], rhs)
        score = jnp.zeros((unit, LANES), F32)
        for h in range(IDX_HEADS):
            score = score + jnp.maximum(dots[:, h * LANES:(h + 1) * LANES], 0.0) * w_t[h:h + 1, :]
        k_pos = r0 + row
        adm = (k_pos <= q_pos) & (q_real | (k_pos == 0))
        key_ref[pl.ds(r0, unit), :] = jnp.where(adm, _sortable(score * IDX_SCALE), INT_MIN)
        return carry

    lax.fori_loop(0, n_units, score_unit, 0)
    tau, last = _select_topk(key_ref, n_units, unit, topk, key_ref.shape[0])

    m_refs, l_refs, acc_refs, lg_refs = (scratch[i * n_kv:(i + 1) * n_kv] for i in range(4))
    for g in range(n_kv):
        m_refs[g][...] = jnp.full_like(m_refs[g], MASKED)
        l_refs[g][...] = jnp.zeros_like(l_refs[g])
        acc_refs[g][...] = jnp.zeros_like(acc_refs[g])
    aq_t = aq_ref[0]

    qgs = [jnp.concatenate([aq_t[(g * rep + r) * HEAD_DIM:(g * rep + r + 1) * HEAD_DIM, :] for r in range(rep)],
                           axis=1) for g in range(n_kv)]
    subs = [(s, ROW_BLOCK) for s in range(0, unit, ROW_BLOCK)]
    fold = lambda x, fn: fn(x.reshape(x.shape[0] // SUBLANES, SUBLANES, x.shape[1]), axis=0)

    def logits_of_unit(u, slot):
        r0 = pl.multiple_of(u * unit, unit)
        sel = _selected(key_ref[pl.ds(r0, unit), :], r0, tau, last)
        tops = [None] * n_kv
        for start, size in subs:
            sub = slice(start, start + size)
            for g in range(n_kv):
                lg = _dot(k_ref[pl.ds(r0 + start, size), g * HEAD_DIM:(g + 1) * HEAD_DIM], qgs[g])
                parts = []
                for r in range(rep):
                    x = jnp.where(sel[sub], lg[:, r * LANES:(r + 1) * LANES], MASKED)
                    lg_refs[g][slot, sub, r * LANES:(r + 1) * LANES] = x
                    parts.append(fold(x, jnp.max))
                part = jnp.concatenate(parts, axis=1)
                tops[g] = part if tops[g] is None else jnp.maximum(tops[g], part)
        return tuple(tops)

    def attend_unit(u, tops):
        r0 = pl.multiple_of(u * unit, unit)
        slot = u % 2
        next_tops = logits_of_unit(jnp.minimum(u + 1, n_units - 1), 1 - slot)
        m_old = [m_refs[g][...] for g in range(n_kv)]
        m_new = [jnp.maximum(m_old[g], jnp.max(tops[g], axis=0, keepdims=True)) for g in range(n_kv)]
        sums, pvs = [None] * n_kv, [None] * n_kv
        for start, size in subs:
            sub = slice(start, start + size)
            for g in range(n_kv):
                p = jnp.exp2(lg_refs[g][slot, sub, :] - m_new[g])
                pv = _dot_tn(v_ref[pl.ds(r0 + start, size), g * HEAD_DIM:(g + 1) * HEAD_DIM], p.astype(BF16))
                part = fold(p, jnp.sum)
                sums[g] = part if sums[g] is None else sums[g] + part
                pvs[g] = pv if pvs[g] is None else pvs[g] + pv
        for g in range(n_kv):
            alpha = jnp.exp2(m_old[g] - m_new[g])
            m_refs[g][...] = m_new[g]
            l_refs[g][...] = alpha * l_refs[g][...] + jnp.sum(sums[g], axis=0, keepdims=True)
            acc_refs[g][...] = acc_refs[g][...] * alpha + pvs[g]
        return next_tops

    lax.fori_loop(0, n_units, attend_unit, logits_of_unit(0, 0))
    for g in range(n_kv):
        for r in range(rep):
            sl = slice(r * LANES, (r + 1) * LANES)
            out = acc_refs[g][:, sl] * (1.0 / l_refs[g][:, sl])
            h = g * rep + r
            o_ref[:, h * HEAD_DIM:(h + 1) * HEAD_DIM] = jnp.transpose(out).astype(o_ref.dtype)


def _dsa_prompt(aq_t, iq_t, misc_t, iw_row, ik, k, v, n_rows, n_valid, topk):
    n_blk = n_rows // ROW_BLOCK
    n_heads = aq_t.shape[1] // HEAD_DIM
    n_kv = k.shape[1] // HEAD_DIM
    rep = n_heads // n_kv
    unit = _row_tile(n_rows, 640)
    blk = lambda i: (i, 0, 0)
    const = lambda i: (0, 0)
    kern = functools.partial(_dsa_prompt_kernel, unit=unit, topk=topk, n_valid=n_valid, n_heads=n_heads, n_kv=n_kv,
                             iw_row=iw_row)
    return pl.pallas_call(
        kern,
        out_shape=jax.ShapeDtypeStruct((aq_t.shape[0] * ROW_BLOCK, n_heads * HEAD_DIM), BF16),
        grid=(n_blk,),
        in_specs=[pl.BlockSpec((1,) + aq_t.shape[1:], blk), pl.BlockSpec((1,) + iq_t.shape[1:], blk),
                  pl.BlockSpec((1,) + misc_t.shape[1:], blk),
                  pl.BlockSpec((n_rows, ik.shape[1]), const), pl.BlockSpec((n_rows, k.shape[1]), const),
                  pl.BlockSpec((n_rows, v.shape[1]), const)],
        out_specs=pl.BlockSpec((ROW_BLOCK, n_heads * HEAD_DIM), lambda i: (i, 0)),
        scratch_shapes=[pltpu.VMEM((n_rows, LANES), I32)]
                       + [pltpu.VMEM((1, rep * LANES), F32)] * (2 * n_kv)
                       + [pltpu.VMEM((HEAD_DIM, rep * LANES), F32)] * n_kv
                       + [pltpu.VMEM((2, unit, rep * LANES), F32)] * n_kv,
        compiler_params=_params("parallel"),
        name="dsa_prompt",
    )(aq_t, iq_t, misc_t, ik, k, v)


Q_PAD = SUBLANES


def _idx_sample_kernel(pt_ref, iq_ref, w_ref, new_ref, *refs, n_pages):
    del pt_ref
    pages, o_ref = refs[:n_pages], refs[n_pages]
    iq = iq_ref[0]
    w = w_ref[0]

    def score(keys_t):
        s = jnp.maximum(_dot(iq, keys_t), 0.0) * w
        return jnp.sum(s.reshape(IDX_HEADS, Q_PAD, LANES), axis=0) * IDX_SCALE

    for p in range(n_pages):
        o_ref[0, :, p * LANES:(p + 1) * LANES] = score(pages[p][0].astype(BF16))
    o_ref[0, :, n_pages * LANES:] = score(new_ref[0])


def _idx_sample(page_table, iq, w, new_ik_t, cache_idx_t):
    db = iq.shape[0]
    n_pages = page_table.shape[1]
    page = cache_idx_t.shape[2]
    assert page == LANES
    seq = lambda b, pt: (b, 0, 0)
    in_specs = [pl.BlockSpec((1,) + iq.shape[1:], seq), pl.BlockSpec((1,) + w.shape[1:], seq),
                pl.BlockSpec((1,) + new_ik_t.shape[1:], seq)]
    in_specs += [pl.BlockSpec((1, IDX_DIM, page), lambda b, pt, p=p: (pt[b, p], 0, 0)) for p in range(n_pages)]
    n_keys = (n_pages + 1) * LANES
    return pl.pallas_call(
        functools.partial(_idx_sample_kernel, n_pages=n_pages),
        out_shape=jax.ShapeDtypeStruct((db, Q_PAD, n_keys), F32),
        grid_spec=pltpu.PrefetchScalarGridSpec(
            num_scalar_prefetch=1, grid=(db,), in_specs=in_specs,
            out_specs=pl.BlockSpec((1, Q_PAD, n_keys), seq)),
        compiler_params=_params("parallel"),
        name="idx_sample",
    )(page_table, iq, w, new_ik_t, *([cache_idx_t] * n_pages))


def _topk_sample_kernel(s_ref, o_ref, key_ref, *, unit, topk, past_len, dec_seq):
    n_keys = s_ref.shape[0]
    n_units = n_keys // unit
    row = lax.broadcasted_iota(I32, (unit, LANES), 0)
    q = lax.broadcasted_iota(I32, (unit, LANES), 1) % Q_PAD
    for u in range(n_units):
        k_pos = u * unit + row
        adm = ((q < dec_seq) & (k_pos <= past_len + q)) | ((q >= dec_seq) & (k_pos == 0))
        key_ref[u * unit:(u + 1) * unit, :] = jnp.where(adm, _sortable(s_ref[u * unit:(u + 1) * unit, :]), INT_MIN)
    tau, last = _select_topk(key_ref, n_units, unit, topk, n_keys)
    for u in range(n_units):
        sel = _selected(key_ref[u * unit:(u + 1) * unit, :], u * unit, tau, last)
        o_ref[u * unit:(u + 1) * unit, :] = jnp.where(sel, 0.0, MASKED)


def _topk_sample(scores_t, topk, past_len, dec_seq):
    n_keys, n_q = scores_t.shape
    unit = _row_tile(n_keys, 640)
    kern = functools.partial(_topk_sample_kernel, unit=unit, topk=topk, past_len=past_len, dec_seq=dec_seq)
    return pl.pallas_call(
        kern,
        out_shape=jax.ShapeDtypeStruct((n_keys, n_q), F32),
        grid=(n_q // LANES,),
        in_specs=[pl.BlockSpec((n_keys, LANES), lambda i: (0, i))],
        out_specs=pl.BlockSpec((n_keys, LANES), lambda i: (0, i)),
        scratch_shapes=[pltpu.VMEM((n_keys, LANES), I32)],
        compiler_params=_params("parallel"),
        name="topk_sample",
    )(scores_t)


def _attn_sample_kernel(pt_ref, q_ref, bias_ref, bias_new_ref, k_new_ref, v_new_ref, *refs, n_kv, pages_per_step):
    del pt_ref
    k_pages = refs[:pages_per_step]
    v_pages = refs[pages_per_step:2 * pages_per_step]
    o_ref, m_ref, l_ref, acc_ref = refs[2 * pages_per_step:]
    c = pl.program_id(1)
    rep = q_ref.shape[2] // Q_PAD

    @pl.when(c == 0)
    def _():
        m_ref[...] = jnp.full_like(m_ref, MASKED)
        l_ref[...] = jnp.zeros_like(l_ref)
        acc_ref[...] = jnp.zeros_like(acc_ref)

    def update(g, kg, vg, bias):
        sel = jnp.concatenate([bias] * rep, axis=0) > 0.5 * MASKED
        lg = jnp.where(sel, _dot_nt(q_ref[0, g], kg), MASKED)
        m_old = m_ref[g]
        m_new = jnp.maximum(m_old, jnp.max(lg, axis=1, keepdims=True))
        alpha = jnp.exp2(m_old - m_new)
        p = jnp.where(sel, jnp.exp2(lg - m_new), 0.0)
        l_ref[g] = alpha * l_ref[g] + jnp.sum(p, axis=1, keepdims=True)
        acc_ref[g] = alpha * acc_ref[g] + _dot(p.astype(BF16), vg)
        m_ref[g] = m_new

    def head(ref, g):
        return ref[0, :, g * HEAD_DIM:(g + 1) * HEAD_DIM].astype(BF16)

    @pl.when(c == 0)
    def _():
        for g in range(n_kv):
            update(g, head(k_new_ref, g), head(v_new_ref, g), bias_new_ref[0])

    for g in range(n_kv):
        kg = jnp.concatenate([_page_head(r, g, n_kv) for r in k_pages], axis=0)
        vg = jnp.concatenate([_page_head(r, g, n_kv) for r in v_pages], axis=0)
        update(g, kg, vg, bias_ref[0])

    @pl.when(c == pl.num_programs(1) - 1)
    def _():
        for g in range(n_kv):
            o_ref[0, g] = acc_ref[g] * (1.0 / l_ref[g])


ATTN_PAGES_PER_STEP = 64


def _attn_sample(page_table, q, bias, new_k, new_v, cache_k, cache_v):
    db, n_kv, rows, _ = q.shape
    n_pages = page_table.shape[1]
    page_rows = cache_k.shape[1]
    page = page_rows // n_kv
    per_step = max(d for d in range(1, ATTN_PAGES_PER_STEP + 1) if n_pages % d == 0)
    seq4 = lambda b, c, pt: (b, 0, 0, 0)
    seq3 = lambda b, c, pt: (b, 0, 0)
    in_specs = [pl.BlockSpec((1, n_kv, rows, HEAD_DIM), seq4),
                pl.BlockSpec((1, Q_PAD, per_step * page), lambda b, c, pt: (b, 0, c)),
                pl.BlockSpec((1, Q_PAD, page), lambda b, c, pt: (b, 0, n_pages)),
                pl.BlockSpec((1,) + new_k.shape[1:], seq3), pl.BlockSpec((1,) + new_v.shape[1:], seq3)]
    for _ in range(2):
        in_specs += [pl.BlockSpec((1, page_rows, HEAD_DIM), lambda b, c, pt, i=i: (pt[b, c * per_step + i], 0, 0))
                     for i in range(per_step)]
    return pl.pallas_call(
        functools.partial(_attn_sample_kernel, n_kv=n_kv, pages_per_step=per_step),
        out_shape=jax.ShapeDtypeStruct((db, n_kv, rows, HEAD_DIM), F32),
        grid_spec=pltpu.PrefetchScalarGridSpec(
            num_scalar_prefetch=1, grid=(db, n_pages // per_step), in_specs=in_specs,
            out_specs=pl.BlockSpec((1, n_kv, rows, HEAD_DIM), seq4),
            scratch_shapes=[pltpu.VMEM((n_kv, rows, 1), F32), pltpu.VMEM((n_kv, rows, 1), F32),
                            pltpu.VMEM((n_kv, rows, HEAD_DIM), F32)]),
        compiler_params=_params("parallel", "arbitrary"),
        name="attn_sample",
    )(page_table, q, bias, bias, new_k, new_v, *([cache_k] * per_step), *([cache_v] * per_step))


SB_EAGER_PAGES = 4
SB_PAGES_PER_STEP = 12


def _sb_sample(page_table, q, new_k, new_v, cache_k, cache_v, dec_seq):
    n_pages = page_table.shape[1]
    eager = min(SB_EAGER_PAGES, n_pages)
    o, carry = _sb_sample_call(page_table, q, (new_k, new_v), cache_k, cache_v, None, dec_seq,
                               n_pages - 1, 1, eager)
    rest = n_pages - eager
    if rest == 0:
        return o
    per_step = max(d for d in range(1, SB_PAGES_PER_STEP + 1) if rest % d == 0)
    finish = lambda: _sb_sample_call(page_table, q, None, cache_k, cache_v, (o, carry), dec_seq,
                                     rest - 1, rest // per_step, per_step)[0]
    return lax.cond(jnp.max(carry) >= SB_DONE2, finish, lambda: o)


def _pad_rows(a, rows):
    return jnp.pad(a, ((0, 0), (0, rows - a.shape[1])) + ((0, 0),) * (a.ndim - 2))


def kernel(x_prompt, x_sample, cache_a_k, cache_a_v, cache_a_idx_k, state_b, cache_c_k, cache_c_v, page_table,
           meta_tokens, norm_mix, norm_ffn, norm_final, w_in0, w_b_gate2, b_b_gate, b_head_norm, w_out0,
           w_in1, w_out1, w_ff1, w_ff2):
    batch, seq, d = x_prompt.shape
    assert batch == 1
    db, ds, _ = x_sample.shape
    assert ds <= Q_PAD and ds <= GLA_SUB
    t_p = N_META + seq
    t_pad = _round_up(t_p, ROW_BLOCK)
    n_s = db * ds
    m_s = _round_up(n_s, ROW_BLOCK)
    n_blk = t_pad // ROW_BLOCK
    n_pool, page = cache_a_k.shape[:2]
    n_pages = page_table.shape[1]
    past_len = n_pages * page
    topk_p = min(TOPK_MAX, seq // 4)
    topk_s = min(TOPK_MAX, (past_len + ds) // 4)
    srows = slice(t_pad, t_pad + n_s)

    h = jnp.concatenate([meta_tokens.astype(F32), x_prompt[0], jnp.zeros((t_pad - t_p, d), F32),
                         x_sample.reshape(n_s, d), jnp.zeros((m_s - n_s, d), F32)], axis=0)

    lay, groups = _even_layout(d)
    gate_w = _gate_weights(w_b_gate2, b_b_gate, lay)

    xn = _rmsnorm(h, norm_mix[0], BF16)
    proj = [_matmul(xn, jnp.pad(w_in0[:, lo:hi].astype(BF16), ((0, 0), (0, width - (hi - lo)))), F32)
            for lo, hi, width in groups]
    seg = lambda name: proj[lay[name][0]][:, lay[name][1]:lay[name][1] + lay[name][2]]
    aq_t, iq_t, misc_t, ak, av, ik, k_bf, v_bf, ik_bf = _rope(proj[lay["aq"][0]], lay, t_pad, past_len, ds)
    a_kv = k_bf.shape[1]
    n_heads_a = aq_t.shape[1] // HEAD_DIM
    rep_a = n_heads_a // A_KV_HEADS
    iw_row = lay["iw"][1] - lay["misc"][1]

    oa = _dsa_prompt(aq_t, iq_t, misc_t, iw_row, ik_bf, k_bf, v_bf, t_pad, t_p, topk_p)
    ob, state_p = _gla_prompt(proj, lay, t_pad, t_p, gate_w, b_head_norm)

    sample_rows = lambda a_t: a_t[n_blk:].transpose(0, 2, 1).reshape(m_s, a_t.shape[1])[:n_s]
    aq, iq = sample_rows(aq_t), sample_rows(iq_t)
    iw = sample_rows(misc_t[:, iw_row:iw_row + IDX_HEADS])
    iq_s = iq.reshape(db, ds, IDX_HEADS, IDX_DIM).transpose(0, 2, 1, 3)
    iq_s = jnp.pad(iq_s, ((0, 0), (0, 0), (0, Q_PAD - ds), (0, 0))).reshape(db, IDX_HEADS * Q_PAD, IDX_DIM)
    iw_s = iw.reshape(db, ds, IDX_HEADS).transpose(0, 2, 1)
    iw_s = jnp.pad(iw_s, ((0, 0), (0, 0), (0, Q_PAD - ds))).reshape(db, IDX_HEADS * Q_PAD, 1)
    iw_s = jnp.broadcast_to(iw_s, (db, IDX_HEADS * Q_PAD, LANES))
    new_ik_t = _pad_rows(ik_bf[srows].reshape(db, ds, IDX_DIM), page).transpose(0, 2, 1)
    scores = _idx_sample(page_table, iq_s, iw_s, new_ik_t, cache_a_idx_k.transpose(0, 2, 1))
    n_keys = scores.shape[2]
    n_q = _round_up(db * Q_PAD, LANES)
    scores_t = jnp.pad(scores.reshape(db * Q_PAD, n_keys).T, ((0, 0), (0, n_q - db * Q_PAD)))
    bias = _topk_sample(scores_t, topk_s, past_len, ds)[:, :db * Q_PAD].T.reshape(db, Q_PAD, n_keys)
    aq_s = aq.reshape(db, ds, A_KV_HEADS, rep_a, HEAD_DIM).transpose(0, 2, 3, 1, 4)
    aq_s = jnp.pad(aq_s, ((0, 0), (0, 0), (0, 0), (0, Q_PAD - ds), (0, 0))).reshape(db, A_KV_HEADS, rep_a * Q_PAD, HEAD_DIM)
    new_k = _pad_rows(k_bf[srows].reshape(db, ds, a_kv), page)
    new_v = _pad_rows(v_bf[srows].reshape(db, ds, a_kv), page)
    oa_s = _attn_sample(page_table, aq_s, bias, new_k, new_v,
                        cache_a_k.reshape(n_pool, page * A_KV_HEADS, HEAD_DIM),
                        cache_a_v.reshape(n_pool, page * A_KV_HEADS, HEAD_DIM))
    oa_s = oa_s.reshape(db, A_KV_HEADS, rep_a, Q_PAD, HEAD_DIM)[:, :, :, :ds].transpose(0, 3, 1, 2, 4)
    oa_s = oa_s.reshape(n_s, n_heads_a * HEAD_DIM).astype(BF16)

    chunk_s = lambda name: _pad_rows(seg(name)[srows].reshape(db, ds, lay[name][2]), GLA_SUB)
    ob_s, state_s = _gla_sample(chunk_s("bq"), chunk_s("bk"), chunk_s("bv"), chunk_s("br"), chunk_s("glr"),
                                state_b, ds, gate_w, b_head_norm)
    ob_s = ob_s[:, :ds].reshape(n_s, -1)

    oa = oa.at[srows].set(oa_s)
    ob = ob.at[srows].set(ob_s)
    h, xn = _proj_res_norm([oa, ob], w_out0.astype(BF16), h, norm_ffn[0])
    w_ff1_bf, w_ff2_bf = w_ff1.astype(BF16), w_ff2.astype(BF16)
    h, xn = _mlp(xn, h, w_ff1_bf, w_ff2_bf, 0, norm_mix[1], BF16)

    c_kv = C_KV_HEADS * HEAD_DIM
    c_q = w_in1.shape[1] - 2 * c_kv
    rep_c = c_q // c_kv
    q1 = _matmul(xn, w_in1[:, :c_q].astype(BF16), BF16, scale=SB_Q_SCALE)
    ck, cv, ck_bf, cv_bf = _kv_proj(xn, w_in1[:, c_q:].astype(BF16), C_KV_HEADS)
    oc = _sb_prompt(q1, ck_bf, cv_bf, t_pad)
    q1_s = q1[srows].reshape(db, ds, C_KV_HEADS, rep_c, HEAD_DIM).transpose(0, 2, 3, 1, 4)
    q1_s = q1_s.reshape(db, C_KV_HEADS, rep_c * ds, HEAD_DIM)
    new_ck = _pad_rows(ck_bf[srows].reshape(db, ds, c_kv), page)
    new_cv = _pad_rows(cv_bf[srows].reshape(db, ds, c_kv), page)
    oc_s = _sb_sample(page_table, q1_s, new_ck, new_cv, cache_c_k.reshape(n_pool, page * C_KV_HEADS, HEAD_DIM),
                      cache_c_v.reshape(n_pool, page * C_KV_HEADS, HEAD_DIM), ds)
    oc_s = oc_s.reshape(db, C_KV_HEADS, rep_c, ds, HEAD_DIM).transpose(0, 3, 1, 2, 4).reshape(n_s, c_q).astype(BF16)
    h, xn = _proj_res_norm([oc.at[srows].set(oc_s)], w_out1.astype(BF16), h, norm_ffn[1])
    h, y = _mlp(xn, h, w_ff1_bf, w_ff2_bf, 1, norm_final, F32)

    y_prompt = y[N_META:t_p][None]
    y_sample = y[srows].reshape(db, ds, d)
    kv_p = lambda a, heads: a[:t_p * heads].reshape(1, t_p, heads, HEAD_DIM)
    kv_s = lambda a, heads: a[t_pad * heads:(t_pad + n_s) * heads].reshape(db, ds, heads, HEAD_DIM)
    return (y_prompt, y_sample,
            kv_p(ak, A_KV_HEADS), kv_p(av, A_KV_HEADS), ik[:t_p, :IDX_DIM][None], state_p[None],
            kv_p(ck, C_KV_HEADS), kv_p(cv, C_KV_HEADS),
            kv_s(ak, A_KV_HEADS), kv_s(av, A_KV_HEADS), ik[srows, :IDX_DIM].reshape(db, ds, IDX_DIM),
            state_s, kv_s(ck, C_KV_HEADS), kv_s(cv, C_KV_HEADS))
```

```python
import functools
import math

import jax
import jax.numpy as jnp
from jax import lax
from jax.experimental import pallas as pl
from jax.experimental.pallas import tpu as pltpu

F32 = jnp.float32
BF16 = jnp.bfloat16
I32 = jnp.int32

N_META = 16
HEAD_DIM = 128
ROPE_THETA = 10000.0
EPS = 1e-6
A_KV_HEADS = 2
IDX_HEADS = 16
IDX_DIM = 64
TOPK_MAX = 256
B_HEADS = 4
B_GATE_RANK = 16
B_GATE_TAU = 16.0
C_KV_HEADS = 4

LANES = 128
SUBLANES = 8
VMEM_LIMIT_BYTES = 56 * 1024 * 1024

ROW_BLOCK = 128
INT_MIN = -(2 ** 31)


def _round_up(x, m):
    return (x + m - 1) // m * m


def _row_tile(m, cap):
    best = ROW_BLOCK
    for k in range(1, m // ROW_BLOCK + 1):
        t = k * ROW_BLOCK
        if t <= cap and m % t == 0:
            best = t
    return best


def _col_tile(n, cap):
    best = LANES
    for k in range(1, n // LANES + 1):
        t = k * LANES
        if t <= cap and n % t == 0:
            best = t
    return best


def _params(*sem):
    return pltpu.CompilerParams(dimension_semantics=sem, vmem_limit_bytes=VMEM_LIMIT_BYTES)


def _rms(x, g):
    ms = jnp.mean(x * x, axis=-1, keepdims=True)
    return x * lax.rsqrt(ms + EPS) * g


def _rmsnorm_kernel(x_ref, g_ref, o_ref):
    o_ref[...] = _rms(x_ref[...], g_ref[...]).astype(o_ref.dtype)


def _rmsnorm(x, g, out_dtype):
    m, d = x.shape
    tm = _row_tile(m, 512)
    return pl.pallas_call(
        _rmsnorm_kernel,
        out_shape=jax.ShapeDtypeStruct((m, d), out_dtype),
        grid=(m // tm,),
        in_specs=[pl.BlockSpec((tm, d), lambda i: (i, 0)), pl.BlockSpec((1, d), lambda i: (0, 0))],
        out_specs=pl.BlockSpec((tm, d), lambda i: (i, 0)),
        compiler_params=_params("parallel"),
        name="rmsnorm",
    )(x, g.reshape(1, d))


MATMUL_COL_TILE = 1152


def _matmul_kernel(x_ref, w_ref, o_ref, *, scale):
    acc = jnp.dot(x_ref[...], w_ref[...], preferred_element_type=F32)
    o_ref[...] = (acc if scale is None else acc * scale).astype(o_ref.dtype)


def _matmul(x, w, out_dtype, scale=None):
    m, k = x.shape
    n = w.shape[1]
    tm = _row_tile(m, 1408)
    tn = _col_tile(n, MATMUL_COL_TILE)
    return pl.pallas_call(
        functools.partial(_matmul_kernel, scale=scale),
        out_shape=jax.ShapeDtypeStruct((m, n), out_dtype),
        grid=(m // tm, n // tn),
        in_specs=[pl.BlockSpec((tm, k), lambda i, j: (i, 0)), pl.BlockSpec((k, tn), lambda i, j: (0, j))],
        out_specs=pl.BlockSpec((tm, tn), lambda i, j: (i, j)),
        compiler_params=_params("parallel", "parallel"),
        name="matmul",
    )(x, w)


def _kv_proj_kernel(x_ref, w_ref, k_ref, v_ref, kbf_ref, vbf_ref, *, n_heads):
    acc = jnp.dot(x_ref[...], w_ref[...], preferred_element_type=F32)
    tm = x_ref.shape[0]
    ckv = n_heads * HEAD_DIM
    for h in range(n_heads):
        k_ref[pl.ds(h, tm, stride=n_heads), :] = acc[:, h * HEAD_DIM:(h + 1) * HEAD_DIM]
        v_ref[pl.ds(h, tm, stride=n_heads), :] = acc[:, ckv + h * HEAD_DIM:ckv + (h + 1) * HEAD_DIM]
    kbf_ref[...] = acc[:, :ckv].astype(BF16)
    vbf_ref[...] = acc[:, ckv:].astype(BF16)


def _kv_proj(x, w, n_heads):
    m, k = x.shape
    ckv = n_heads * HEAD_DIM
    assert w.shape == (k, 2 * ckv)
    tm = _row_tile(m, 768)
    row = lambda i: (i, 0)
    return pl.pallas_call(
        functools.partial(_kv_proj_kernel, n_heads=n_heads),
        out_shape=(jax.ShapeDtypeStruct((m * n_heads, HEAD_DIM), F32), jax.ShapeDtypeStruct((m * n_heads, HEAD_DIM), F32),
                   jax.ShapeDtypeStruct((m, ckv), BF16), jax.ShapeDtypeStruct((m, ckv), BF16)),
        grid=(m // tm,),
        in_specs=[pl.BlockSpec((tm, k), row), pl.BlockSpec((k, 2 * ckv), lambda i: (0, 0))],
        out_specs=(pl.BlockSpec((tm * n_heads, HEAD_DIM), row), pl.BlockSpec((tm * n_heads, HEAD_DIM), row),
                   pl.BlockSpec((tm, ckv), row), pl.BlockSpec((tm, ckv), row)),
        compiler_params=_params("parallel"),
        name="kv_proj",
    )(x, w)


def _proj_res_norm_kernel(*refs, n_in):
    x_refs, w_refs = refs[:n_in], refs[n_in:2 * n_in]
    h_ref, g_ref, hout_ref, xn_ref = refs[2 * n_in:]
    hn = h_ref[...]
    for x_ref, w_ref in zip(x_refs, w_refs):
        hn = hn + jnp.dot(x_ref[...], w_ref[...], preferred_element_type=F32)
    hout_ref[...] = hn
    xn_ref[...] = _rms(hn, g_ref[...]).astype(xn_ref.dtype)


def _proj_res_norm(xs, w, h, g):
    m, k = xs[0].shape
    assert all(x.shape == (m, k) for x in xs) and w.shape[0] == k * len(xs)
    d = w.shape[1]
    tm = _row_tile(m, 384)
    row = lambda i: (i, 0)
    in_specs = [pl.BlockSpec((tm, k), row) for _ in xs]
    in_specs += [pl.BlockSpec((k, d), lambda i, j=j: (j, 0)) for j in range(len(xs))]
    in_specs += [pl.BlockSpec((tm, d), row), pl.BlockSpec((1, d), lambda i: (0, 0))]
    return pl.pallas_call(
        functools.partial(_proj_res_norm_kernel, n_in=len(xs)),
        out_shape=(jax.ShapeDtypeStruct((m, d), F32), jax.ShapeDtypeStruct((m, d), BF16)),
        grid=(m // tm,),
        in_specs=in_specs,
        out_specs=(pl.BlockSpec((tm, d), row), pl.BlockSpec((tm, d), row)),
        compiler_params=_params("parallel"),
        name="proj_res_norm",
    )(*xs, *([w] * len(xs)), h, g.reshape(1, d))


def _mlp_kernel(xn_ref, h_ref, w1_ref, w2_ref, g_ref, hout_ref, xnout_ref):
    f = pl.program_id(1)

    @pl.when(f == 0)
    def _():
        hout_ref[...] = h_ref[...]

    a = jnp.maximum(jnp.dot(xn_ref[...], w1_ref[...], preferred_element_type=F32), 0.0)
    hout_ref[...] += jnp.dot((a * a).astype(BF16), w2_ref[...], preferred_element_type=F32)

    @pl.when(f == pl.num_programs(1) - 1)
    def _():
        xnout_ref[...] = _rms(hout_ref[...], g_ref[...]).astype(xnout_ref.dtype)


def _mlp(xn, h, w1, w2, layer, g, norm_dtype):
    m, d = xn.shape
    dff = w1.shape[2]
    tm = _row_tile(m, 768)
    tf = _col_tile(dff, 1024)
    return pl.pallas_call(
        _mlp_kernel,
        out_shape=(jax.ShapeDtypeStruct((m, d), F32), jax.ShapeDtypeStruct((m, d), norm_dtype)),
        grid=(m // tm, dff // tf),
        in_specs=[pl.BlockSpec((tm, d), lambda i, f: (i, 0)),
                  pl.BlockSpec((tm, d), lambda i, f: (i, 0), pipeline_mode=pl.Buffered(1)),
                  pl.BlockSpec((None, d, tf), lambda i, f: (layer, 0, f)),
                  pl.BlockSpec((None, tf, d), lambda i, f: (layer, f, 0)),
                  pl.BlockSpec((1, d), lambda i, f: (0, 0))],
        out_specs=(pl.BlockSpec((tm, d), lambda i, f: (i, 0)), pl.BlockSpec((tm, d), lambda i, f: (i, 0))),
        compiler_params=_params("parallel", "arbitrary"),
        name="mlp",
    )(xn, h, w1, w2, g.reshape(1, d))


def _even_layout(d_model):
    a_q = d_model // 2
    a_kv = A_KV_HEADS * HEAD_DIM
    idx_q = IDX_HEADS * IDX_DIM
    b_v = d_model // 2
    b_qk = b_v // 2
    sizes = dict(aq=a_q, ak=a_kv, av=a_kv, iq=idx_q, ik=IDX_DIM, iw=IDX_HEADS, bq=b_qk, bk=b_qk, bv=b_v,
                 bglr=B_GATE_RANK, br=b_v)
    start, off = {}, 0
    for name, size in sizes.items():
        start[name] = off
        off += size
    firsts = ("aq", "bq", "br")
    bounds = [start[n] for n in firsts] + [off]
    lay, groups = {}, []
    for g in range(len(firsts)):
        lo, hi = bounds[g], bounds[g + 1]
        width = _round_up(hi - lo, LANES)
        while _col_tile(width, MATMUL_COL_TILE) < MATMUL_COL_TILE // 2:
            width += LANES
        groups.append((lo, hi, width))
        for name, size in sizes.items():
            if lo <= start[name] < hi:
                lay[name] = (g, start[name] - lo, size)
    lay["misc"] = (lay["ik"][0], lay["ik"][1], LANES)
    lay["glr"] = (lay["bglr"][0], lay["bglr"][1], LANES)
    for name in ("aq", "ak", "av", "iq", "misc", "bq", "bk", "bv", "glr", "br"):
        assert lay[name][1] % LANES == 0
    return lay, groups


def _rope_freqs():
    def inv(d):
        return ROPE_THETA ** (-jnp.arange(d // 2, dtype=F32) * (2.0 / d))
    pad = jnp.zeros((LANES - HEAD_DIM // 2 - IDX_DIM // 2,), F32)
    return jnp.concatenate([inv(HEAD_DIM), inv(IDX_DIM), pad]).reshape(1, LANES)


def _rope_kernel(*refs, n_aq, n_iq, n_ak, n_iq_in, n_prompt_rows, past_len, dec_seq):
    q_ref, kv_ref = refs[:2]
    iq_in = refs[2:2 + n_iq_in]
    misc_in, inv_ref = refs[2 + n_iq_in:4 + n_iq_in]
    aq_ref, iq_ref, misc_ref, ak_ref, av_ref, ik_ref, kbf_ref, vbf_ref, ikbf_ref = refs[4 + n_iq_in:]
    tr = q_ref.shape[0]
    iq_per_in = n_iq // n_iq_in
    half, quarter = HEAD_DIM // 2, IDX_DIM // 2
    row = pl.program_id(0) * tr + lax.broadcasted_iota(I32, (tr, 1), 0)
    pos = jnp.where(row < n_prompt_rows, row, past_len + (row - n_prompt_rows) % dec_seq).astype(F32)
    ang = pos * inv_ref[...]
    c, s = jnp.cos(ang), jnp.sin(ang)
    cos128 = jnp.concatenate([c[:, :half]] * 2, axis=1)
    sin128 = jnp.concatenate([-s[:, :half], s[:, :half]], axis=1)
    c_i, s_i = c[:, half:half + quarter], s[:, half:half + quarter]
    cos64 = jnp.concatenate([c_i] * (LANES // quarter), axis=1)
    sin64 = jnp.concatenate([-s_i, s_i] * (LANES // IDX_DIM), axis=1)

    def rope128(x):
        return x * cos128 + pltpu.roll(x, half, 1) * sin128

    lane = lax.broadcasted_iota(I32, (tr, LANES), 1)
    low_half = (lane % IDX_DIM) < quarter

    def rope64(x):
        rot = jnp.where(low_half, pltpu.roll(x, LANES - quarter, 1), pltpu.roll(x, quarter, 1))
        return x * cos64 + rot * sin64

    n_blk = tr // ROW_BLOCK

    def put_transposed(ref, c, x):
        for b in range(n_blk):
            tile = jnp.transpose(x[b * ROW_BLOCK:(b + 1) * ROW_BLOCK, :])
            ref[b, c * LANES:(c + 1) * LANES, :] = tile.astype(ref.dtype)

    for c in range(n_aq):
        sl = slice(c * LANES, (c + 1) * LANES)
        put_transposed(aq_ref, c, rope128(q_ref[:, sl]) * EXP2_SCALE)
    for c in range(n_iq):
        src = iq_in[c // iq_per_in]
        put_transposed(iq_ref, c, rope64(src[:, (c % iq_per_in) * LANES:(c % iq_per_in + 1) * LANES]))
    put_transposed(misc_ref, 0, misc_in[...])
    for c in range(n_ak):
        sl = slice(c * LANES, (c + 1) * LANES)
        kr = rope128(kv_ref[:, sl])
        vr = kv_ref[:, n_ak * LANES + c * LANES:n_ak * LANES + (c + 1) * LANES]
        ak_ref[pl.ds(c, tr, stride=n_ak), :] = kr
        av_ref[pl.ds(c, tr, stride=n_ak), :] = vr
        kbf_ref[:, sl] = kr.astype(BF16)
        vbf_ref[:, sl] = vr.astype(BF16)
    ik = rope64(misc_in[...])
    ik_ref[...] = ik
    ikbf_ref[...] = ik[:, :IDX_DIM].astype(BF16)


def _rope(p0, lay, n_prompt_rows, past_len, dec_seq):
    m = p0.shape[0]
    tr = _row_tile(m, 384)
    a_q, a_kv, idx_q = lay["aq"][2], lay["ak"][2], lay["iq"][2]
    n_ak = a_kv // LANES
    assert lay["av"][1] == lay["ak"][1] + a_kv and lay["ak"][1] % (2 * a_kv) == 0 and lay["aq"][1] % a_q == 0
    iq_w = math.gcd(lay["iq"][1], idx_q)
    n_iq_in = idx_q // iq_w
    kern = functools.partial(_rope_kernel, n_aq=a_q // LANES, n_iq=idx_q // LANES, n_ak=n_ak, n_iq_in=n_iq_in,
                             n_prompt_rows=n_prompt_rows, past_len=past_len, dec_seq=dec_seq)
    cols = lambda off, width: (lambda i: (i, off // width))
    in_specs = [pl.BlockSpec((tr, a_q), cols(lay["aq"][1], a_q)),
                pl.BlockSpec((tr, 2 * a_kv), cols(lay["ak"][1], 2 * a_kv))]
    in_specs += [pl.BlockSpec((tr, iq_w), cols(lay["iq"][1] + j * iq_w, iq_w)) for j in range(n_iq_in)]
    in_specs += [pl.BlockSpec((tr, LANES), cols(lay["misc"][1], LANES)), pl.BlockSpec((1, LANES), lambda i: (0, 0))]
    row = lambda i: (i, 0)
    blk = lambda i: (i, 0, 0)
    nb, tb = m // ROW_BLOCK, tr // ROW_BLOCK
    return pl.pallas_call(
        kern,
        out_shape=(jax.ShapeDtypeStruct((nb, a_q, ROW_BLOCK), BF16), jax.ShapeDtypeStruct((nb, idx_q, ROW_BLOCK), BF16),
                   jax.ShapeDtypeStruct((nb, LANES, ROW_BLOCK), F32),
                   jax.ShapeDtypeStruct((m * n_ak, LANES), F32), jax.ShapeDtypeStruct((m * n_ak, LANES), F32),
                   jax.ShapeDtypeStruct((m, LANES), F32),
                   jax.ShapeDtypeStruct((m, a_kv), BF16), jax.ShapeDtypeStruct((m, a_kv), BF16),
                   jax.ShapeDtypeStruct((m, IDX_DIM), BF16)),
        grid=(m // tr,),
        in_specs=in_specs,
        out_specs=(pl.BlockSpec((tb, a_q, ROW_BLOCK), blk), pl.BlockSpec((tb, idx_q, ROW_BLOCK), blk),
                   pl.BlockSpec((tb, LANES, ROW_BLOCK), blk),
                   pl.BlockSpec((tr * n_ak, LANES), row), pl.BlockSpec((tr * n_ak, LANES), row),
                   pl.BlockSpec((tr, LANES), row), pl.BlockSpec((tr, a_kv), row), pl.BlockSpec((tr, a_kv), row),
                   pl.BlockSpec((tr, IDX_DIM), row)),
        compiler_params=_params("parallel"),
        name="rope",
    )(*([p0] * (3 + n_iq_in)), _rope_freqs())


GLA_SUB = 16


def _split_bf16(x):
    hi = x.astype(BF16)
    return hi, (x - hi.astype(F32)).astype(BF16)


def _log_sigmoid(x):
    return jnp.minimum(x, 0.0) - jnp.log(1.0 + jnp.exp(-jnp.abs(x)))


def _dot(a, b):
    return jnp.dot(a, b, preferred_element_type=F32)


def _dot_nt(a, b):
    return lax.dot_general(a, b, (((1,), (1,)), ((), ())), preferred_element_type=F32)


def _dot_tn(a, b):
    return lax.dot_general(a, b, (((0,), (0,)), ((), ())), preferred_element_type=F32)


def _gla_gate(misc, w2_hi, w2_lo, bias, valid):
    m_hi, m_lo = _split_bf16(misc)
    x = _dot(m_hi, w2_hi) + _dot(m_lo, w2_hi) + _dot(m_hi, w2_lo) + bias
    return jnp.where(valid, _log_sigmoid(x) * (1.0 / B_GATE_TAU), 0.0)


def _gla_chunks(chunks, state_ref, dk, dv):
    heads = range(B_HEADS)
    local = [_gla_local(q, k, v, g, dk, dv) for q, k, v, g in chunks]
    states = [state_ref[h] for h in heads]
    outs = []
    for intra, q_in, grow, decay in local:
        outs.append([intra[h] + _dot(q_in[h], states[h].astype(BF16)) for h in heads])
        states = [decay[h] * states[h] + grow[h] for h in heads]
    for h in heads:
        state_ref[h] = states[h]
    return outs


def _gla_local(q, k, v, g, dk, dv):
    c = q.shape[0]
    n_sub = c // GLA_SUB
    ri = lax.broadcasted_iota(I32, (c, c), 0)
    ci = lax.broadcasted_iota(I32, (c, c), 1)
    causal = ri >= ci
    tri = jnp.where(causal, 1.0, 0.0).astype(BF16)
    gcum = _dot(jnp.concatenate([tri, tri], axis=1), jnp.concatenate(_split_bf16(g), axis=0))
    scale = dk ** -0.5
    pad = _round_up(c, LANES)
    heads = range(B_HEADS)
    gh = [gcum[:, h * dk:(h + 1) * dk] for h in heads]
    qh = [q[:, h * dk:(h + 1) * dk] * scale for h in heads]
    kh = [k[:, h * dk:(h + 1) * dk] for h in heads]
    vh = [v[:, h * dv:(h + 1) * dv].astype(BF16) for h in heads]
    last = [gh[h][c - 1:c, :] for h in heads]
    qts, kts = [], []
    for h in heads:
        q_parts, k_parts = [], []
        for i in range(n_sub):
            lo, hi = i * GLA_SUB, (i + 1) * GLA_SUB
            base = gh[h][lo - 1:lo, :] if i else jnp.zeros((1, dk), F32)
            q_parts.append(qh[h][lo:hi] * jnp.exp(gh[h][lo:hi] - base))
            k_parts.append(kh[h][:hi] * jnp.exp(base - gh[h][:hi]))
            if pad > hi:
                k_parts.append(jnp.zeros((pad - hi, dk), F32))
        qts.append(jnp.concatenate(q_parts, axis=0).astype(BF16))
        kts.append(jnp.concatenate(k_parts, axis=0).astype(BF16))
    q_in = [(qh[h] * jnp.exp(gh[h])).astype(BF16) for h in heads]
    k_tail = [(kh[h] * jnp.exp(last[h] - gh[h])).astype(BF16) for h in heads]
    rs = [_dot_nt(qts[h], kts[h]) for h in heads]
    grow = [_dot_tn(k_tail[h], vh[h]) for h in heads]
    intra, decay = [], []
    for h in heads:
        att = jnp.concatenate([rs[h][i * GLA_SUB:(i + 1) * GLA_SUB, i * pad:i * pad + c] for i in range(n_sub)],
                              axis=0)
        att = jnp.where(causal, att, 0.0).astype(BF16)
        intra.append(_dot(att, vh[h]))
        d = jnp.transpose(jnp.broadcast_to(jnp.exp(last[h]), (dk, dk)))
        decay.append(jnp.concatenate([d] * (dv // dk), axis=1))
    return intra, q_in, grow, decay


def _gla_out(outs, r, head_norm, dv):
    res = []
    for h, o in enumerate(outs):
        gate = r[:, h * dv:(h + 1) * dv]
        gate = gate * (1.0 / (1.0 + jnp.exp(-gate)))
        res.append(_rms(o, head_norm) * gate)
    return jnp.concatenate(res, axis=1)


def _gla_prompt_kernel(q_ref, k_ref, v_ref, r_ref, misc_ref, w2h_ref, w2l_ref, b_ref, hn_ref, o_ref, state_ref,
                       *, chunk, n_valid, dk, dv):
    blk = pl.program_id(0)

    @pl.when(blk == 0)
    def _():
        state_ref[...] = jnp.zeros_like(state_ref)

    rows = q_ref.shape[0]
    row = blk * rows + lax.broadcasted_iota(I32, (rows, 1), 0)
    g = _gla_gate(misc_ref[...], w2h_ref[...], w2l_ref[...], b_ref[...], row < n_valid)
    slices = [slice(c0, c0 + chunk) for c0 in range(0, rows, chunk)]
    outs = _gla_chunks([(q_ref[sl, :], k_ref[sl, :], v_ref[sl, :], g[sl, :]) for sl in slices], state_ref, dk, dv)
    for sl, out in zip(slices, outs):
        o_ref[sl, :] = _gla_out(out, r_ref[sl, :], hn_ref[...], dv).astype(o_ref.dtype)


def _gla_sample_kernel(q_ref, k_ref, v_ref, r_ref, misc_ref, w2h_ref, w2l_ref, b_ref, hn_ref, s_ref, o_ref,
                       state_ref, *, n_valid, dk, dv):
    state_ref[0] = s_ref[0]
    rows = q_ref.shape[1]
    row = lax.broadcasted_iota(I32, (rows, 1), 0)
    g = _gla_gate(misc_ref[0], w2h_ref[...], w2l_ref[...], b_ref[...], row < n_valid)
    outs, = _gla_chunks([(q_ref[0], k_ref[0], v_ref[0], g)], state_ref.at[0], dk, dv)
    o_ref[0] = _gla_out(outs, r_ref[0], hn_ref[...], dv).astype(o_ref.dtype)


def _gate_weights(w_b_gate2, b_b_gate, lay):
    off = lay["bglr"][1] - lay["glr"][1]
    w2 = jnp.zeros((LANES, w_b_gate2.shape[1]), F32).at[off:off + B_GATE_RANK].set(w_b_gate2)
    hi = w2.astype(BF16)
    lo = (w2 - hi.astype(F32)).astype(BF16)
    return hi, lo, b_b_gate.reshape(1, -1)


def _gla_prompt(proj, lay, n_rows, n_valid, gate_w, head_norm, chunk=64):
    bqk, bv = lay["bq"][2], lay["bv"][2]
    dk, dv = bqk // B_HEADS, bv // B_HEADS
    w2h, w2l, bias = gate_w
    names = ("bq", "bk", "bv", "br", "glr")
    for name in names:
        assert lay[name][1] % lay[name][2] == 0
    col = lambda name: (lambda i, o=lay[name][1] // lay[name][2]: (i, o))
    const = lambda i: (0, 0)
    kern = functools.partial(_gla_prompt_kernel, chunk=chunk, n_valid=n_valid, dk=dk, dv=dv)
    return pl.pallas_call(
        kern,
        out_shape=(jax.ShapeDtypeStruct((proj[0].shape[0], bv), BF16), jax.ShapeDtypeStruct((B_HEADS, dk, dv), F32)),
        grid=(n_rows // ROW_BLOCK,),
        in_specs=[pl.BlockSpec((ROW_BLOCK, bqk), col("bq")), pl.BlockSpec((ROW_BLOCK, bqk), col("bk")),
                  pl.BlockSpec((ROW_BLOCK, bv), col("bv")), pl.BlockSpec((ROW_BLOCK, bv), col("br")),
                  pl.BlockSpec((ROW_BLOCK, LANES), col("glr")),
                  pl.BlockSpec(w2h.shape, const), pl.BlockSpec(w2l.shape, const), pl.BlockSpec(bias.shape, const),
                  pl.BlockSpec((1, dv), const)],
        out_specs=(pl.BlockSpec((ROW_BLOCK, bv), lambda i: (i, 0)),
                   pl.BlockSpec((B_HEADS, dk, dv), lambda i: (0, 0, 0))),
        compiler_params=_params("arbitrary"),
        name="gla_prompt",
    )(*[proj[lay[name][0]] for name in names], w2h, w2l, bias, head_norm.reshape(1, dv))


def _gla_sample(q, k, v, r, misc, state, n_valid, gate_w, head_norm):
    db, rows, bqk = q.shape
    bv = v.shape[2]
    dk, dv = bqk // B_HEADS, bv // B_HEADS
    w2h, w2l, bias = gate_w
    seq = lambda i: (i, 0, 0)
    const = lambda i: (0, 0)
    kern = functools.partial(_gla_sample_kernel, n_valid=n_valid, dk=dk, dv=dv)
    return pl.pallas_call(
        kern,
        out_shape=(jax.ShapeDtypeStruct((db, rows, bv), BF16), jax.ShapeDtypeStruct(state.shape, F32)),
        grid=(db,),
        in_specs=[pl.BlockSpec((1, rows, bqk), seq), pl.BlockSpec((1, rows, bqk), seq),
                  pl.BlockSpec((1, rows, bv), seq), pl.BlockSpec((1, rows, bv), seq),
                  pl.BlockSpec((1, rows, LANES), seq),
                  pl.BlockSpec(w2h.shape, const), pl.BlockSpec(w2l.shape, const), pl.BlockSpec(bias.shape, const),
                  pl.BlockSpec((1, dv), const),
                  pl.BlockSpec((1, B_HEADS, dk, dv), lambda i: (i, 0, 0, 0))],
        out_specs=(pl.BlockSpec((1, rows, bv), seq), pl.BlockSpec((1, B_HEADS, dk, dv), lambda i: (i, 0, 0, 0))),
        compiler_params=_params("parallel"),
        name="gla_sample",
    )(q, k, v, r, misc, w2h, w2l, bias, head_norm.reshape(1, dv), state)


SB_DONE = -105.0
SB_Q_SCALE = HEAD_DIM ** -0.5 * math.log2(math.e)
SB_DONE2 = SB_DONE * math.log2(math.e)


def _sb_suffix_matrix():
    j = lax.broadcasted_iota(I32, (2 * LANES, 2 * LANES), 0) % LANES
    s = lax.broadcasted_iota(I32, (2 * LANES, 2 * LANES), 1)
    return jnp.where((j > s) | (s >= LANES), 1.0, 0.0).astype(BF16)


def _sb_blocks(qs, ks, vs, strict, carries, accs, suffix):
    zs = [_dot_nt(q, k) for q, k in zip(qs, ks)]
    keeps, takes = [], []
    for z in zs:
        soft = jnp.log2(1.0 + jnp.exp2(-jnp.abs(z)))
        log_keep = -(jnp.maximum(z, 0.0) + soft)
        keeps.append(log_keep if strict is None else jnp.where(strict, log_keep, 0.0))
        takes.append(jnp.minimum(z, 0.0) - soft)
    sums = []
    for lk in keeps:
        sums.append(_dot(jnp.concatenate(_split_bf16(lk), axis=1), suffix))
    ws = []
    for take, s, c in zip(takes, sums, carries):
        w = jnp.exp2(take + (s[:, :LANES] + c))
        ws.append((w if strict is None else jnp.where(strict, w, 0.0)).astype(BF16))
    new_carries = [c + s[:, LANES:] for c, s in zip(carries, sums)]
    new_accs = [a + _dot(w, v) for a, w, v in zip(accs, ws, vs)]
    return new_carries, new_accs


def _sb_prompt_kernel(q_ref, k_ref, v_ref, o_ref, *scratch, n_kv, rep):
    carry_refs, acc_refs = scratch[:n_kv], scratch[n_kv:]
    b = pl.program_id(0)
    suffix = _sb_suffix_matrix()
    for ref in scratch:
        ref[...] = jnp.zeros_like(ref)
    rows = rep * ROW_BLOCK
    q_row = lax.broadcasted_iota(I32, (rows, LANES), 0) % ROW_BLOCK
    col = lax.broadcasted_iota(I32, (rows, LANES), 1)
    head = lambda g: slice(g * HEAD_DIM, (g + 1) * HEAD_DIM)

    def visit(kb, strict):
        k0 = pl.multiple_of(kb * ROW_BLOCK, ROW_BLOCK)
        qs = [jnp.concatenate([q_ref[:, head(g * rep + r)] for r in range(rep)], axis=0) for g in range(n_kv)]
        ks = [k_ref[pl.ds(k0, ROW_BLOCK), head(g)] for g in range(n_kv)]
        vs = [v_ref[pl.ds(k0, ROW_BLOCK), head(g)] for g in range(n_kv)]
        carries, accs = _sb_blocks(qs, ks, vs, strict, [r[...] for r in carry_refs], [r[...] for r in acc_refs],
                                   suffix)
        top = None
        for g in range(n_kv):
            carry_refs[g][...] = carries[g]
            acc_refs[g][...] = accs[g]
            part = _col_reduce(carries[g], jnp.max)
            top = part if top is None else jnp.maximum(top, part)
        return (jnp.max(top) < SB_DONE2).astype(I32)

    done = visit(b, col < q_row)
    lax.while_loop(lambda st: (st[0] >= 0) & (st[1] == 0), lambda st: (st[0] - 1, visit(st[0], None)), (b - 1, done))
    for g in range(n_kv):
        for r in range(rep):
            o_ref[:, head(g * rep + r)] = acc_refs[g][r * ROW_BLOCK:(r + 1) * ROW_BLOCK, :].astype(o_ref.dtype)


def _sb_prompt(q, k, v, n_rows):
    cq = q.shape[1]
    ckv = k.shape[1]
    n_kv = ckv // HEAD_DIM
    rep = cq // ckv
    kern = functools.partial(_sb_prompt_kernel, n_kv=n_kv, rep=rep)
    return pl.pallas_call(
        kern,
        out_shape=jax.ShapeDtypeStruct(q.shape, BF16),
        grid=(n_rows // ROW_BLOCK,),
        in_specs=[pl.BlockSpec((ROW_BLOCK, cq), lambda i: (i, 0)),
                  pl.BlockSpec((n_rows, ckv), lambda i: (0, 0)),
                  pl.BlockSpec((n_rows, ckv), lambda i: (0, 0))],
        out_specs=pl.BlockSpec((ROW_BLOCK, cq), lambda i: (i, 0)),
        scratch_shapes=[pltpu.VMEM((rep * ROW_BLOCK, LANES), F32)] * n_kv
                       + [pltpu.VMEM((rep * ROW_BLOCK, HEAD_DIM), F32)] * n_kv,
        compiler_params=_params("parallel"),
        name="stick_prompt",
    )(q, k, v)


def _page_head(ref, g, n_heads):
    keys = ref.shape[1] // n_heads
    return ref[0, pl.ds(g, keys, stride=n_heads), :].astype(BF16)


def _sb_sample_kernel(pt_ref, q_ref, *refs, n_kv, n_new, dec_seq, n_pages_step, has_init):
    del pt_ref
    refs = list(refs)
    new_refs = [refs.pop(0), refs.pop(0)] if n_new else None
    k_pages = [refs.pop(0) for _ in range(n_pages_step)]
    v_pages = [refs.pop(0) for _ in range(n_pages_step)]
    init_refs = [refs.pop(0), refs.pop(0)] if has_init else None
    o_ref, carry_ref = refs
    suffix = _sb_suffix_matrix()
    rows = q_ref.shape[2]

    @pl.when(pl.program_id(1) == 0)
    def _():
        if has_init:
            o_ref[...] = init_refs[0][...]
            carry_ref[...] = init_refs[1][...]
        else:
            o_ref[...] = jnp.zeros_like(o_ref)
            carry_ref[...] = jnp.zeros_like(carry_ref)

    def visit(k_of, v_of, strict):
        @pl.when(jnp.max(carry_ref[...]) >= SB_DONE2)
        def _():
            heads = range(n_kv)
            carries, accs = _sb_blocks([q_ref[0, g] for g in heads], [k_of(g) for g in heads],
                                       [v_of(g) for g in heads], strict, [carry_ref[0, g] for g in heads],
                                       [o_ref[0, g] for g in heads], suffix)
            for g in heads:
                carry_ref[0, g] = carries[g]
                o_ref[0, g] = accs[g]

    cols = lambda ref, g: ref[0, :, g * HEAD_DIM:(g + 1) * HEAD_DIM].astype(BF16)
    if n_new:
        q_idx = lax.broadcasted_iota(I32, (rows, LANES), 0) % dec_seq
        col = lax.broadcasted_iota(I32, (rows, LANES), 1)
        visit(lambda g: cols(new_refs[0], g), lambda g: cols(new_refs[1], g), col < q_idx)
    for kp, vp in zip(k_pages, v_pages):
        visit(lambda g, kp=kp: _page_head(kp, g, n_kv), lambda g, vp=vp: _page_head(vp, g, n_kv), None)


def _sb_sample_call(page_table, q, new_kv, cache_k, cache_v, init, dec_seq, first_page, n_steps, pages_per_step):
    db, n_kv, rows, _ = q.shape
    page_rows = cache_k.shape[1]
    assert page_rows == LANES * n_kv
    seq4 = lambda b, c, pt: (b, 0, 0, 0)
    in_specs = [pl.BlockSpec((1, n_kv, rows, HEAD_DIM), seq4)]
    args = [q]
    if new_kv is not None:
        in_specs += [pl.BlockSpec((1,) + new_kv[0].shape[1:], lambda b, c, pt: (b, 0, 0))] * 2
        args += list(new_kv)
    for cache in (cache_k, cache_v):
        for i in range(pages_per_step):
            in_specs.append(pl.BlockSpec(
                (1, page_rows, HEAD_DIM), lambda b, c, pt, i=i: (pt[b, first_page - c * pages_per_step - i], 0, 0)))
            args.append(cache)
    if init is not None:
        in_specs += [pl.BlockSpec((1, n_kv, rows, HEAD_DIM), seq4)] * 2
        args += list(init)
    kern = functools.partial(_sb_sample_kernel, n_kv=n_kv, n_new=new_kv is not None, dec_seq=dec_seq,
                             n_pages_step=pages_per_step, has_init=init is not None)
    out_sds = jax.ShapeDtypeStruct((db, n_kv, rows, HEAD_DIM), F32)
    return pl.pallas_call(
        kern,
        out_shape=(out_sds, out_sds),
        grid_spec=pltpu.PrefetchScalarGridSpec(
            num_scalar_prefetch=1,
            grid=(db, n_steps),
            in_specs=in_specs,
            out_specs=(pl.BlockSpec((1, n_kv, rows, HEAD_DIM), seq4), pl.BlockSpec((1, n_kv, rows, HEAD_DIM), seq4)),
        ),
        compiler_params=_params("parallel", "arbitrary"),
        name="stick_sample",
    )(page_table, *args)


def _sortable(score):
    bits = lax.bitcast_convert_type(score + 0.0, I32)
    key = bits ^ (lax.shift_right_arithmetic(bits, 31) & 0x7FFFFFFF)
    return jnp.maximum(key, INT_MIN + 1)


COUNT_ROWS = 64


def _count(key_ref, n_units, unit, pred):
    rows = COUNT_ROWS if unit % COUNT_ROWS == 0 else SUBLANES

    def body(u, acc):
        r0 = pl.multiple_of(u * unit, unit)
        hit = pred(key_ref[pl.ds(r0, unit), :], r0)
        ones = jnp.where(hit, jnp.ones(hit.shape, I32), jnp.zeros(hit.shape, I32))
        return acc + jnp.sum(ones.reshape(unit // rows, rows, LANES), axis=0)
    acc = lax.fori_loop(0, n_units, body, jnp.zeros((rows, LANES), I32))
    return jnp.sum(acc, axis=0, keepdims=True)


def _select_topk(key_ref, n_units, unit, k, n_rows):
    def bit_step(i, state):
        tau, c_ge = state
        cand = tau ^ lax.shift_left(jnp.int32(1), 31 - i)
        cnt = _count(key_ref, n_units, unit, lambda keys, r0: keys >= cand)
        return jnp.where(cnt >= k, cand, tau), jnp.where(cnt >= k, cnt, c_ge)
    start = (jnp.full((1, LANES), INT_MIN, I32), jnp.full((1, LANES), n_units * unit, I32))
    tau, c_ge = lax.fori_loop(0, 32, bit_step, start)
    real = tau > INT_MIN
    tie = real & (c_ge > k)
    row = lax.broadcasted_iota(I32, (unit, LANES), 0)
    n_bits = max(1, (n_rows - 1).bit_length())

    def break_ties():
        need = k - _count(key_ref, n_units, unit, lambda keys, r0: keys > tau)

        def pos_step(i, x):
            cand = x | lax.shift_left(jnp.int32(1), n_bits - 1 - i)
            cnt = _count(key_ref, n_units, unit, lambda keys, r0: (keys == tau) & ((r0 + row) < cand))
            return jnp.where(cnt < need, cand, x)
        return lax.fori_loop(0, n_bits, pos_step, jnp.zeros((1, LANES), I32))

    last_tie = lax.cond(jnp.max(tie.astype(I32)) > 0, break_ties, lambda: jnp.zeros((1, LANES), I32))
    last = jnp.where(tie, last_tie, jnp.where(real, jnp.int32(2 ** 30), jnp.int32(-1)))
    return tau, last


def _col_reduce(x, fn):
    rows = x.shape[0]
    if rows > COUNT_ROWS and rows % COUNT_ROWS == 0:
        x = fn(x.reshape(rows // COUNT_ROWS, COUNT_ROWS, x.shape[1]), axis=0)
    return fn(x, axis=0, keepdims=True)


def _selected(keys, r0, tau, last):
    row = lax.broadcasted_iota(I32, keys.shape, 0)
    return (keys > tau) | ((keys == tau) & ((r0 + row) <= last))


IDX_SCALE = (IDX_HEADS * IDX_DIM) ** -0.5
MASKED = -1e30
EXP2_SCALE = HEAD_DIM ** -0.5 * math.log2(math.e)


def _dsa_prompt_kernel(aq_ref, iq_ref, misc_ref, ik_ref, k_ref, v_ref, o_ref, key_ref, *scratch,
                       unit, topk, n_valid, n_heads, n_kv, iw_row):
    b = pl.program_id(0)
    n_units = ((b + 1) * ROW_BLOCK + unit - 1) // unit
    rep = n_heads // n_kv
    row = lax.broadcasted_iota(I32, (unit, LANES), 0)
    q_pos = b * ROW_BLOCK + lax.broadcasted_iota(I32, (unit, LANES), 1)
    q_real = q_pos < n_valid

    iq_t = iq_ref[0]
    rhs = jnp.concatenate([iq_t[h * IDX_DIM:(h + 1) * IDX_DIM, :] for h in range(IDX_HEADS)], axis=1)
    w_t = misc_ref[0, iw_row:iw_row + IDX_HEADS, :]

    def score_unit(u, carry):
        r0 = pl.multiple_of(u * unit, unit)
        dots = _dot(ik_ref[pl.ds(r0, unit), :], rhs)
        score = jnp.zeros((unit, LANES), F32)
        for h in range(IDX_HEADS):
            score = score + jnp.maximum(dots[:, h * LANES:(h + 1) * LANES], 0.0) * w_t[h:h + 1, :]
        k_pos = r0 + row
        adm = (k_pos <= q_pos) & (q_real | (k_pos == 0))
        key_ref[pl.ds(r0, unit), :] = jnp.where(adm, _sortable(score * IDX_SCALE), INT_MIN)
        return carry

    lax.fori_loop(0, n_units, score_unit, 0)
    tau, last = _select_topk(key_ref, n_units, unit, topk, key_ref.shape[0])

    m_refs, l_refs, acc_refs, lg_refs = (scratch[i * n_kv:(i + 1) * n_kv] for i in range(4))
    for g in range(n_kv):
        m_refs[g][...] = jnp.full_like(m_refs[g], MASKED)
        l_refs[g][...] = jnp.zeros_like(l_refs[g])
        acc_refs[g][...] = jnp.zeros_like(acc_refs[g])
    aq_t = aq_ref[0]

    qgs = [jnp.concatenate([aq_t[(g * rep + r) * HEAD_DIM:(g * rep + r + 1) * HEAD_DIM, :] for r in range(rep)],
                           axis=1) for g in range(n_kv)]
    subs = [(s, ROW_BLOCK) for s in range(0, unit, ROW_BLOCK)]
    fold = lambda x, fn: fn(x.reshape(x.shape[0] // SUBLANES, SUBLANES, x.shape[1]), axis=0)

    def logits_of_unit(u, slot):
        r0 = pl.multiple_of(u * unit, unit)
        sel = _selected(key_ref[pl.ds(r0, unit), :], r0, tau, last)
        tops = [None] * n_kv
        for start, size in subs:
            sub = slice(start, start + size)
            for g in range(n_kv):
                lg = _dot(k_ref[pl.ds(r0 + start, size), g * HEAD_DIM:(g + 1) * HEAD_DIM], qgs[g])
                parts = []
                for r in range(rep):
                    x = jnp.where(sel[sub], lg[:, r * LANES:(r + 1) * LANES], MASKED)
                    lg_refs[g][slot, sub, r * LANES:(r + 1) * LANES] = x
                    parts.append(fold(x, jnp.max))
                part = jnp.concatenate(parts, axis=1)
                tops[g] = part if tops[g] is None else jnp.maximum(tops[g], part)
        return tuple(tops)

    def attend_unit(u, tops):
        r0 = pl.multiple_of(u * unit, unit)
        slot = u % 2
        next_tops = logits_of_unit(jnp.minimum(u + 1, n_units - 1), 1 - slot)
        m_old = [m_refs[g][...] for g in range(n_kv)]
        m_new = [jnp.maximum(m_old[g], jnp.max(tops[g], axis=0, keepdims=True)) for g in range(n_kv)]
        sums, pvs = [None] * n_kv, [None] * n_kv
        for start, size in subs:
            sub = slice(start, start + size)
            for g in range(n_kv):
                p = jnp.exp2(lg_refs[g][slot, sub, :] - m_new[g])
                pv = _dot_tn(v_ref[pl.ds(r0 + start, size), g * HEAD_DIM:(g + 1) * HEAD_DIM], p.astype(BF16))
                part = fold(p, jnp.sum)
                sums[g] = part if sums[g] is None else sums[g] + part
                pvs[g] = pv if pvs[g] is None else pvs[g] + pv
        for g in range(n_kv):
            alpha = jnp.exp2(m_old[g] - m_new[g])
            m_refs[g][...] = m_new[g]
            l_refs[g][...] = alpha * l_refs[g][...] + jnp.sum(sums[g], axis=0, keepdims=True)
            acc_refs[g][...] = acc_refs[g][...] * alpha + pvs[g]
        return next_tops

    lax.fori_loop(0, n_units, attend_unit, logits_of_unit(0, 0))
    for g in range(n_kv):
        for r in range(rep):
            sl = slice(r * LANES, (r + 1) * LANES)
            out = acc_refs[g][:, sl] * (1.0 / l_refs[g][:, sl])
            h = g * rep + r
            o_ref[:, h * HEAD_DIM:(h + 1) * HEAD_DIM] = jnp.transpose(out).astype(o_ref.dtype)


def _dsa_prompt(aq_t, iq_t, misc_t, iw_row, ik, k, v, n_rows, n_valid, topk):
    n_blk = n_rows // ROW_BLOCK
    n_heads = aq_t.shape[1] // HEAD_DIM
    n_kv = k.shape[1] // HEAD_DIM
    rep = n_heads // n_kv
    unit = _row_tile(n_rows, 640)
    blk = lambda i: (i, 0, 0)
    const = lambda i: (0, 0)
    kern = functools.partial(_dsa_prompt_kernel, unit=unit, topk=topk, n_valid=n_valid, n_heads=n_heads, n_kv=n_kv,
                             iw_row=iw_row)
    return pl.pallas_call(
        kern,
        out_shape=jax.ShapeDtypeStruct((aq_t.shape[0] * ROW_BLOCK, n_heads * HEAD_DIM), BF16),
        grid=(n_blk,),
        in_specs=[pl.BlockSpec((1,) + aq_t.shape[1:], blk), pl.BlockSpec((1,) + iq_t.shape[1:], blk),
                  pl.BlockSpec((1,) + misc_t.shape[1:], blk),
                  pl.BlockSpec((n_rows, ik.shape[1]), const), pl.BlockSpec((n_rows, k.shape[1]), const),
                  pl.BlockSpec((n_rows, v.shape[1]), const)],
        out_specs=pl.BlockSpec((ROW_BLOCK, n_heads * HEAD_DIM), lambda i: (i, 0)),
        scratch_shapes=[pltpu.VMEM((n_rows, LANES), I32)]
                       + [pltpu.VMEM((1, rep * LANES), F32)] * (2 * n_kv)
                       + [pltpu.VMEM((HEAD_DIM, rep * LANES), F32)] * n_kv
                       + [pltpu.VMEM((2, unit, rep * LANES), F32)] * n_kv,
        compiler_params=_params("parallel"),
        name="dsa_prompt",
    )(aq_t, iq_t, misc_t, ik, k, v)


Q_PAD = SUBLANES


def _idx_sample_kernel(pt_ref, iq_ref, w_ref, new_ref, *refs, n_pages):
    del pt_ref
    pages, o_ref = refs[:n_pages], refs[n_pages]
    iq = iq_ref[0]
    w = w_ref[0]

    def score(keys_t):
        s = jnp.maximum(_dot(iq, keys_t), 0.0) * w
        return jnp.sum(s.reshape(IDX_HEADS, Q_PAD, LANES), axis=0) * IDX_SCALE

    for p in range(n_pages):
        o_ref[0, :, p * LANES:(p + 1) * LANES] = score(pages[p][0].astype(BF16))
    o_ref[0, :, n_pages * LANES:] = score(new_ref[0])


def _idx_sample(page_table, iq, w, new_ik_t, cache_idx_t):
    db = iq.shape[0]
    n_pages = page_table.shape[1]
    page = cache_idx_t.shape[2]
    assert page == LANES
    seq = lambda b, pt: (b, 0, 0)
    in_specs = [pl.BlockSpec((1,) + iq.shape[1:], seq), pl.BlockSpec((1,) + w.shape[1:], seq),
                pl.BlockSpec((1,) + new_ik_t.shape[1:], seq)]
    in_specs += [pl.BlockSpec((1, IDX_DIM, page), lambda b, pt, p=p: (pt[b, p], 0, 0)) for p in range(n_pages)]
    n_keys = (n_pages + 1) * LANES
    return pl.pallas_call(
        functools.partial(_idx_sample_kernel, n_pages=n_pages),
        out_shape=jax.ShapeDtypeStruct((db, Q_PAD, n_keys), F32),
        grid_spec=pltpu.PrefetchScalarGridSpec(
            num_scalar_prefetch=1, grid=(db,), in_specs=in_specs,
            out_specs=pl.BlockSpec((1, Q_PAD, n_keys), seq)),
        compiler_params=_params("parallel"),
        name="idx_sample",
    )(page_table, iq, w, new_ik_t, *([cache_idx_t] * n_pages))


def _topk_sample_kernel(s_ref, o_ref, key_ref, *, unit, topk, past_len, dec_seq):
    n_keys = s_ref.shape[0]
    n_units = n_keys // unit
    row = lax.broadcasted_iota(I32, (unit, LANES), 0)
    q = lax.broadcasted_iota(I32, (unit, LANES), 1) % Q_PAD
    for u in range(n_units):
        k_pos = u * unit + row
        adm = ((q < dec_seq) & (k_pos <= past_len + q)) | ((q >= dec_seq) & (k_pos == 0))
        key_ref[u * unit:(u + 1) * unit, :] = jnp.where(adm, _sortable(s_ref[u * unit:(u + 1) * unit, :]), INT_MIN)
    tau, last = _select_topk(key_ref, n_units, unit, topk, n_keys)
    for u in range(n_units):
        sel = _selected(key_ref[u * unit:(u + 1) * unit, :], u * unit, tau, last)
        o_ref[u * unit:(u + 1) * unit, :] = jnp.where(sel, 0.0, MASKED)


def _topk_sample(scores_t, topk, past_len, dec_seq):
    n_keys, n_q = scores_t.shape
    unit = _row_tile(n_keys, 640)
    kern = functools.partial(_topk_sample_kernel, unit=unit, topk=topk, past_len=past_len, dec_seq=dec_seq)
    return pl.pallas_call(
        kern,
        out_shape=jax.ShapeDtypeStruct((n_keys, n_q), F32),
        grid=(n_q // LANES,),
        in_specs=[pl.BlockSpec((n_keys, LANES), lambda i: (0, i))],
        out_specs=pl.BlockSpec((n_keys, LANES), lambda i: (0, i)),
        scratch_shapes=[pltpu.VMEM((n_keys, LANES), I32)],
        compiler_params=_params("parallel"),
        name="topk_sample",
    )(scores_t)


def _attn_sample_kernel(pt_ref, q_ref, bias_ref, bias_new_ref, k_new_ref, v_new_ref, *refs, n_kv, pages_per_step):
    del pt_ref
    k_pages = refs[:pages_per_step]
    v_pages = refs[pages_per_step:2 * pages_per_step]
    o_ref, m_ref, l_ref, acc_ref = refs[2 * pages_per_step:]
    c = pl.program_id(1)
    rep = q_ref.shape[2] // Q_PAD

    @pl.when(c == 0)
    def _():
        m_ref[...] = jnp.full_like(m_ref, MASKED)
        l_ref[...] = jnp.zeros_like(l_ref)
        acc_ref[...] = jnp.zeros_like(acc_ref)

    def update(g, kg, vg, bias):
        sel = jnp.concatenate([bias] * rep, axis=0) > 0.5 * MASKED
        lg = jnp.where(sel, _dot_nt(q_ref[0, g], kg), MASKED)
        m_old = m_ref[g]
        m_new = jnp.maximum(m_old, jnp.max(lg, axis=1, keepdims=True))
        alpha = jnp.exp2(m_old - m_new)
        p = jnp.where(sel, jnp.exp2(lg - m_new), 0.0)
        l_ref[g] = alpha * l_ref[g] + jnp.sum(p, axis=1, keepdims=True)
        acc_ref[g] = alpha * acc_ref[g] + _dot(p.astype(BF16), vg)
        m_ref[g] = m_new

    def head(ref, g):
        return ref[0, :, g * HEAD_DIM:(g + 1) * HEAD_DIM].astype(BF16)

    @pl.when(c == 0)
    def _():
        for g in range(n_kv):
            update(g, head(k_new_ref, g), head(v_new_ref, g), bias_new_ref[0])

    for g in range(n_kv):
        kg = jnp.concatenate([_page_head(r, g, n_kv) for r in k_pages], axis=0)
        vg = jnp.concatenate([_page_head(r, g, n_kv) for r in v_pages], axis=0)
        update(g, kg, vg, bias_ref[0])

    @pl.when(c == pl.num_programs(1) - 1)
    def _():
        for g in range(n_kv):
            o_ref[0, g] = acc_ref[g] * (1.0 / l_ref[g])


ATTN_PAGES_PER_STEP = 64


def _attn_sample(page_table, q, bias, new_k, new_v, cache_k, cache_v):
    db, n_kv, rows, _ = q.shape
    n_pages = page_table.shape[1]
    page_rows = cache_k.shape[1]
    page = page_rows // n_kv
    per_step = max(d for d in range(1, ATTN_PAGES_PER_STEP + 1) if n_pages % d == 0)
    seq4 = lambda b, c, pt: (b, 0, 0, 0)
    seq3 = lambda b, c, pt: (b, 0, 0)
    in_specs = [pl.BlockSpec((1, n_kv, rows, HEAD_DIM), seq4),
                pl.BlockSpec((1, Q_PAD, per_step * page), lambda b, c, pt: (b, 0, c)),
                pl.BlockSpec((1, Q_PAD, page), lambda b, c, pt: (b, 0, n_pages)),
                pl.BlockSpec((1,) + new_k.shape[1:], seq3), pl.BlockSpec((1,) + new_v.shape[1:], seq3)]
    for _ in range(2):
        in_specs += [pl.BlockSpec((1, page_rows, HEAD_DIM), lambda b, c, pt, i=i: (pt[b, c * per_step + i], 0, 0))
                     for i in range(per_step)]
    return pl.pallas_call(
        functools.partial(_attn_sample_kernel, n_kv=n_kv, pages_per_step=per_step),
        out_shape=jax.ShapeDtypeStruct((db, n_kv, rows, HEAD_DIM), F32),
        grid_spec=pltpu.PrefetchScalarGridSpec(
            num_scalar_prefetch=1, grid=(db, n_pages // per_step), in_specs=in_specs,
            out_specs=pl.BlockSpec((1, n_kv, rows, HEAD_DIM), seq4),
            scratch_shapes=[pltpu.VMEM((n_kv, rows, 1), F32), pltpu.VMEM((n_kv, rows, 1), F32),
                            pltpu.VMEM((n_kv, rows, HEAD_DIM), F32)]),
        compiler_params=_params("parallel", "arbitrary"),
        name="attn_sample",
    )(page_table, q, bias, bias, new_k, new_v, *([cache_k] * per_step), *([cache_v] * per_step))


SB_EAGER_PAGES = 4
SB_PAGES_PER_STEP = 12


def _sb_sample(page_table, q, new_k, new_v, cache_k, cache_v, dec_seq):
    n_pages = page_table.shape[1]
    eager = min(SB_EAGER_PAGES, n_pages)
    o, carry = _sb_sample_call(page_table, q, (new_k, new_v), cache_k, cache_v, None, dec_seq,
                               n_pages - 1, 1, eager)
    rest = n_pages - eager
    if rest == 0:
        return o
    per_step = max(d for d in range(1, SB_PAGES_PER_STEP + 1) if rest % d == 0)
    finish = lambda: _sb_sample_call(page_table, q, None, cache_k, cache_v, (o, carry), dec_seq,
                                     rest - 1, rest // per_step, per_step)[0]
    return lax.cond(jnp.max(carry) >= SB_DONE2, finish, lambda: o)


def _pad_rows(a, rows):
    return jnp.pad(a, ((0, 0), (0, rows - a.shape[1])) + ((0, 0),) * (a.ndim - 2))


def kernel(x_prompt, x_sample, cache_a_k, cache_a_v, cache_a_idx_k, state_b, cache_c_k, cache_c_v, page_table,
           meta_tokens, norm_mix, norm_ffn, norm_final, w_in0, w_b_gate2, b_b_gate, b_head_norm, w_out0,
           w_in1, w_out1, w_ff1, w_ff2):
    batch, seq, d = x_prompt.shape
    assert batch == 1
    db, ds, _ = x_sample.shape
    assert ds <= Q_PAD and ds <= GLA_SUB
    t_p = N_META + seq
    t_pad = _round_up(t_p, ROW_BLOCK)
    n_s = db * ds
    m_s = _round_up(n_s, ROW_BLOCK)
    n_blk = t_pad // ROW_BLOCK
    n_pool, page = cache_a_k.shape[:2]
    n_pages = page_table.shape[1]
    past_len = n_pages * page
    topk_p = min(TOPK_MAX, seq // 4)
    topk_s = min(TOPK_MAX, (past_len + ds) // 4)
    srows = slice(t_pad, t_pad + n_s)

    h = jnp.concatenate([meta_tokens.astype(F32), x_prompt[0], jnp.zeros((t_pad - t_p, d), F32),
                         x_sample.reshape(n_s, d), jnp.zeros((m_s - n_s, d), F32)], axis=0)

    lay, groups = _even_layout(d)
    gate_w = _gate_weights(w_b_gate2, b_b_gate, lay)

    xn = _rmsnorm(h, norm_mix[0], BF16)
    proj = [_matmul(xn, jnp.pad(w_in0[:, lo:hi].astype(BF16), ((0, 0), (0, width - (hi - lo)))), F32)
            for lo, hi, width in groups]
    seg = lambda name: proj[lay[name][0]][:, lay[name][1]:lay[name][1] + lay[name][2]]
    aq_t, iq_t, misc_t, ak, av, ik, k_bf, v_bf, ik_bf = _rope(proj[lay["aq"][0]], lay, t_pad, past_len, ds)
    a_kv = k_bf.shape[1]
    n_heads_a = aq_t.shape[1] // HEAD_DIM
    rep_a = n_heads_a // A_KV_HEADS
    iw_row = lay["iw"][1] - lay["misc"][1]

    oa = _dsa_prompt(aq_t, iq_t, misc_t, iw_row, ik_bf, k_bf, v_bf, t_pad, t_p, topk_p)
    ob, state_p = _gla_prompt(proj, lay, t_pad, t_p, gate_w, b_head_norm)

    sample_rows = lambda a_t: a_t[n_blk:].transpose(0, 2, 1).reshape(m_s, a_t.shape[1])[:n_s]
    aq, iq = sample_rows(aq_t), sample_rows(iq_t)
    iw = sample_rows(misc_t[:, iw_row:iw_row + IDX_HEADS])
    iq_s = iq.reshape(db, ds, IDX_HEADS, IDX_DIM).transpose(0, 2, 1, 3)
    iq_s = jnp.pad(iq_s, ((0, 0), (0, 0), (0, Q_PAD - ds), (0, 0))).reshape(db, IDX_HEADS * Q_PAD, IDX_DIM)
    iw_s = iw.reshape(db, ds, IDX_HEADS).transpose(0, 2, 1)
    iw_s = jnp.pad(iw_s, ((0, 0), (0, 0), (0, Q_PAD - ds))).reshape(db, IDX_HEADS * Q_PAD, 1)
    iw_s = jnp.broadcast_to(iw_s, (db, IDX_HEADS * Q_PAD, LANES))
    new_ik_t = _pad_rows(ik_bf[srows].reshape(db, ds, IDX_DIM), page).transpose(0, 2, 1)
    scores = _idx_sample(page_table, iq_s, iw_s, new_ik_t, cache_a_idx_k.transpose(0, 2, 1))
    n_keys = scores.shape[2]
    n_q = _round_up(db * Q_PAD, LANES)
    scores_t = jnp.pad(scores.reshape(db * Q_PAD, n_keys).T, ((0, 0), (0, n_q - db * Q_PAD)))
    bias = _topk_sample(scores_t, topk_s, past_len, ds)[:, :db * Q_PAD].T.reshape(db, Q_PAD, n_keys)
    aq_s = aq.reshape(db, ds, A_KV_HEADS, rep_a, HEAD_DIM).transpose(0, 2, 3, 1, 4)
    aq_s = jnp.pad(aq_s, ((0, 0), (0, 0), (0, 0), (0, Q_PAD - ds), (0, 0))).reshape(db, A_KV_HEADS, rep_a * Q_PAD, HEAD_DIM)
    new_k = _pad_rows(k_bf[srows].reshape(db, ds, a_kv), page)
    new_v = _pad_rows(v_bf[srows].reshape(db, ds, a_kv), page)
    oa_s = _attn_sample(page_table, aq_s, bias, new_k, new_v,
                        cache_a_k.reshape(n_pool, page * A_KV_HEADS, HEAD_DIM),
                        cache_a_v.reshape(n_pool, page * A_KV_HEADS, HEAD_DIM))
    oa_s = oa_s.reshape(db, A_KV_HEADS, rep_a, Q_PAD, HEAD_DIM)[:, :, :, :ds].transpose(0, 3, 1, 2, 4)
    oa_s = oa_s.reshape(n_s, n_heads_a * HEAD_DIM).astype(BF16)

    chunk_s = lambda name: _pad_rows(seg(name)[srows].reshape(db, ds, lay[name][2]), GLA_SUB)
    ob_s, state_s = _gla_sample(chunk_s("bq"), chunk_s("bk"), chunk_s("bv"), chunk_s("br"), chunk_s("glr"),
                                state_b, ds, gate_w, b_head_norm)
    ob_s = ob_s[:, :ds].reshape(n_s, -1)

    oa = oa.at[srows].set(oa_s)
    ob = ob.at[srows].set(ob_s)
    h, xn = _proj_res_norm([oa, ob], w_out0.astype(BF16), h, norm_ffn[0])
    w_ff1_bf, w_ff2_bf = w_ff1.astype(BF16), w_ff2.astype(BF16)
    h, xn = _mlp(xn, h, w_ff1_bf, w_ff2_bf, 0, norm_mix[1], BF16)

    c_kv = C_KV_HEADS * HEAD_DIM
    c_q = w_in1.shape[1] - 2 * c_kv
    rep_c = c_q // c_kv
    q1 = _matmul(xn, w_in1[:, :c_q].astype(BF16), BF16, scale=SB_Q_SCALE)
    ck, cv, ck_bf, cv_bf = _kv_proj(xn, w_in1[:, c_q:].astype(BF16), C_KV_HEADS)
    oc = _sb_prompt(q1, ck_bf, cv_bf, t_pad)
    q1_s = q1[srows].reshape(db, ds, C_KV_HEADS, rep_c, HEAD_DIM).transpose(0, 2, 3, 1, 4)
    q1_s = q1_s.reshape(db, C_KV_HEADS, rep_c * ds, HEAD_DIM)
    new_ck = _pad_rows(ck_bf[srows].reshape(db, ds, c_kv), page)
    new_cv = _pad_rows(cv_bf[srows].reshape(db, ds, c_kv), page)
    oc_s = _sb_sample(page_table, q1_s, new_ck, new_cv, cache_c_k.reshape(n_pool, page * C_KV_HEADS, HEAD_DIM),
                      cache_c_v.reshape(n_pool, page * C_KV_HEADS, HEAD_DIM), ds)
    oc_s = oc_s.reshape(db, C_KV_HEADS, rep_c, ds, HEAD_DIM).transpose(0, 3, 1, 2, 4).reshape(n_s, c_q).astype(BF16)
    h, xn = _proj_res_norm([oc.at[srows].set(oc_s)], w_out1.astype(BF16), h, norm_ffn[1])
    h, y = _mlp(xn, h, w_ff1_bf, w_ff2_bf, 1, norm_final, F32)

    y_prompt = y[N_META:t_p][None]
    y_sample = y[srows].reshape(db, ds, d)
    kv_p = lambda a, heads: a[:t_p * heads].reshape(1, t_p, heads, HEAD_DIM)
    kv_s = lambda a, heads: a[t_pad * heads:(t_pad + n_s) * heads].reshape(db, ds, heads, HEAD_DIM)
    return (y_prompt, y_sample,
            kv_p(ak, A_KV_HEADS), kv_p(av, A_KV_HEADS), ik[:t_p, :IDX_DIM][None], state_p[None],
            kv_p(ck, C_KV_HEADS), kv_p(cv, C_KV_HEADS),
            kv_s(ak, A_KV_HEADS), kv_s(av, A_KV_HEADS), ik[srows, :IDX_DIM].reshape(db, ds, IDX_DIM),
            state_s, kv_s(ck, C_KV_HEADS), kv_s(cv, C_KV_HEADS))
```

```python
import functools
import math

import jax
import jax.numpy as jnp
from jax import lax
from jax.experimental import pallas as pl
from jax.experimental.pallas import tpu as pltpu

F32 = jnp.float32
BF16 = jnp.bfloat16
I32 = jnp.int32

N_META = 16
HEAD_DIM = 128
ROPE_THETA = 10000.0
EPS = 1e-6
A_KV_HEADS = 2
IDX_HEADS = 16
IDX_DIM = 64
TOPK_MAX = 256
B_HEADS = 4
B_GATE_RANK = 16
B_GATE_TAU = 16.0
C_KV_HEADS = 4

LANES = 128
SUBLANES = 8
VMEM_LIMIT_BYTES = 56 * 1024 * 1024

ROW_BLOCK = 128
INT_MIN = -(2 ** 31)


def _round_up(x, m):
    return (x + m - 1) // m * m


def _row_tile(m, cap):
    best = ROW_BLOCK
    for k in range(1, m // ROW_BLOCK + 1):
        t = k * ROW_BLOCK
        if t <= cap and m % t == 0:
            best = t
    return best


def _col_tile(n, cap):
    best = LANES
    for k in range(1, n // LANES + 1):
        t = k * LANES
        if t <= cap and n % t == 0:
            best = t
    return best


def _params(*sem):
    return pltpu.CompilerParams(dimension_semantics=sem, vmem_limit_bytes=VMEM_LIMIT_BYTES)


def _rms(x, g):
    ms = jnp.mean(x * x, axis=-1, keepdims=True)
    return x * lax.rsqrt(ms + EPS) * g


def _rmsnorm_kernel(x_ref, g_ref, o_ref):
    o_ref[...] = _rms(x_ref[...], g_ref[...]).astype(o_ref.dtype)


def _rmsnorm(x, g, out_dtype):
    m, d = x.shape
    tm = _row_tile(m, 512)
    return pl.pallas_call(
        _rmsnorm_kernel,
        out_shape=jax.ShapeDtypeStruct((m, d), out_dtype),
        grid=(m // tm,),
        in_specs=[pl.BlockSpec((tm, d), lambda i: (i, 0)), pl.BlockSpec((1, d), lambda i: (0, 0))],
        out_specs=pl.BlockSpec((tm, d), lambda i: (i, 0)),
        compiler_params=_params("parallel"),
        name="rmsnorm",
    )(x, g.reshape(1, d))


MATMUL_COL_TILE = 1152


def _matmul_kernel(x_ref, w_ref, o_ref, *, scale):
    acc = jnp.dot(x_ref[...], w_ref[...], preferred_element_type=F32)
    o_ref[...] = (acc if scale is None else acc * scale).astype(o_ref.dtype)


def _matmul(x, w, out_dtype, scale=None):
    m, k = x.shape
    n = w.shape[1]
    tm = _row_tile(m, 1408)
    tn = _col_tile(n, MATMUL_COL_TILE)
    return pl.pallas_call(
        functools.partial(_matmul_kernel, scale=scale),
        out_shape=jax.ShapeDtypeStruct((m, n), out_dtype),
        grid=(m // tm, n // tn),
        in_specs=[pl.BlockSpec((tm, k), lambda i, j: (i, 0)), pl.BlockSpec((k, tn), lambda i, j: (0, j))],
        out_specs=pl.BlockSpec((tm, tn), lambda i, j: (i, j)),
        compiler_params=_params("parallel", "parallel"),
        name="matmul",
    )(x, w)


def _kv_proj_kernel(x_ref, w_ref, k_ref, v_ref, kbf_ref, vbf_ref, *, n_heads):
    acc = jnp.dot(x_ref[...], w_ref[...], preferred_element_type=F32)
    tm = x_ref.shape[0]
    ckv = n_heads * HEAD_DIM
    for h in range(n_heads):
        k_ref[pl.ds(h, tm, stride=n_heads), :] = acc[:, h * HEAD_DIM:(h + 1) * HEAD_DIM]
        v_ref[pl.ds(h, tm, stride=n_heads), :] = acc[:, ckv + h * HEAD_DIM:ckv + (h + 1) * HEAD_DIM]
    kbf_ref[...] = acc[:, :ckv].astype(BF16)
    vbf_ref[...] = acc[:, ckv:].astype(BF16)


def _kv_proj(x, w, n_heads):
    m, k = x.shape
    ckv = n_heads * HEAD_DIM
    assert w.shape == (k, 2 * ckv)
    tm = _row_tile(m, 768)
    row = lambda i: (i, 0)
    return pl.pallas_call(
        functools.partial(_kv_proj_kernel, n_heads=n_heads),
        out_shape=(jax.ShapeDtypeStruct((m * n_heads, HEAD_DIM), F32), jax.ShapeDtypeStruct((m * n_heads, HEAD_DIM), F32),
                   jax.ShapeDtypeStruct((m, ckv), BF16), jax.ShapeDtypeStruct((m, ckv), BF16)),
        grid=(m // tm,),
        in_specs=[pl.BlockSpec((tm, k), row), pl.BlockSpec((k, 2 * ckv), lambda i: (0, 0))],
        out_specs=(pl.BlockSpec((tm * n_heads, HEAD_DIM), row), pl.BlockSpec((tm * n_heads, HEAD_DIM), row),
                   pl.BlockSpec((tm, ckv), row), pl.BlockSpec((tm, ckv), row)),
        compiler_params=_params("parallel"),
        name="kv_proj",
    )(x, w)


def _proj_res_norm_kernel(*refs, n_in):
    x_refs, w_refs = refs[:n_in], refs[n_in:2 * n_in]
    h_ref, g_ref, hout_ref, xn_ref = refs[2 * n_in:]
    hn = h_ref[...]
    for x_ref, w_ref in zip(x_refs, w_refs):
        hn = hn + jnp.dot(x_ref[...], w_ref[...], preferred_element_type=F32)
    hout_ref[...] = hn
    xn_ref[...] = _rms(hn, g_ref[...]).astype(xn_ref.dtype)


def _proj_res_norm(xs, w, h, g):
    m, k = xs[0].shape
    assert all(x.shape == (m, k) for x in xs) and w.shape[0] == k * len(xs)
    d = w.shape[1]
    tm = _row_tile(m, 384)
    row = lambda i: (i, 0)
    in_specs = [pl.BlockSpec((tm, k), row) for _ in xs]
    in_specs += [pl.BlockSpec((k, d), lambda i, j=j: (j, 0)) for j in range(len(xs))]
    in_specs += [pl.BlockSpec((tm, d), row), pl.BlockSpec((1, d), lambda i: (0, 0))]
    return pl.pallas_call(
        functools.partial(_proj_res_norm_kernel, n_in=len(xs)),
        out_shape=(jax.ShapeDtypeStruct((m, d), F32), jax.ShapeDtypeStruct((m, d), BF16)),
        grid=(m // tm,),
        in_specs=in_specs,
        out_specs=(pl.BlockSpec((tm, d), row), pl.BlockSpec((tm, d), row)),
        compiler_params=_params("parallel"),
        name="proj_res_norm",
    )(*xs, *([w] * len(xs)), h, g.reshape(1, d))


def _mlp_kernel(xn_ref, h_ref, w1_ref, w2_ref, g_ref, hout_ref, xnout_ref):
    f = pl.program_id(1)

    @pl.when(f == 0)
    def _():
        hout_ref[...] = h_ref[...]

    a = jnp.maximum(jnp.dot(xn_ref[...], w1_ref[...], preferred_element_type=F32), 0.0)
    hout_ref[...] += jnp.dot((a * a).astype(BF16), w2_ref[...], preferred_element_type=F32)

    @pl.when(f == pl.num_programs(1) - 1)
    def _():
        xnout_ref[...] = _rms(hout_ref[...], g_ref[...]).astype(xnout_ref.dtype)


def _mlp(xn, h, w1, w2, layer, g, norm_dtype):
    m, d = xn.shape
    dff = w1.shape[2]
    tm = _row_tile(m, 768)
    tf = _col_tile(dff, 1024)
    return pl.pallas_call(
        _mlp_kernel,
        out_shape=(jax.ShapeDtypeStruct((m, d), F32), jax.ShapeDtypeStruct((m, d), norm_dtype)),
        grid=(m // tm, dff // tf),
        in_specs=[pl.BlockSpec((tm, d), lambda i, f: (i, 0)),
                  pl.BlockSpec((tm, d), lambda i, f: (i, 0), pipeline_mode=pl.Buffered(1)),
                  pl.BlockSpec((None, d, tf), lambda i, f: (layer, 0, f)),
                  pl.BlockSpec((None, tf, d), lambda i, f: (layer, f, 0)),
                  pl.BlockSpec((1, d), lambda i, f: (0, 0))],
        out_specs=(pl.BlockSpec((tm, d), lambda i, f: (i, 0)), pl.BlockSpec((tm, d), lambda i, f: (i, 0))),
        compiler_params=_params("parallel", "arbitrary"),
        name="mlp",
    )(xn, h, w1, w2, g.reshape(1, d))


def _even_layout(d_model):
    a_q = d_model // 2
    a_kv = A_KV_HEADS * HEAD_DIM
    idx_q = IDX_HEADS * IDX_DIM
    b_v = d_model // 2
    b_qk = b_v // 2
    sizes = dict(aq=a_q, ak=a_kv, av=a_kv, iq=idx_q, ik=IDX_DIM, iw=IDX_HEADS, bq=b_qk, bk=b_qk, bv=b_v,
                 bglr=B_GATE_RANK, br=b_v)
    start, off = {}, 0
    for name, size in sizes.items():
        start[name] = off
        off += size
    firsts = ("aq", "bq", "br")
    bounds = [start[n] for n in firsts] + [off]
    lay, groups = {}, []
    for g in range(len(firsts)):
        lo, hi = bounds[g], bounds[g + 1]
        width = _round_up(hi - lo, LANES)
        while _col_tile(width, MATMUL_COL_TILE) < MATMUL_COL_TILE // 2:
            width += LANES
        groups.append((lo, hi, width))
        for name, size in sizes.items():
            if lo <= start[name] < hi:
                lay[name] = (g, start[name] - lo, size)
    lay["misc"] = (lay["ik"][0], lay["ik"][1], LANES)
    lay["glr"] = (lay["bglr"][0], lay["bglr"][1], LANES)
    for name in ("aq", "ak", "av", "iq", "misc", "bq", "bk", "bv", "glr", "br"):
        assert lay[name][1] % LANES == 0
    return lay, groups


def _rope_freqs():
    def inv(d):
        return ROPE_THETA ** (-jnp.arange(d // 2, dtype=F32) * (2.0 / d))
    pad = jnp.zeros((LANES - HEAD_DIM // 2 - IDX_DIM // 2,), F32)
    return jnp.concatenate([inv(HEAD_DIM), inv(IDX_DIM), pad]).reshape(1, LANES)


def _rope_kernel(*refs, n_aq, n_iq, n_ak, n_iq_in, n_prompt_rows, past_len, dec_seq):
    q_ref, kv_ref = refs[:2]
    iq_in = refs[2:2 + n_iq_in]
    misc_in, inv_ref = refs[2 + n_iq_in:4 + n_iq_in]
    aq_ref, iq_ref, misc_ref, ak_ref, av_ref, ik_ref, kbf_ref, vbf_ref, ikbf_ref = refs[4 + n_iq_in:]
    tr = q_ref.shape[0]
    iq_per_in = n_iq // n_iq_in
    half, quarter = HEAD_DIM // 2, IDX_DIM // 2
    row = pl.program_id(0) * tr + lax.broadcasted_iota(I32, (tr, 1), 0)
    pos = jnp.where(row < n_prompt_rows, row, past_len + (row - n_prompt_rows) % dec_seq).astype(F32)
    ang = pos * inv_ref[...]
    c, s = jnp.cos(ang), jnp.sin(ang)
    cos128 = jnp.concatenate([c[:, :half]] * 2, axis=1)
    sin128 = jnp.concatenate([-s[:, :half], s[:, :half]], axis=1)
    c_i, s_i = c[:, half:half + quarter], s[:, half:half + quarter]
    cos64 = jnp.concatenate([c_i] * (LANES // quarter), axis=1)
    sin64 = jnp.concatenate([-s_i, s_i] * (LANES // IDX_DIM), axis=1)

    def rope128(x):
        return x * cos128 + pltpu.roll(x, half, 1) * sin128

    lane = lax.broadcasted_iota(I32, (tr, LANES), 1)
    low_half = (lane % IDX_DIM) < quarter

    def rope64(x):
        rot = jnp.where(low_half, pltpu.roll(x, LANES - quarter, 1), pltpu.roll(x, quarter, 1))
        return x * cos64 + rot * sin64

    n_blk = tr // ROW_BLOCK

    def put_transposed(ref, c, x):
        for b in range(n_blk):
            tile = jnp.transpose(x[b * ROW_BLOCK:(b + 1) * ROW_BLOCK, :])
            ref[b, c * LANES:(c + 1) * LANES, :] = tile.astype(ref.dtype)

    for c in range(n_aq):
        sl = slice(c * LANES, (c + 1) * LANES)
        put_transposed(aq_ref, c, rope128(q_ref[:, sl]) * EXP2_SCALE)
    for c in range(n_iq):
        src = iq_in[c // iq_per_in]
        put_transposed(iq_ref, c, rope64(src[:, (c % iq_per_in) * LANES:(c % iq_per_in + 1) * LANES]))
    put_transposed(misc_ref, 0, misc_in[...])
    for c in range(n_ak):
        sl = slice(c * LANES, (c + 1) * LANES)
        kr = rope128(kv_ref[:, sl])
        vr = kv_ref[:, n_ak * LANES + c * LANES:n_ak * LANES + (c + 1) * LANES]
        ak_ref[pl.ds(c, tr, stride=n_ak), :] = kr
        av_ref[pl.ds(c, tr, stride=n_ak), :] = vr
        kbf_ref[:, sl] = kr.astype(BF16)
        vbf_ref[:, sl] = vr.astype(BF16)
    ik = rope64(misc_in[...])
    ik_ref[...] = ik
    ikbf_ref[...] = ik[:, :IDX_DIM].astype(BF16)


def _rope(p0, lay, n_prompt_rows, past_len, dec_seq):
    m = p0.shape[0]
    tr = _row_tile(m, 384)
    a_q, a_kv, idx_q = lay["aq"][2], lay["ak"][2], lay["iq"][2]
    n_ak = a_kv // LANES
    assert lay["av"][1] == lay["ak"][1] + a_kv and lay["ak"][1] % (2 * a_kv) == 0 and lay["aq"][1] % a_q == 0
    iq_w = math.gcd(lay["iq"][1], idx_q)
    n_iq_in = idx_q // iq_w
    kern = functools.partial(_rope_kernel, n_aq=a_q // LANES, n_iq=idx_q // LANES, n_ak=n_ak, n_iq_in=n_iq_in,
                             n_prompt_rows=n_prompt_rows, past_len=past_len, dec_seq=dec_seq)
    cols = lambda off, width: (lambda i: (i, off // width))
    in_specs = [pl.BlockSpec((tr, a_q), cols(lay["aq"][1], a_q)),
                pl.BlockSpec((tr, 2 * a_kv), cols(lay["ak"][1], 2 * a_kv))]
    in_specs += [pl.BlockSpec((tr, iq_w), cols(lay["iq"][1] + j * iq_w, iq_w)) for j in range(n_iq_in)]
    in_specs += [pl.BlockSpec((tr, LANES), cols(lay["misc"][1], LANES)), pl.BlockSpec((1, LANES), lambda i: (0, 0))]
    row = lambda i: (i, 0)
    blk = lambda i: (i, 0, 0)
    nb, tb = m // ROW_BLOCK, tr // ROW_BLOCK
    return pl.pallas_call(
        kern,
        out_shape=(jax.ShapeDtypeStruct((nb, a_q, ROW_BLOCK), BF16), jax.ShapeDtypeStruct((nb, idx_q, ROW_BLOCK), BF16),
                   jax.ShapeDtypeStruct((nb, LANES, ROW_BLOCK), F32),
                   jax.ShapeDtypeStruct((m * n_ak, LANES), F32), jax.ShapeDtypeStruct((m * n_ak, LANES), F32),
                   jax.ShapeDtypeStruct((m, LANES), F32),
                   jax.ShapeDtypeStruct((m, a_kv), BF16), jax.ShapeDtypeStruct((m, a_kv), BF16),
                   jax.ShapeDtypeStruct((m, IDX_DIM), BF16)),
        grid=(m // tr,),
        in_specs=in_specs,
        out_specs=(pl.BlockSpec((tb, a_q, ROW_BLOCK), blk), pl.BlockSpec((tb, idx_q, ROW_BLOCK), blk),
                   pl.BlockSpec((tb, LANES, ROW_BLOCK), blk),
                   pl.BlockSpec((tr * n_ak, LANES), row), pl.BlockSpec((tr * n_ak, LANES), row),
                   pl.BlockSpec((tr, LANES), row), pl.BlockSpec((tr, a_kv), row), pl.BlockSpec((tr, a_kv), row),
                   pl.BlockSpec((tr, IDX_DIM), row)),
        compiler_params=_params("parallel"),
        name="rope",
    )(*([p0] * (3 + n_iq_in)), _rope_freqs())


GLA_SUB = 16


def _split_bf16(x):
    hi = x.astype(BF16)
    return hi, (x - hi.astype(F32)).astype(BF16)


def _log_sigmoid(x):
    return jnp.minimum(x, 0.0) - jnp.log(1.0 + jnp.exp(-jnp.abs(x)))


def _dot(a, b):
    return jnp.dot(a, b, preferred_element_type=F32)


def _dot_nt(a, b):
    return lax.dot_general(a, b, (((1,), (1,)), ((), ())), preferred_element_type=F32)


def _dot_tn(a, b):
    return lax.dot_general(a, b, (((0,), (0,)), ((), ())), preferred_element_type=F32)


def _gla_gate(misc, w2_hi, w2_lo, bias, valid):
    m_hi, m_lo = _split_bf16(misc)
    x = _dot(m_hi, w2_hi) + _dot(m_lo, w2_hi) + _dot(m_hi, w2_lo) + bias
    return jnp.where(valid, _log_sigmoid(x) * (1.0 / B_GATE_TAU), 0.0)


def _gla_chunks(chunks, state_ref, dk, dv):
    heads = range(B_HEADS)
    local = [_gla_local(q, k, v, g, dk, dv) for q, k, v, g in chunks]
    states = [state_ref[h] for h in heads]
    outs = []
    for intra, q_in, grow, decay in local:
        outs.append([intra[h] + _dot(q_in[h], states[h].astype(BF16)) for h in heads])
        states = [decay[h] * states[h] + grow[h] for h in heads]
    for h in heads:
        state_ref[h] = states[h]
    return outs


def _gla_local(q, k, v, g, dk, dv):
    c = q.shape[0]
    n_sub = c // GLA_SUB
    ri = lax.broadcasted_iota(I32, (c, c), 0)
    ci = lax.broadcasted_iota(I32, (c, c), 1)
    causal = ri >= ci
    tri = jnp.where(causal, 1.0, 0.0).astype(BF16)
    gcum = _dot(jnp.concatenate([tri, tri], axis=1), jnp.concatenate(_split_bf16(g), axis=0))
    scale = dk ** -0.5
    pad = _round_up(c, LANES)
    heads = range(B_HEADS)
    gh = [gcum[:, h * dk:(h + 1) * dk] for h in heads]
    qh = [q[:, h * dk:(h + 1) * dk] * scale for h in heads]
    kh = [k[:, h * dk:(h + 1) * dk] for h in heads]
    vh = [v[:, h * dv:(h + 1) * dv].astype(BF16) for h in heads]
    last = [gh[h][c - 1:c, :] for h in heads]
    qts, kts = [], []
    for h in heads:
        q_parts, k_parts = [], []
        for i in range(n_sub):
            lo, hi = i * GLA_SUB, (i + 1) * GLA_SUB
            base = gh[h][lo - 1:lo, :] if i else jnp.zeros((1, dk), F32)
            q_parts.append(qh[h][lo:hi] * jnp.exp(gh[h][lo:hi] - base))
            k_parts.append(kh[h][:hi] * jnp.exp(base - gh[h][:hi]))
            if pad > hi:
                k_parts.append(jnp.zeros((pad - hi, dk), F32))
        qts.append(jnp.concatenate(q_parts, axis=0).astype(BF16))
        kts.append(jnp.concatenate(k_parts, axis=0).astype(BF16))
    q_in = [(qh[h] * jnp.exp(gh[h])).astype(BF16) for h in heads]
    k_tail = [(kh[h] * jnp.exp(last[h] - gh[h])).astype(BF16) for h in heads]
    rs = [_dot_nt(qts[h], kts[h]) for h in heads]
    grow = [_dot_tn(k_tail[h], vh[h]) for h in heads]
    intra, decay = [], []
    for h in heads:
        att = jnp.concatenate([rs[h][i * GLA_SUB:(i + 1) * GLA_SUB, i * pad:i * pad + c] for i in range(n_sub)],
                              axis=0)
        att = jnp.where(causal, att, 0.0).astype(BF16)
        intra.append(_dot(att, vh[h]))
        d = jnp.transpose(jnp.broadcast_to(jnp.exp(last[h]), (dk, dk)))
        decay.append(jnp.concatenate([d] * (dv // dk), axis=1))
    return intra, q_in, grow, decay


def _gla_out(outs, r, head_norm, dv):
    res = []
    for h, o in enumerate(outs):
        gate = r[:, h * dv:(h + 1) * dv]
        gate = gate * (1.0 / (1.0 + jnp.exp(-gate)))
        res.append(_rms(o, head_norm) * gate)
    return jnp.concatenate(res, axis=1)


def _gla_prompt_kernel(q_ref, k_ref, v_ref, r_ref, misc_ref, w2h_ref, w2l_ref, b_ref, hn_ref, o_ref, state_ref,
                       *, chunk, n_valid, dk, dv):
    blk = pl.program_id(0)

    @pl.when(blk == 0)
    def _():
        state_ref[...] = jnp.zeros_like(state_ref)

    rows = q_ref.shape[0]
    row = blk * rows + lax.broadcasted_iota(I32, (rows, 1), 0)
    g = _gla_gate(misc_ref[...], w2h_ref[...], w2l_ref[...], b_ref[...], row < n_valid)
    slices = [slice(c0, c0 + chunk) for c0 in range(0, rows, chunk)]
    outs = _gla_chunks([(q_ref[sl, :], k_ref[sl, :], v_ref[sl, :], g[sl, :]) for sl in slices], state_ref, dk, dv)
    for sl, out in zip(slices, outs):
        o_ref[sl, :] = _gla_out(out, r_ref[sl, :], hn_ref[...], dv).astype(o_ref.dtype)


def _gla_sample_kernel(q_ref, k_ref, v_ref, r_ref, misc_ref, w2h_ref, w2l_ref, b_ref, hn_ref, s_ref, o_ref,
                       state_ref, *, n_valid, dk, dv):
    state_ref[0] = s_ref[0]
    rows = q_ref.shape[1]
    row = lax.broadcasted_iota(I32, (rows, 1), 0)
    g = _gla_gate(misc_ref[0], w2h_ref[...], w2l_ref[...], b_ref[...], row < n_valid)
    outs, = _gla_chunks([(q_ref[0], k_ref[0], v_ref[0], g)], state_ref.at[0], dk, dv)
    o_ref[0] = _gla_out(outs, r_ref[0], hn_ref[...], dv).astype(o_ref.dtype)


def _gate_weights(w_b_gate2, b_b_gate, lay):
    off = lay["bglr"][1] - lay["glr"][1]
    w2 = jnp.zeros((LANES, w_b_gate2.shape[1]), F32).at[off:off + B_GATE_RANK].set(w_b_gate2)
    hi = w2.astype(BF16)
    lo = (w2 - hi.astype(F32)).astype(BF16)
    return hi, lo, b_b_gate.reshape(1, -1)


def _gla_prompt(proj, lay, n_rows, n_valid, gate_w, head_norm, chunk=64):
    bqk, bv = lay["bq"][2], lay["bv"][2]
    dk, dv = bqk // B_HEADS, bv // B_HEADS
    w2h, w2l, bias = gate_w
    names = ("bq", "bk", "bv", "br", "glr")
    for name in names:
        assert lay[name][1] % lay[name][2] == 0
    col = lambda name: (lambda i, o=lay[name][1] // lay[name][2]: (i, o))
    const = lambda i: (0, 0)
    kern = functools.partial(_gla_prompt_kernel, chunk=chunk, n_valid=n_valid, dk=dk, dv=dv)
    return pl.pallas_call(
        kern,
        out_shape=(jax.ShapeDtypeStruct((proj[0].shape[0], bv), BF16), jax.ShapeDtypeStruct((B_HEADS, dk, dv), F32)),
        grid=(n_rows // ROW_BLOCK,),
        in_specs=[pl.BlockSpec((ROW_BLOCK, bqk), col("bq")), pl.BlockSpec((ROW_BLOCK, bqk), col("bk")),
                  pl.BlockSpec((ROW_BLOCK, bv), col("bv")), pl.BlockSpec((ROW_BLOCK, bv), col("br")),
                  pl.BlockSpec((ROW_BLOCK, LANES), col("glr")),
                  pl.BlockSpec(w2h.shape, const), pl.BlockSpec(w2l.shape, const), pl.BlockSpec(bias.shape, const),
                  pl.BlockSpec((1, dv), const)],
        out_specs=(pl.BlockSpec((ROW_BLOCK, bv), lambda i: (i, 0)),
                   pl.BlockSpec((B_HEADS, dk, dv), lambda i: (0, 0, 0))),
        compiler_params=_params("arbitrary"),
        name="gla_prompt",
    )(*[proj[lay[name][0]] for name in names], w2h, w2l, bias, head_norm.reshape(1, dv))


def _gla_sample(q, k, v, r, misc, state, n_valid, gate_w, head_norm):
    db, rows, bqk = q.shape
    bv = v.shape[2]
    dk, dv = bqk // B_HEADS, bv // B_HEADS
    w2h, w2l, bias = gate_w
    seq = lambda i: (i, 0, 0)
    const = lambda i: (0, 0)
    kern = functools.partial(_gla_sample_kernel, n_valid=n_valid, dk=dk, dv=dv)
    return pl.pallas_call(
        kern,
        out_shape=(jax.ShapeDtypeStruct((db, rows, bv), BF16), jax.ShapeDtypeStruct(state.shape, F32)),
        grid=(db,),
        in_specs=[pl.BlockSpec((1, rows, bqk), seq), pl.BlockSpec((1, rows, bqk), seq),
                  pl.BlockSpec((1, rows, bv), seq), pl.BlockSpec((1, rows, bv), seq),
                  pl.BlockSpec((1, rows, LANES), seq),
                  pl.BlockSpec(w2h.shape, const), pl.BlockSpec(w2l.shape, const), pl.BlockSpec(bias.shape, const),
                  pl.BlockSpec((1, dv), const),
                  pl.BlockSpec((1, B_HEADS, dk, dv), lambda i: (i, 0, 0, 0))],
        out_specs=(pl.BlockSpec((1, rows, bv), seq), pl.BlockSpec((1, B_HEADS, dk, dv), lambda i: (i, 0, 0, 0))),
        compiler_params=_params("parallel"),
        name="gla_sample",
    )(q, k, v, r, misc, w2h, w2l, bias, head_norm.reshape(1, dv), state)


SB_DONE = -105.0
SB_Q_SCALE = HEAD_DIM ** -0.5 * math.log2(math.e)
SB_DONE2 = SB_DONE * math.log2(math.e)


def _sb_suffix_matrix():
    j = lax.broadcasted_iota(I32, (2 * LANES, 2 * LANES), 0) % LANES
    s = lax.broadcasted_iota(I32, (2 * LANES, 2 * LANES), 1)
    return jnp.where((j > s) | (s >= LANES), 1.0, 0.0).astype(BF16)


def _sb_blocks(qs, ks, vs, strict, carries, accs, suffix):
    zs = [_dot_nt(q, k) for q, k in zip(qs, ks)]
    keeps, takes = [], []
    for z in zs:
        soft = jnp.log2(1.0 + jnp.exp2(-jnp.abs(z)))
        log_keep = -(jnp.maximum(z, 0.0) + soft)
        keeps.append(log_keep if strict is None else jnp.where(strict, log_keep, 0.0))
        takes.append(jnp.minimum(z, 0.0) - soft)
    sums = []
    for lk in keeps:
        sums.append(_dot(jnp.concatenate(_split_bf16(lk), axis=1), suffix))
    ws = []
    for take, s, c in zip(takes, sums, carries):
        w = jnp.exp2(take + (s[:, :LANES] + c))
        ws.append((w if strict is None else jnp.where(strict, w, 0.0)).astype(BF16))
    new_carries = [c + s[:, LANES:] for c, s in zip(carries, sums)]
    new_accs = [a + _dot(w, v) for a, w, v in zip(accs, ws, vs)]
    return new_carries, new_accs


def _sb_prompt_kernel(q_ref, k_ref, v_ref, o_ref, *scratch, n_kv, rep):
    carry_refs, acc_refs = scratch[:n_kv], scratch[n_kv:]
    b = pl.program_id(0)
    suffix = _sb_suffix_matrix()
    for ref in scratch:
        ref[...] = jnp.zeros_like(ref)
    rows = rep * ROW_BLOCK
    q_row = lax.broadcasted_iota(I32, (rows, LANES), 0) % ROW_BLOCK
    col = lax.broadcasted_iota(I32, (rows, LANES), 1)
    head = lambda g: slice(g * HEAD_DIM, (g + 1) * HEAD_DIM)

    def visit(kb, strict):
        k0 = pl.multiple_of(kb * ROW_BLOCK, ROW_BLOCK)
        qs = [jnp.concatenate([q_ref[:, head(g * rep + r)] for r in range(rep)], axis=0) for g in range(n_kv)]
        ks = [k_ref[pl.ds(k0, ROW_BLOCK), head(g)] for g in range(n_kv)]
        vs = [v_ref[pl.ds(k0, ROW_BLOCK), head(g)] for g in range(n_kv)]
        carries, accs = _sb_blocks(qs, ks, vs, strict, [r[...] for r in carry_refs], [r[...] for r in acc_refs],
                                   suffix)
        top = None
        for g in range(n_kv):
            carry_refs[g][...] = carries[g]
            acc_refs[g][...] = accs[g]
            part = _col_reduce(carries[g], jnp.max)
            top = part if top is None else jnp.maximum(top, part)
        return (jnp.max(top) < SB_DONE2).astype(I32)

    done = visit(b, col < q_row)
    lax.while_loop(lambda st: (st[0] >= 0) & (st[1] == 0), lambda st: (st[0] - 1, visit(st[0], None)), (b - 1, done))
    for g in range(n_kv):
        for r in range(rep):
            o_ref[:, head(g * rep + r)] = acc_refs[g][r * ROW_BLOCK:(r + 1) * ROW_BLOCK, :].astype(o_ref.dtype)


def _sb_prompt(q, k, v, n_rows):
    cq = q.shape[1]
    ckv = k.shape[1]
    n_kv = ckv // HEAD_DIM
    rep = cq // ckv
    kern = functools.partial(_sb_prompt_kernel, n_kv=n_kv, rep=rep)
    return pl.pallas_call(
        kern,
        out_shape=jax.ShapeDtypeStruct(q.shape, BF16),
        grid=(n_rows // ROW_BLOCK,),
        in_specs=[pl.BlockSpec((ROW_BLOCK, cq), lambda i: (i, 0)),
                  pl.BlockSpec((n_rows, ckv), lambda i: (0, 0)),
                  pl.BlockSpec((n_rows, ckv), lambda i: (0, 0))],
        out_specs=pl.BlockSpec((ROW_BLOCK, cq), lambda i: (i, 0)),
        scratch_shapes=[pltpu.VMEM((rep * ROW_BLOCK, LANES), F32)] * n_kv
                       + [pltpu.VMEM((rep * ROW_BLOCK, HEAD_DIM), F32)] * n_kv,
        compiler_params=_params("parallel"),
        name="stick_prompt",
    )(q, k, v)


def _page_head(ref, g, n_heads):
    keys = ref.shape[1] // n_heads
    return ref[0, pl.ds(g, keys, stride=n_heads), :].astype(BF16)


def _sb_sample_kernel(pt_ref, q_ref, *refs, n_kv, n_new, dec_seq, n_pages_step, has_init):
    del pt_ref
    refs = list(refs)
    new_refs = [refs.pop(0), refs.pop(0)] if n_new else None
    k_pages = [refs.pop(0) for _ in range(n_pages_step)]
    v_pages = [refs.pop(0) for _ in range(n_pages_step)]
    init_refs = [refs.pop(0), refs.pop(0)] if has_init else None
    o_ref, carry_ref = refs
    suffix = _sb_suffix_matrix()
    rows = q_ref.shape[2]

    @pl.when(pl.program_id(1) == 0)
    def _():
        if has_init:
            o_ref[...] = init_refs[0][...]
            carry_ref[...] = init_refs[1][...]
        else:
            o_ref[...] = jnp.zeros_like(o_ref)
            carry_ref[...] = jnp.zeros_like(carry_ref)

    def visit(k_of, v_of, strict):
        @pl.when(jnp.max(carry_ref[...]) >= SB_DONE2)
        def _():
            heads = range(n_kv)
            carries, accs = _sb_blocks([q_ref[0, g] for g in heads], [k_of(g) for g in heads],
                                       [v_of(g) for g in heads], strict, [carry_ref[0, g] for g in heads],
                                       [o_ref[0, g] for g in heads], suffix)
            for g in heads:
                carry_ref[0, g] = carries[g]
                o_ref[0, g] = accs[g]

    cols = lambda ref, g: ref[0, :, g * HEAD_DIM:(g + 1) * HEAD_DIM].astype(BF16)
    if n_new:
        q_idx = lax.broadcasted_iota(I32, (rows, LANES), 0) % dec_seq
        col = lax.broadcasted_iota(I32, (rows, LANES), 1)
        visit(lambda g: cols(new_refs[0], g), lambda g: cols(new_refs[1], g), col < q_idx)
    for kp, vp in zip(k_pages, v_pages):
        visit(lambda g, kp=kp: _page_head(kp, g, n_kv), lambda g, vp=vp: _page_head(vp, g, n_kv), None)


def _sb_sample_call(page_table, q, new_kv, cache_k, cache_v, init, dec_seq, first_page, n_steps, pages_per_step):
    db, n_kv, rows, _ = q.shape
    page_rows = cache_k.shape[1]
    assert page_rows == LANES * n_kv
    seq4 = lambda b, c, pt: (b, 0, 0, 0)
    in_specs = [pl.BlockSpec((1, n_kv, rows, HEAD_DIM), seq4)]
    args = [q]
    if new_kv is not None:
        in_specs += [pl.BlockSpec((1,) + new_kv[0].shape[1:], lambda b, c, pt: (b, 0, 0))] * 2
        args += list(new_kv)
    for cache in (cache_k, cache_v):
        for i in range(pages_per_step):
            in_specs.append(pl.BlockSpec(
                (1, page_rows, HEAD_DIM), lambda b, c, pt, i=i: (pt[b, first_page - c * pages_per_step - i], 0, 0)))
            args.append(cache)
    if init is not None:
        in_specs += [pl.BlockSpec((1, n_kv, rows, HEAD_DIM), seq4)] * 2
        args += list(init)
    kern = functools.partial(_sb_sample_kernel, n_kv=n_kv, n_new=new_kv is not None, dec_seq=dec_seq,
                             n_pages_step=pages_per_step, has_init=init is not None)
    out_sds = jax.ShapeDtypeStruct((db, n_kv, rows, HEAD_DIM), F32)
    return pl.pallas_call(
        kern,
        out_shape=(out_sds, out_sds),
        grid_spec=pltpu.PrefetchScalarGridSpec(
            num_scalar_prefetch=1,
            grid=(db, n_steps),
            in_specs=in_specs,
            out_specs=(pl.BlockSpec((1, n_kv, rows, HEAD_DIM), seq4), pl.BlockSpec((1, n_kv, rows, HEAD_DIM), seq4)),
        ),
        compiler_params=_params("parallel", "arbitrary"),
        name="stick_sample",
    )(page_table, *args)


def _sortable(score):
    bits = lax.bitcast_convert_type(score + 0.0, I32)
    key = bits ^ (lax.shift_right_arithmetic(bits, 31) & 0x7FFFFFFF)
    return jnp.maximum(key, INT_MIN + 1)


COUNT_ROWS = 64


def _count(key_ref, n_units, unit, pred):
    rows = COUNT_ROWS if unit % COUNT_ROWS == 0 else SUBLANES

    def body(u, acc):
        r0 = pl.multiple_of(u * unit, unit)
        hit = pred(key_ref[pl.ds(r0, unit), :], r0)
        ones = jnp.where(hit, jnp.ones(hit.shape, I32), jnp.zeros(hit.shape, I32))
        return acc + jnp.sum(ones.reshape(unit // rows, rows, LANES), axis=0)
    acc = lax.fori_loop(0, n_units, body, jnp.zeros((rows, LANES), I32))
    return jnp.sum(acc, axis=0, keepdims=True)


def _select_topk(key_ref, n_units, unit, k, n_rows):
    def bit_step(i, state):
        tau, c_ge = state
        cand = tau ^ lax.shift_left(jnp.int32(1), 31 - i)
        cnt = _count(key_ref, n_units, unit, lambda keys, r0: keys >= cand)
        return jnp.where(cnt >= k, cand, tau), jnp.where(cnt >= k, cnt, c_ge)
    start = (jnp.full((1, LANES), INT_MIN, I32), jnp.full((1, LANES), n_units * unit, I32))
    tau, c_ge = lax.fori_loop(0, 32, bit_step, start)
    real = tau > INT_MIN
    tie = real & (c_ge > k)
    row = lax.broadcasted_iota(I32, (unit, LANES), 0)
    n_bits = max(1, (n_rows - 1).bit_length())

    def break_ties():
        need = k - _count(key_ref, n_units, unit, lambda keys, r0: keys > tau)

        def pos_step(i, x):
            cand = x | lax.shift_left(jnp.int32(1), n_bits - 1 - i)
            cnt = _count(key_ref, n_units, unit, lambda keys, r0: (keys == tau) & ((r0 + row) < cand))
            return jnp.where(cnt < need, cand, x)
        return lax.fori_loop(0, n_bits, pos_step, jnp.zeros((1, LANES), I32))

    last_tie = lax.cond(jnp.max(tie.astype(I32)) > 0, break_ties, lambda: jnp.zeros((1, LANES), I32))
    last = jnp.where(tie, last_tie, jnp.where(real, jnp.int32(2 ** 30), jnp.int32(-1)))
    return tau, last


def _col_reduce(x, fn):
    rows = x.shape[0]
    if rows > COUNT_ROWS and rows % COUNT_ROWS == 0:
        x = fn(x.reshape(rows // COUNT_ROWS, COUNT_ROWS, x.shape[1]), axis=0)
    return fn(x, axis=0, keepdims=True)


def _selected(keys, r0, tau, last):
    row = lax.broadcasted_iota(I32, keys.shape, 0)
    return (keys > tau) | ((keys == tau) & ((r0 + row) <= last))


IDX_SCALE = (IDX_HEADS * IDX_DIM) ** -0.5
MASKED = -1e30
EXP2_SCALE = HEAD_DIM ** -0.5 * math.log2(math.e)


def _dsa_prompt_kernel(aq_ref, iq_ref, misc_ref, ik_ref, k_ref, v_ref, o_ref, key_ref, *scratch,
                       unit, topk, n_valid, n_heads, n_kv, iw_row):
    b = pl.program_id(0)
    n_units = ((b + 1) * ROW_BLOCK + unit - 1) // unit
    rep = n_heads // n_kv
    row = lax.broadcasted_iota(I32, (unit, LANES), 0)
    q_pos = b * ROW_BLOCK + lax.broadcasted_iota(I32, (unit, LANES), 1)
    q_real = q_pos < n_valid

    iq_t = iq_ref[0]
    rhs = jnp.concatenate([iq_t[h * IDX_DIM:(h + 1) * IDX_DIM, :] for h in range(IDX_HEADS)], axis=1)
    w_t = misc_ref[0, iw_row:iw_row + IDX_HEADS, :]

    def score_unit(u, carry):
        r0 = pl.multiple_of(u * unit, unit)
        dots = _dot(ik_ref[pl.ds(r0, unit), :], rhs)
        score = jnp.zeros((unit, LANES), F32)
        for h in range(IDX_HEADS):
            score = score + jnp.maximum(dots[:, h * LANES:(h + 1) * LANES], 0.0) * w_t[h:h + 1, :]
        k_pos = r0 + row
        adm = (k_pos <= q_pos) & (q_real | (k_pos == 0))
        key_ref[pl.ds(r0, unit), :] = jnp.where(adm, _sortable(score * IDX_SCALE), INT_MIN)
        return carry

    lax.fori_loop(0, n_units, score_unit, 0)
    tau, last = _select_topk(key_ref, n_units, unit, topk, key_ref.shape[0])

    m_refs, l_refs, acc_refs, lg_refs = (scratch[i * n_kv:(i + 1) * n_kv] for i in range(4))
    for g in range(n_kv):
        m_refs[g][...] = jnp.full_like(m_refs[g], MASKED)
        l_refs[g][...] = jnp.zeros_like(l_refs[g])
        acc_refs[g][...] = jnp.zeros_like(acc_refs[g])
    aq_t = aq_ref[0]

    qgs = [jnp.concatenate([aq_t[(g * rep + r) * HEAD_DIM:(g * rep + r + 1) * HEAD_DIM, :] for r in range(rep)],
                           axis=1) for g in range(n_kv)]
    subs = [(s, ROW_BLOCK) for s in range(0, unit, ROW_BLOCK)]
    fold = lambda x, fn: fn(x.reshape(x.shape[0] // SUBLANES, SUBLANES, x.shape[1]), axis=0)

    def logits_of_unit(u, slot):
        r0 = pl.multiple_of(u * unit, unit)
        sel = _selected(key_ref[pl.ds(r0, unit), :], r0, tau, last)
        tops = [None] * n_kv
        for start, size in subs:
            sub = slice(start, start + size)
            for g in range(n_kv):
                lg = _dot(k_ref[pl.ds(r0 + start, size), g * HEAD_DIM:(g + 1) * HEAD_DIM], qgs[g])
                parts = []
                for r in range(rep):
                    x = jnp.where(sel[sub], lg[:, r * LANES:(r + 1) * LANES], MASKED)
                    lg_refs[g][slot, sub, r * LANES:(r + 1) * LANES] = x
                    parts.append(fold(x, jnp.max))
                part = jnp.concatenate(parts, axis=1)
                tops[g] = part if tops[g] is None else jnp.maximum(tops[g], part)
        return tuple(tops)

    def attend_unit(u, tops, prefetch=True):
        r0 = pl.multiple_of(u * unit, unit)
        slot = u % 2
        next_tops = logits_of_unit(u + 1, 1 - slot) if prefetch else tops
        m_old = [m_refs[g][...] for g in range(n_kv)]
        m_new = [jnp.maximum(m_old[g], jnp.max(tops[g], axis=0, keepdims=True)) for g in range(n_kv)]
        sums, pvs = [None] * n_kv, [None] * n_kv
        for start, size in subs:
            sub = slice(start, start + size)
            for g in range(n_kv):
                p = jnp.exp2(lg_refs[g][slot, sub, :] - m_new[g])
                pv = _dot_tn(v_ref[pl.ds(r0 + start, size), g * HEAD_DIM:(g + 1) * HEAD_DIM], p.astype(BF16))
                part = fold(p, jnp.sum)
                sums[g] = part if sums[g] is None else sums[g] + part
                pvs[g] = pv if pvs[g] is None else pvs[g] + pv
        for g in range(n_kv):
            alpha = jnp.exp2(m_old[g] - m_new[g])
            m_refs[g][...] = m_new[g]
            l_refs[g][...] = alpha * l_refs[g][...] + jnp.sum(sums[g], axis=0, keepdims=True)
            acc_refs[g][...] = acc_refs[g][...] * alpha + pvs[g]
        return next_tops

    last_tops = lax.fori_loop(0, n_units - 1, attend_unit, logits_of_unit(0, 0))
    attend_unit(n_units - 1, last_tops, prefetch=False)
    for g in range(n_kv):
        for r in range(rep):
            sl = slice(r * LANES, (r + 1) * LANES)
            out = acc_refs[g][:, sl] * (1.0 / l_refs[g][:, sl])
            h = g * rep + r
            o_ref[:, h * HEAD_DIM:(h + 1) * HEAD_DIM] = jnp.transpose(out).astype(o_ref.dtype)


def _dsa_prompt(aq_t, iq_t, misc_t, iw_row, ik, k, v, n_rows, n_valid, topk):
    n_blk = n_rows // ROW_BLOCK
    n_heads = aq_t.shape[1] // HEAD_DIM
    n_kv = k.shape[1] // HEAD_DIM
    rep = n_heads // n_kv
    unit = _row_tile(n_rows, 640)
    blk = lambda i: (i, 0, 0)
    const = lambda i: (0, 0)
    kern = functools.partial(_dsa_prompt_kernel, unit=unit, topk=topk, n_valid=n_valid, n_heads=n_heads, n_kv=n_kv,
                             iw_row=iw_row)
    return pl.pallas_call(
        kern,
        out_shape=jax.ShapeDtypeStruct((aq_t.shape[0] * ROW_BLOCK, n_heads * HEAD_DIM), BF16),
        grid=(n_blk,),
        in_specs=[pl.BlockSpec((1,) + aq_t.shape[1:], blk), pl.BlockSpec((1,) + iq_t.shape[1:], blk),
                  pl.BlockSpec((1,) + misc_t.shape[1:], blk),
                  pl.BlockSpec((n_rows, ik.shape[1]), const), pl.BlockSpec((n_rows, k.shape[1]), const),
                  pl.BlockSpec((n_rows, v.shape[1]), const)],
        out_specs=pl.BlockSpec((ROW_BLOCK, n_heads * HEAD_DIM), lambda i: (i, 0)),
        scratch_shapes=[pltpu.VMEM((n_rows, LANES), I32)]
                       + [pltpu.VMEM((1, rep * LANES), F32)] * (2 * n_kv)
                       + [pltpu.VMEM((HEAD_DIM, rep * LANES), F32)] * n_kv
                       + [pltpu.VMEM((2, unit, rep * LANES), F32)] * n_kv,
        compiler_params=_params("parallel"),
        name="dsa_prompt",
    )(aq_t, iq_t, misc_t, ik, k, v)


Q_PAD = SUBLANES


def _idx_sample_kernel(pt_ref, iq_ref, w_ref, new_ref, *refs, n_pages):
    del pt_ref
    pages, o_ref = refs[:n_pages], refs[n_pages]
    iq = iq_ref[0]
    w = w_ref[0]

    def score(keys_t):
        s = jnp.maximum(_dot(iq, keys_t), 0.0) * w
        return jnp.sum(s.reshape(IDX_HEADS, Q_PAD, LANES), axis=0) * IDX_SCALE

    for p in range(n_pages):
        o_ref[0, :, p * LANES:(p + 1) * LANES] = score(pages[p][0].astype(BF16))
    o_ref[0, :, n_pages * LANES:] = score(new_ref[0])


def _idx_sample(page_table, iq, w, new_ik_t, cache_idx_t):
    db = iq.shape[0]
    n_pages = page_table.shape[1]
    page = cache_idx_t.shape[2]
    assert page == LANES
    seq = lambda b, pt: (b, 0, 0)
    in_specs = [pl.BlockSpec((1,) + iq.shape[1:], seq), pl.BlockSpec((1,) + w.shape[1:], seq),
                pl.BlockSpec((1,) + new_ik_t.shape[1:], seq)]
    in_specs += [pl.BlockSpec((1, IDX_DIM, page), lambda b, pt, p=p: (pt[b, p], 0, 0)) for p in range(n_pages)]
    n_keys = (n_pages + 1) * LANES
    return pl.pallas_call(
        functools.partial(_idx_sample_kernel, n_pages=n_pages),
        out_shape=jax.ShapeDtypeStruct((db, Q_PAD, n_keys), F32),
        grid_spec=pltpu.PrefetchScalarGridSpec(
            num_scalar_prefetch=1, grid=(db,), in_specs=in_specs,
            out_specs=pl.BlockSpec((1, Q_PAD, n_keys), seq)),
        compiler_params=_params("parallel"),
        name="idx_sample",
    )(page_table, iq, w, new_ik_t, *([cache_idx_t] * n_pages))


def _topk_sample_kernel(s_ref, o_ref, key_ref, *, unit, topk, past_len, dec_seq):
    n_keys = s_ref.shape[0]
    n_units = n_keys // unit
    row = lax.broadcasted_iota(I32, (unit, LANES), 0)
    q = lax.broadcasted_iota(I32, (unit, LANES), 1) % Q_PAD
    for u in range(n_units):
        k_pos = u * unit + row
        adm = ((q < dec_seq) & (k_pos <= past_len + q)) | ((q >= dec_seq) & (k_pos == 0))
        key_ref[u * unit:(u + 1) * unit, :] = jnp.where(adm, _sortable(s_ref[u * unit:(u + 1) * unit, :]), INT_MIN)
    tau, last = _select_topk(key_ref, n_units, unit, topk, n_keys)
    for u in range(n_units):
        sel = _selected(key_ref[u * unit:(u + 1) * unit, :], u * unit, tau, last)
        o_ref[u * unit:(u + 1) * unit, :] = jnp.where(sel, 0.0, MASKED)


def _topk_sample(scores_t, topk, past_len, dec_seq):
    n_keys, n_q = scores_t.shape
    unit = _row_tile(n_keys, 640)
    kern = functools.partial(_topk_sample_kernel, unit=unit, topk=topk, past_len=past_len, dec_seq=dec_seq)
    return pl.pallas_call(
        kern,
        out_shape=jax.ShapeDtypeStruct((n_keys, n_q), F32),
        grid=(n_q // LANES,),
        in_specs=[pl.BlockSpec((n_keys, LANES), lambda i: (0, i))],
        out_specs=pl.BlockSpec((n_keys, LANES), lambda i: (0, i)),
        scratch_shapes=[pltpu.VMEM((n_keys, LANES), I32)],
        compiler_params=_params("parallel"),
        name="topk_sample",
    )(scores_t)


def _attn_sample_kernel(pt_ref, q_ref, bias_ref, bias_new_ref, k_new_ref, v_new_ref, *refs, n_kv, pages_per_step):
    del pt_ref
    k_pages = refs[:pages_per_step]
    v_pages = refs[pages_per_step:2 * pages_per_step]
    o_ref, m_ref, l_ref, acc_ref = refs[2 * pages_per_step:]
    c = pl.program_id(1)
    rep = q_ref.shape[2] // Q_PAD

    @pl.when(c == 0)
    def _():
        m_ref[...] = jnp.full_like(m_ref, MASKED)
        l_ref[...] = jnp.zeros_like(l_ref)
        acc_ref[...] = jnp.zeros_like(acc_ref)

    def update(g, kg, vg, bias):
        sel = jnp.concatenate([bias] * rep, axis=0) > 0.5 * MASKED
        lg = jnp.where(sel, _dot_nt(q_ref[0, g], kg), MASKED)
        m_old = m_ref[g]
        m_new = jnp.maximum(m_old, jnp.max(lg, axis=1, keepdims=True))
        alpha = jnp.exp2(m_old - m_new)
        p = jnp.where(sel, jnp.exp2(lg - m_new), 0.0)
        l_ref[g] = alpha * l_ref[g] + jnp.sum(p, axis=1, keepdims=True)
        acc_ref[g] = alpha * acc_ref[g] + _dot(p.astype(BF16), vg)
        m_ref[g] = m_new

    def head(ref, g):
        return ref[0, :, g * HEAD_DIM:(g + 1) * HEAD_DIM].astype(BF16)

    @pl.when(c == 0)
    def _():
        for g in range(n_kv):
            update(g, head(k_new_ref, g), head(v_new_ref, g), bias_new_ref[0])

    for g in range(n_kv):
        kg = jnp.concatenate([_page_head(r, g, n_kv) for r in k_pages], axis=0)
        vg = jnp.concatenate([_page_head(r, g, n_kv) for r in v_pages], axis=0)
        update(g, kg, vg, bias_ref[0])

    @pl.when(c == pl.num_programs(1) - 1)
    def _():
        for g in range(n_kv):
            o_ref[0, g] = acc_ref[g] * (1.0 / l_ref[g])


ATTN_PAGES_PER_STEP = 64


def _attn_sample(page_table, q, bias, new_k, new_v, cache_k, cache_v):
    db, n_kv, rows, _ = q.shape
    n_pages = page_table.shape[1]
    page_rows = cache_k.shape[1]
    page = page_rows // n_kv
    per_step = max(d for d in range(1, ATTN_PAGES_PER_STEP + 1) if n_pages % d == 0)
    seq4 = lambda b, c, pt: (b, 0, 0, 0)
    seq3 = lambda b, c, pt: (b, 0, 0)
    in_specs = [pl.BlockSpec((1, n_kv, rows, HEAD_DIM), seq4),
                pl.BlockSpec((1, Q_PAD, per_step * page), lambda b, c, pt: (b, 0, c)),
                pl.BlockSpec((1, Q_PAD, page), lambda b, c, pt: (b, 0, n_pages)),
                pl.BlockSpec((1,) + new_k.shape[1:], seq3), pl.BlockSpec((1,) + new_v.shape[1:], seq3)]
    for _ in range(2):
        in_specs += [pl.BlockSpec((1, page_rows, HEAD_DIM), lambda b, c, pt, i=i: (pt[b, c * per_step + i], 0, 0))
                     for i in range(per_step)]
    return pl.pallas_call(
        functools.partial(_attn_sample_kernel, n_kv=n_kv, pages_per_step=per_step),
        out_shape=jax.ShapeDtypeStruct((db, n_kv, rows, HEAD_DIM), F32),
        grid_spec=pltpu.PrefetchScalarGridSpec(
            num_scalar_prefetch=1, grid=(db, n_pages // per_step), in_specs=in_specs,
            out_specs=pl.BlockSpec((1, n_kv, rows, HEAD_DIM), seq4),
            scratch_shapes=[pltpu.VMEM((n_kv, rows, 1), F32), pltpu.VMEM((n_kv, rows, 1), F32),
                            pltpu.VMEM((n_kv, rows, HEAD_DIM), F32)]),
        compiler_params=_params("parallel", "arbitrary"),
        name="attn_sample",
    )(page_table, q, bias, bias, new_k, new_v, *([cache_k] * per_step), *([cache_v] * per_step))


SB_EAGER_PAGES = 4
SB_PAGES_PER_STEP = 12


def _sb_sample(page_table, q, new_k, new_v, cache_k, cache_v, dec_seq):
    n_pages = page_table.shape[1]
    eager = min(SB_EAGER_PAGES, n_pages)
    o, carry = _sb_sample_call(page_table, q, (new_k, new_v), cache_k, cache_v, None, dec_seq,
                               n_pages - 1, 1, eager)
    rest = n_pages - eager
    if rest == 0:
        return o
    per_step = max(d for d in range(1, SB_PAGES_PER_STEP + 1) if rest % d == 0)
    finish = lambda: _sb_sample_call(page_table, q, None, cache_k, cache_v, (o, carry), dec_seq,
                                     rest - 1, rest // per_step, per_step)[0]
    return lax.cond(jnp.max(carry) >= SB_DONE2, finish, lambda: o)


def _pad_rows(a, rows):
    return jnp.pad(a, ((0, 0), (0, rows - a.shape[1])) + ((0, 0),) * (a.ndim - 2))


def kernel(x_prompt, x_sample, cache_a_k, cache_a_v, cache_a_idx_k, state_b, cache_c_k, cache_c_v, page_table,
           meta_tokens, norm_mix, norm_ffn, norm_final, w_in0, w_b_gate2, b_b_gate, b_head_norm, w_out0,
           w_in1, w_out1, w_ff1, w_ff2):
    batch, seq, d = x_prompt.shape
    assert batch == 1
    db, ds, _ = x_sample.shape
    assert ds <= Q_PAD and ds <= GLA_SUB
    t_p = N_META + seq
    t_pad = _round_up(t_p, ROW_BLOCK)
    n_s = db * ds
    m_s = _round_up(n_s, ROW_BLOCK)
    n_blk = t_pad // ROW_BLOCK
    n_pool, page = cache_a_k.shape[:2]
    n_pages = page_table.shape[1]
    past_len = n_pages * page
    topk_p = min(TOPK_MAX, seq // 4)
    topk_s = min(TOPK_MAX, (past_len + ds) // 4)
    srows = slice(t_pad, t_pad + n_s)

    h = jnp.concatenate([meta_tokens.astype(F32), x_prompt[0], jnp.zeros((t_pad - t_p, d), F32),
                         x_sample.reshape(n_s, d), jnp.zeros((m_s - n_s, d), F32)], axis=0)

    lay, groups = _even_layout(d)
    gate_w = _gate_weights(w_b_gate2, b_b_gate, lay)

    xn = _rmsnorm(h, norm_mix[0], BF16)
    proj = [_matmul(xn, jnp.pad(w_in0[:, lo:hi].astype(BF16), ((0, 0), (0, width - (hi - lo)))), F32)
            for lo, hi, width in groups]
    seg = lambda name: proj[lay[name][0]][:, lay[name][1]:lay[name][1] + lay[name][2]]
    aq_t, iq_t, misc_t, ak, av, ik, k_bf, v_bf, ik_bf = _rope(proj[lay["aq"][0]], lay, t_pad, past_len, ds)
    a_kv = k_bf.shape[1]
    n_heads_a = aq_t.shape[1] // HEAD_DIM
    rep_a = n_heads_a // A_KV_HEADS
    iw_row = lay["iw"][1] - lay["misc"][1]

    oa = _dsa_prompt(aq_t, iq_t, misc_t, iw_row, ik_bf, k_bf, v_bf, t_pad, t_p, topk_p)
    ob, state_p = _gla_prompt(proj, lay, t_pad, t_p, gate_w, b_head_norm)

    sample_rows = lambda a_t: a_t[n_blk:].transpose(0, 2, 1).reshape(m_s, a_t.shape[1])[:n_s]
    aq, iq = sample_rows(aq_t), sample_rows(iq_t)
    iw = sample_rows(misc_t[:, iw_row:iw_row + IDX_HEADS])
    iq_s = iq.reshape(db, ds, IDX_HEADS, IDX_DIM).transpose(0, 2, 1, 3)
    iq_s = jnp.pad(iq_s, ((0, 0), (0, 0), (0, Q_PAD - ds), (0, 0))).reshape(db, IDX_HEADS * Q_PAD, IDX_DIM)
    iw_s = iw.reshape(db, ds, IDX_HEADS).transpose(0, 2, 1)
    iw_s = jnp.pad(iw_s, ((0, 0), (0, 0), (0, Q_PAD - ds))).reshape(db, IDX_HEADS * Q_PAD, 1)
    iw_s = jnp.broadcast_to(iw_s, (db, IDX_HEADS * Q_PAD, LANES))
    new_ik_t = _pad_rows(ik_bf[srows].reshape(db, ds, IDX_DIM), page).transpose(0, 2, 1)
    scores = _idx_sample(page_table, iq_s, iw_s, new_ik_t, cache_a_idx_k.transpose(0, 2, 1))
    n_keys = scores.shape[2]
    n_q = _round_up(db * Q_PAD, LANES)
    scores_t = jnp.pad(scores.reshape(db * Q_PAD, n_keys).T, ((0, 0), (0, n_q - db * Q_PAD)))
    bias = _topk_sample(scores_t, topk_s, past_len, ds)[:, :db * Q_PAD].T.reshape(db, Q_PAD, n_keys)
    aq_s = aq.reshape(db, ds, A_KV_HEADS, rep_a, HEAD_DIM).transpose(0, 2, 3, 1, 4)
    aq_s = jnp.pad(aq_s, ((0, 0), (0, 0), (0, 0), (0, Q_PAD - ds), (0, 0))).reshape(db, A_KV_HEADS, rep_a * Q_PAD, HEAD_DIM)
    new_k = _pad_rows(k_bf[srows].reshape(db, ds, a_kv), page)
    new_v = _pad_rows(v_bf[srows].reshape(db, ds, a_kv), page)
    oa_s = _attn_sample(page_table, aq_s, bias, new_k, new_v,
                        cache_a_k.reshape(n_pool, page * A_KV_HEADS, HEAD_DIM),
                        cache_a_v.reshape(n_pool, page * A_KV_HEADS, HEAD_DIM))
    oa_s = oa_s.reshape(db, A_KV_HEADS, rep_a, Q_PAD, HEAD_DIM)[:, :, :, :ds].transpose(0, 3, 1, 2, 4)
    oa_s = oa_s.reshape(n_s, n_heads_a * HEAD_DIM).astype(BF16)

    chunk_s = lambda name: _pad_rows(seg(name)[srows].reshape(db, ds, lay[name][2]), GLA_SUB)
    ob_s, state_s = _gla_sample(chunk_s("bq"), chunk_s("bk"), chunk_s("bv"), chunk_s("br"), chunk_s("glr"),
                                state_b, ds, gate_w, b_head_norm)
    ob_s = ob_s[:, :ds].reshape(n_s, -1)

    oa = oa.at[srows].set(oa_s)
    ob = ob.at[srows].set(ob_s)
    h, xn = _proj_res_norm([oa, ob], w_out0.astype(BF16), h, norm_ffn[0])
    w_ff1_bf, w_ff2_bf = w_ff1.astype(BF16), w_ff2.astype(BF16)
    h, xn = _mlp(xn, h, w_ff1_bf, w_ff2_bf, 0, norm_mix[1], BF16)

    c_kv = C_KV_HEADS * HEAD_DIM
    c_q = w_in1.shape[1] - 2 * c_kv
    rep_c = c_q // c_kv
    q1 = _matmul(xn, w_in1[:, :c_q].astype(BF16), BF16, scale=SB_Q_SCALE)
    ck, cv, ck_bf, cv_bf = _kv_proj(xn, w_in1[:, c_q:].astype(BF16), C_KV_HEADS)
    oc = _sb_prompt(q1, ck_bf, cv_bf, t_pad)
    q1_s = q1[srows].reshape(db, ds, C_KV_HEADS, rep_c, HEAD_DIM).transpose(0, 2, 3, 1, 4)
    q1_s = q1_s.reshape(db, C_KV_HEADS, rep_c * ds, HEAD_DIM)
    new_ck = _pad_rows(ck_bf[srows].reshape(db, ds, c_kv), page)
    new_cv = _pad_rows(cv_bf[srows].reshape(db, ds, c_kv), page)
    oc_s = _sb_sample(page_table, q1_s, new_ck, new_cv, cache_c_k.reshape(n_pool, page * C_KV_HEADS, HEAD_DIM),
                      cache_c_v.reshape(n_pool, page * C_KV_HEADS, HEAD_DIM), ds)
    oc_s = oc_s.reshape(db, C_KV_HEADS, rep_c, ds, HEAD_DIM).transpose(0, 3, 1, 2, 4).reshape(n_s, c_q).astype(BF16)
    h, xn = _proj_res_norm([oc.at[srows].set(oc_s)], w_out1.astype(BF16), h, norm_ffn[1])
    h, y = _mlp(xn, h, w_ff1_bf, w_ff2_bf, 1, norm_final, F32)

    y_prompt = y[N_META:t_p][None]
    y_sample = y[srows].reshape(db, ds, d)
    kv_p = lambda a, heads: a[:t_p * heads].reshape(1, t_p, heads, HEAD_DIM)
    kv_s = lambda a, heads: a[t_pad * heads:(t_pad + n_s) * heads].reshape(db, ds, heads, HEAD_DIM)
    return (y_prompt, y_sample,
            kv_p(ak, A_KV_HEADS), kv_p(av, A_KV_HEADS), ik[:t_p, :IDX_DIM][None], state_p[None],
            kv_p(ck, C_KV_HEADS), kv_p(cv, C_KV_HEADS),
            kv_s(ak, A_KV_HEADS), kv_s(av, A_KV_HEADS), ik[srows, :IDX_DIM].reshape(db, ds, IDX_DIM),
            state_s, kv_s(ck, C_KV_HEADS), kv_s(cv, C_KV_HEADS))
```
